```python
import math
import jax, jax.numpy as jnp
from jax import lax
import numpy as np

D_MODEL = 4096
BATCH = 4
SEQ = 2048
DEPTH = 4
DEC_BATCH = 1
DEC_SEQ = 8192
PAST_LEN = 128

GRID_W = 64
HEAD_DIM = 128
N_HEADS = D_MODEL // HEAD_DIM
NA_WIN_H = 8
NA_WIN_W = 16
DIFF_HEADS = D_MODEL // (2 * HEAD_DIM)
GQA_KV_HEADS = N_HEADS // 4
KV_DIM = GQA_KV_HEADS * HEAD_DIM
D_FF = 4 * D_MODEL
ROPE_THETA = 500000.0
PARTIAL_ROT = HEAD_DIM // 4
AXIAL_THETA = 10000.0
AXIAL_DIM = HEAD_DIM // 2
Q_BLOCK = 128
NORM_EPS = 1e-6
SUBLN_EPS = 1e-5
MIXER_PATTERN = ('nat', 'diff', 'gqa')

kernel_name = 'hybrid_nat_diff_axial_gqa_encoder'


def rms_norm(x, w, eps=NORM_EPS):
    xf = x.astype(jnp.float32)
    y = xf * lax.rsqrt(jnp.mean(xf * xf, axis=-1, keepdims=True) + eps)
    return (y * w.astype(jnp.float32)).astype(x.dtype)


def rope_angles(pos, dim, theta):
    inv = theta ** (-jnp.arange(0, dim, 2, dtype=jnp.float32) / dim)
    ang = pos.astype(jnp.float32)[:, None] * inv[None, :]
    return jnp.cos(ang), jnp.sin(ang)


def apply_rope(x, cos, sin):
    xf = x.astype(jnp.float32)
    half = x.shape[-1] // 2
    x1, x2 = xf[..., :half], xf[..., half:]
    c = cos[None, :, None, :]
    s = sin[None, :, None, :]
    return jnp.concatenate([x1 * c - x2 * s, x2 * c + x1 * s], axis=-1).astype(x.dtype)


def _split_blocks(q):
    b, s = q.shape[:2]
    qb = q.reshape((b, s // Q_BLOCK, Q_BLOCK) + q.shape[2:])
    return jnp.moveaxis(qb, 1, 0)


def _merge_blocks(ob):
    nb, b = ob.shape[:2]
    return jnp.moveaxis(ob, 0, 1).reshape((b, nb * Q_BLOCK) + ob.shape[3:])


def neighborhood_attention(h, w_qkv, rel_bias, w_o):
    b, s, d = h.shape
    rows = s // GRID_W
    kh = min(NA_WIN_H, rows)
    qkv = (h @ w_qkv).reshape(b, rows, GRID_W, 3, N_HEADS, HEAD_DIM)
    q, k, v = qkv[:, :, :, 0], qkv[:, :, :, 1], qkv[:, :, :, 2]
    scale = HEAD_DIM ** -0.5
    col = jnp.arange(GRID_W)
    col_start = jnp.clip(col - NA_WIN_W // 2, 0, GRID_W - NA_WIN_W)
    col_mask = (col[None, :] >= col_start[:, None]) & (col[None, :] < col_start[:, None] + NA_WIN_W)
    dc_idx = jnp.clip(col[None, :] - col[:, None], -(NA_WIN_W - 1), NA_WIN_W - 1) + NA_WIN_W - 1

    def row_block(r):
        r0 = jnp.clip(r - kh // 2, 0, rows - kh)
        q_r = lax.dynamic_index_in_dim(q, r, axis=1, keepdims=False)
        k_r = lax.dynamic_slice_in_dim(k, r0, kh, axis=1)
        v_r = lax.dynamic_slice_in_dim(v, r0, kh, axis=1)
        dr_idx = r0 + jnp.arange(kh) - r + NA_WIN_H - 1
        bias = rel_bias[:, dr_idx[:, None, None], dc_idx[None, :, :]]
        sc = jnp.einsum('bqhd,bjkhd->bhqjk', q_r, k_r).astype(jnp.float32) * scale
        sc = sc + jnp.transpose(bias, (0, 2, 1, 3))[None].astype(jnp.float32)
        sc = jnp.where(col_mask[None, None, :, None, :], sc, -jnp.inf)
        p = jax.nn.softmax(sc.reshape(b, N_HEADS, GRID_W, kh * GRID_W), axis=-1)
        p = p.reshape(sc.shape).astype(h.dtype)
        return jnp.einsum('bhqjk,bjkhd->bqhd', p, v_r)

    o = lax.map(row_block, jnp.arange(rows))
    o = jnp.moveaxis(o, 0, 1).reshape(b, s, d)
    return o @ w_o


def diff_attention(h, w_qkv, lq1, lk1, lq2, lk2, subln, w_o, lambda_init):
    b, s, d = h.shape
    nq = 2 * DIFF_HEADS
    qkv = h @ w_qkv
    q = qkv[..., :nq * HEAD_DIM].reshape(b, s, nq, HEAD_DIM)
    k = qkv[..., nq * HEAD_DIM:2 * nq * HEAD_DIM].reshape(b, s, nq, HEAD_DIM)
    v = qkv[..., 2 * nq * HEAD_DIM:].reshape(b, s, DIFF_HEADS, 2 * HEAD_DIM)
    cos, sin = rope_angles(jnp.arange(s), PARTIAL_ROT, ROPE_THETA)
    q = jnp.concatenate([apply_rope(q[..., :PARTIAL_ROT], cos, sin), q[..., PARTIAL_ROT:]], axis=-1)
    k = jnp.concatenate([apply_rope(k[..., :PARTIAL_ROT], cos, sin), k[..., PARTIAL_ROT:]], axis=-1)
    f32 = jnp.float32
    lam = (jnp.exp(jnp.sum(lq1.astype(f32) * lk1.astype(f32)))
           - jnp.exp(jnp.sum(lq2.astype(f32) * lk2.astype(f32))) + lambda_init)
    scale = HEAD_DIM ** -0.5

    def block(qb):
        sc = jnp.einsum('bqhd,bkhd->bhqk', qb, k).astype(f32) * scale
        p = jax.nn.softmax(sc, axis=-1).reshape(b, DIFF_HEADS, 2, Q_BLOCK, s)
        a = (p[:, :, 0] - lam * p[:, :, 1]).astype(h.dtype)
        return jnp.einsum('bhqk,bkhd->bqhd', a, v)

    o = _merge_blocks(lax.map(block, _split_blocks(q)))
    o = rms_norm(o, subln, SUBLN_EPS) * (1.0 - lambda_init)
    return o.reshape(b, s, d) @ w_o


def gqa_axial_attention(h, w_qkv, q_norm, k_norm, w_o):
    b, s, d = h.shape
    g = N_HEADS // GQA_KV_HEADS
    qkv = h @ w_qkv
    q = qkv[..., :D_MODEL].reshape(b, s, N_HEADS, HEAD_DIM)
    k = qkv[..., D_MODEL:D_MODEL + KV_DIM].reshape(b, s, GQA_KV_HEADS, HEAD_DIM)
    v = qkv[..., D_MODEL + KV_DIM:].reshape(b, s, GQA_KV_HEADS, HEAD_DIM)
    q = rms_norm(q, q_norm)
    k = rms_norm(k, k_norm)
    t = jnp.arange(s)
    rcos, rsin = rope_angles(t // GRID_W, AXIAL_DIM, AXIAL_THETA)
    ccos, csin = rope_angles(t % GRID_W, AXIAL_DIM, AXIAL_THETA)

    def axial(x):
        return jnp.concatenate([apply_rope(x[..., :AXIAL_DIM], rcos, rsin),
                                apply_rope(x[..., AXIAL_DIM:], ccos, csin)], axis=-1)

    q = axial(q).reshape(b, s, GQA_KV_HEADS, g, HEAD_DIM)
    k = axial(k)
    scale = HEAD_DIM ** -0.5

    def block(qb):
        sc = jnp.einsum('bqngd,bknd->bngqk', qb, k).astype(jnp.float32) * scale
        p = jax.nn.softmax(sc, axis=-1).astype(h.dtype)
        return jnp.einsum('bngqk,bknd->bqngd', p, v)

    o = _merge_blocks(lax.map(block, _split_blocks(q)))
    return o.reshape(b, s, d) @ w_o


def squared_relu_mlp(h, w_up, w_down):
    return jnp.square(jax.nn.relu(h @ w_up)) @ w_down


def lambda_init_fn(layer_idx):
    return 0.8 - 0.6 * math.exp(-0.3 * layer_idx)


def _trunk(x, layers, final_norm):
    for i in range(DEPTH):
        p = layers[i]
        kind = MIXER_PATTERN[i % len(MIXER_PATTERN)]
        h = rms_norm(x, p['norm_mix'])
        if kind == 'nat':
            x = x + neighborhood_attention(h, p['w_qkv'], p['rel_bias'], p['w_o'])
        elif kind == 'diff':
            x = x + diff_attention(h, p['w_qkv'], p['lq1'], p['lk1'], p['lq2'], p['lk2'],
                                   p['subln'], p['w_o'], lambda_init_fn(i))
        else:
            x = x + gqa_axial_attention(h, p['w_qkv'], p['q_norm'], p['k_norm'], p['w_o'])
        x = x + squared_relu_mlp(rms_norm(x, p['norm_mlp']), p['w_up'], p['w_down'])
    return rms_norm(x, final_norm)


def _w(key, shape, fan_in):
    return jax.random.normal(key, shape, jnp.float32) * (fan_in ** -0.5)


def _gain(key, n):
    return 1.0 + 0.02 * jax.random.normal(key, (n,), jnp.float32)


def setup_inputs(seed: int = 0) -> dict:
    key = jax.random.key(seed)
    ks = iter(jax.random.split(key, 64))
    d = D_MODEL
    out = {}
    out['x_prompt'] = jax.random.normal(next(ks), (BATCH, SEQ, d), jnp.float32)
    out['x_sample'] = jax.random.normal(next(ks), (DEC_BATCH, DEC_SEQ, d), jnp.float32)
    out['l0_norm_mix'] = _gain(next(ks), d)
    out['l0_w_qkv'] = _w(next(ks), (d, 3 * d), d)
    out['l0_rel_bias'] = 0.1 * jax.random.normal(next(ks), (N_HEADS, 2 * NA_WIN_H - 1, 2 * NA_WIN_W - 1), jnp.float32)
    out['l0_w_o'] = _w(next(ks), (d, d), d)
    out['l0_norm_mlp'] = _gain(next(ks), d)
    out['l0_w_up'] = _w(next(ks), (d, D_FF), d)
    out['l0_w_down'] = _w(next(ks), (D_FF, d), D_FF)
    out['l1_norm_mix'] = _gain(next(ks), d)
    out['l1_w_qkv'] = _w(next(ks), (d, 3 * d), d)
    out['l1_lambda_q1'] = 0.1 * jax.random.normal(next(ks), (HEAD_DIM,), jnp.float32)
    out['l1_lambda_k1'] = 0.1 * jax.random.normal(next(ks), (HEAD_DIM,), jnp.float32)
    out['l1_lambda_q2'] = 0.1 * jax.random.normal(next(ks), (HEAD_DIM,), jnp.float32)
    out['l1_lambda_k2'] = 0.1 * jax.random.normal(next(ks), (HEAD_DIM,), jnp.float32)
    out['l1_subln'] = _gain(next(ks), 2 * HEAD_DIM)
    out['l1_w_o'] = _w(next(ks), (d, d), d)
    out['l1_norm_mlp'] = _gain(next(ks), d)
    out['l1_w_up'] = _w(next(ks), (d, D_FF), d)
    out['l1_w_down'] = _w(next(ks), (D_FF, d), D_FF)
    out['l2_norm_mix'] = _gain(next(ks), d)
    out['l2_w_qkv'] = _w(next(ks), (d, d + 2 * KV_DIM), d)
    out['l2_q_norm'] = _gain(next(ks), HEAD_DIM)
    out['l2_k_norm'] = _gain(next(ks), HEAD_DIM)
    out['l2_w_o'] = _w(next(ks), (d, d), d)
    out['l2_norm_mlp'] = _gain(next(ks), d)
    out['l2_w_up'] = _w(next(ks), (d, D_FF), d)
    out['l2_w_down'] = _w(next(ks), (D_FF, d), D_FF)
    out['l3_norm_mix'] = _gain(next(ks), d)
    out['l3_w_qkv'] = _w(next(ks), (d, 3 * d), d)
    out['l3_rel_bias'] = 0.1 * jax.random.normal(next(ks), (N_HEADS, 2 * NA_WIN_H - 1, 2 * NA_WIN_W - 1), jnp.float32)
    out['l3_w_o'] = _w(next(ks), (d, d), d)
    out['l3_norm_mlp'] = _gain(next(ks), d)
    out['l3_w_up'] = _w(next(ks), (d, D_FF), d)
    out['l3_w_down'] = _w(next(ks), (D_FF, d), D_FF)
    out['final_norm'] = _gain(next(ks), d)
    return out


def reference(x_prompt, x_sample,
              l0_norm_mix, l0_w_qkv, l0_rel_bias, l0_w_o, l0_norm_mlp, l0_w_up, l0_w_down,
              l1_norm_mix, l1_w_qkv, l1_lambda_q1, l1_lambda_k1, l1_lambda_q2, l1_lambda_k2,
              l1_subln, l1_w_o, l1_norm_mlp, l1_w_up, l1_w_down,
              l2_norm_mix, l2_w_qkv, l2_q_norm, l2_k_norm, l2_w_o, l2_norm_mlp, l2_w_up, l2_w_down,
              l3_norm_mix, l3_w_qkv, l3_rel_bias, l3_w_o, l3_norm_mlp, l3_w_up, l3_w_down,
              final_norm):
    layers = [
        {'norm_mix': l0_norm_mix, 'w_qkv': l0_w_qkv, 'rel_bias': l0_rel_bias, 'w_o': l0_w_o,
         'norm_mlp': l0_norm_mlp, 'w_up': l0_w_up, 'w_down': l0_w_down},
        {'norm_mix': l1_norm_mix, 'w_qkv': l1_w_qkv, 'lq1': l1_lambda_q1, 'lk1': l1_lambda_k1,
         'lq2': l1_lambda_q2, 'lk2': l1_lambda_k2, 'subln': l1_subln, 'w_o': l1_w_o,
         'norm_mlp': l1_norm_mlp, 'w_up': l1_w_up, 'w_down': l1_w_down},
        {'norm_mix': l2_norm_mix, 'w_qkv': l2_w_qkv, 'q_norm': l2_q_norm, 'k_norm': l2_k_norm,
         'w_o': l2_w_o, 'norm_mlp': l2_norm_mlp, 'w_up': l2_w_up, 'w_down': l2_w_down},
        {'norm_mix': l3_norm_mix, 'w_qkv': l3_w_qkv, 'rel_bias': l3_rel_bias, 'w_o': l3_w_o,
         'norm_mlp': l3_norm_mlp, 'w_up': l3_w_up, 'w_down': l3_w_down},
    ]
    y_prompt = _trunk(x_prompt, layers, final_norm)
    y_sample = _trunk(x_sample, layers, final_norm)
    return (y_prompt, y_sample)
```

```python
import functools
import math

import numpy as np
import jax
import jax.numpy as jnp
from jax import lax
from jax.experimental import pallas as pl
from jax.experimental.pallas import tpu as pltpu

F32 = jnp.float32
BF16 = jnp.bfloat16

HEAD_DIM = 128
GRID_W = 64
NA_WIN_H = 8
NA_WIN_W = 16
GQA_GROUP = 4
ROPE_THETA = 500000.0
PARTIAL_ROT = HEAD_DIM // 4
AXIAL_THETA = 10000.0
AXIAL_DIM = HEAD_DIM // 2
NORM_EPS = 1e-6
SUBLN_EPS = 1e-5
NEG_BIG = -1e30

NAT_G = 4
NAT_KROWS = NAT_G + NA_WIN_H

V7X_VMEM_BYTES = 64 * 1024 * 1024
VMEM_CAP_BYTES = V7X_VMEM_BYTES - 6 * 1024 * 1024


def _params(semantics, block_bytes):
    limit = min(VMEM_CAP_BYTES, int(block_bytes * 1.2) + (8 << 20))
    return pltpu.CompilerParams(dimension_semantics=semantics, vmem_limit_bytes=limit)


def _divisor_block(n, target, align):
    best = None
    for b in range(align, min(n, target) + 1, align):
        if n % b == 0:
            best = b
    assert best is not None, (n, target, align)
    return best


def _dot_nt(a, b):
    return lax.dot_general(a, b, (((1,), (1,)), ((), ())), preferred_element_type=F32)


def _rmsnorm_kernel(x_ref, w_ref, o_ref, *, eps):
    x = x_ref[...]
    y = x * lax.rsqrt(jnp.mean(x * x, axis=-1, keepdims=True) + eps)
    o_ref[...] = (y * w_ref[...]).astype(o_ref.dtype)


def _rmsnorm(x, w, out_dtype, *, bt=256):
    t, d = x.shape
    bt = min(bt, t)
    blk = bt * d * (4 + jnp.dtype(out_dtype).itemsize) * 2
    return pl.pallas_call(
        functools.partial(_rmsnorm_kernel, eps=NORM_EPS),
        grid=(t // bt,),
        in_specs=[pl.BlockSpec((bt, d), lambda i: (i, 0)),
                  pl.BlockSpec((1, d), lambda i: (0, 0))],
        out_specs=pl.BlockSpec((bt, d), lambda i: (i, 0)),
        out_shape=jax.ShapeDtypeStruct((t, d), out_dtype),
        compiler_params=_params(("parallel",), blk),
        name="rmsnorm",
    )(x, w.reshape(1, d).astype(F32))


def _mm_kernel(*refs, nk, epilogue):
    if epilogue == "residual":
        a_ref, b_ref, r_ref, o_ref = refs[:4]
        rest = refs[4:]
    else:
        a_ref, b_ref, o_ref = refs[:3]
        r_ref = None
        rest = refs[3:]

    def finish(acc):
        if epilogue == "residual":
            o_ref[...] = r_ref[...] + acc
        elif epilogue == "relu2":
            o_ref[...] = jnp.square(jnp.maximum(acc, 0.0)).astype(o_ref.dtype)
        else:
            o_ref[...] = acc.astype(o_ref.dtype)

    part = jnp.dot(a_ref[...], b_ref[...], preferred_element_type=F32)
    if nk == 1:
        finish(part)
        return
    acc_ref = rest[0]
    k = pl.program_id(2)

    @pl.when(k == 0)
    def _():
        acc_ref[...] = part

    @pl.when(jnp.logical_and(k > 0, k < nk - 1))
    def _():
        acc_ref[...] += part

    @pl.when(k == nk - 1)
    def _():
        finish(acc_ref[...] + part)


def _matmul(a, b, *, epilogue="cast", residual=None, out_dtype=BF16, bm=1024, bn=1024, bk_max=4096):
    m, k = a.shape
    _, n = b.shape
    bm, bn = _divisor_block(m, bm, 8), _divisor_block(n, bn, HEAD_DIM)
    bk = k if k <= bk_max else _divisor_block(k, bk_max // 2, HEAD_DIM)
    nk = k // bk
    in_specs = [pl.BlockSpec((bm, bk), lambda i, j, kk: (i, kk)),
                pl.BlockSpec((bk, bn), lambda i, j, kk: (kk, j))]
    args = [a, b]
    osz = jnp.dtype(out_dtype).itemsize
    blk = 2 * (bm * bk * 2 + bk * bn * 2 + bm * bn * osz)
    if epilogue == "residual":
        in_specs.append(pl.BlockSpec((bm, bn), lambda i, j, kk: (i, j)))
        args.append(residual)
        blk += 2 * bm * bn * 4
    scratch = []
    if nk > 1:
        scratch.append(pltpu.VMEM((bm, bn), F32))
        blk += bm * bn * 4
    return pl.pallas_call(
        functools.partial(_mm_kernel, nk=nk, epilogue=epilogue),
        grid=(m // bm, n // bn, nk),
        in_specs=in_specs,
        out_specs=pl.BlockSpec((bm, bn), lambda i, j, kk: (i, j)),
        out_shape=jax.ShapeDtypeStruct((m, n), out_dtype),
        scratch_shapes=scratch,
        compiler_params=_params(("parallel", "parallel", "arbitrary"), blk),
        name="matmul_" + epilogue,
    )(*args)


def _rope_angles(pos, dim, theta):
    inv = theta ** (-jnp.arange(0, dim, 2, dtype=F32) / dim)
    ang = pos.astype(F32)[:, None] * inv[None, :]
    return jnp.cos(ang), jnp.sin(ang)


def _partial_rope_tables(pos):
    cos, sin = _rope_angles(pos, PARTIAL_ROT, ROPE_THETA)
    t, h = cos.shape
    rest = HEAD_DIM - 2 * h
    c = jnp.concatenate([cos, cos, jnp.ones((t, rest), F32)], axis=1)
    sa = jnp.concatenate([-sin, jnp.zeros((t, HEAD_DIM - h), F32)], axis=1)
    sb = jnp.concatenate([jnp.zeros((t, h), F32), sin, jnp.zeros((t, rest), F32)], axis=1)
    return c, sa, sb, h


def _axial_rope_tables(pos):
    rcos, rsin = _rope_angles(pos // GRID_W, AXIAL_DIM, AXIAL_THETA)
    ccos, csin = _rope_angles(pos % GRID_W, AXIAL_DIM, AXIAL_THETA)
    z = jnp.zeros_like(rsin)
    c = jnp.concatenate([rcos, rcos, ccos, ccos], axis=1)
    sa = jnp.concatenate([-rsin, z, -csin, z], axis=1)
    sb = jnp.concatenate([z, rsin, z, csin], axis=1)
    return c, sa, sb, rcos.shape[1]


def _prep_kernel(*refs, heads_per_block, shift, n_q_blocks, scale, use_norm):
    if use_norm:
        x_ref, c_ref, sa_ref, sb_ref, w_ref, o_ref = refs
    else:
        x_ref, c_ref, sa_ref, sb_ref, o_ref = refs
    is_q = pl.program_id(1) < n_q_blocks
    sc = jnp.where(is_q, scale, 1.0).astype(F32)
    c = c_ref[...]
    sa = sa_ref[...]
    sb = sb_ref[...]
    if use_norm:
        w = jnp.where(is_q, w_ref[0:1, :], w_ref[1:2, :])
    for hh in range(heads_per_block):
        lanes = slice(hh * HEAD_DIM, (hh + 1) * HEAD_DIM)
        x = x_ref[:, lanes].astype(F32)
        if use_norm:
            x = x * lax.rsqrt(jnp.mean(x * x, axis=-1, keepdims=True) + NORM_EPS) * w
        y = x * c + pltpu.roll(x, HEAD_DIM - shift, 1) * sa + pltpu.roll(x, shift, 1) * sb
        o_ref[:, lanes] = (y * sc).astype(o_ref.dtype)


def _prep_qk(qkv, tables, *, n_q_cols, n_k_cols, norm_w=None, bt=512, bc=1024):
    t = qkv.shape[0]
    c, sa, sb, shift = tables
    bt = min(bt, t)
    bc = math.gcd(math.gcd(bc, n_q_cols), n_k_cols)
    n_cols = n_q_cols + n_k_cols
    tab_spec = pl.BlockSpec((bt, HEAD_DIM), lambda i, j: (i, 0))
    in_specs = [pl.BlockSpec((bt, bc), lambda i, j: (i, j)), tab_spec, tab_spec, tab_spec]
    args = [qkv, c, sa, sb]
    if norm_w is not None:
        in_specs.append(pl.BlockSpec((2, HEAD_DIM), lambda i, j: (0, 0)))
        args.append(norm_w)
    blk = 2 * (2 * bt * bc * 2 + 3 * bt * HEAD_DIM * 4)
    return pl.pallas_call(
        functools.partial(_prep_kernel, heads_per_block=bc // HEAD_DIM, shift=shift,
                          n_q_blocks=n_q_cols // bc, scale=HEAD_DIM ** -0.5, use_norm=norm_w is not None),
        grid=(t // bt, n_cols // bc),
        in_specs=in_specs,
        out_specs=pl.BlockSpec((bt, bc), lambda i, j: (i, j)),
        out_shape=jax.ShapeDtypeStruct((t, n_cols), BF16),
        compiler_params=_params(("parallel", "arbitrary"), blk),
        name="prep_qk",
    )(*args)


def _attend(score_fn, v_ref, m_rows, dv, n_chunks, bk):
    def body(ci, carry):
        m, l, acc = carry
        start = pl.multiple_of(ci * bk, bk)
        s = score_fn(start)
        m_new = jnp.maximum(m, jnp.max(s, axis=1, keepdims=True))
        alpha = jnp.exp(m - m_new)
        p = jnp.exp(s - m_new)
        l = alpha * l + jnp.sum(p, axis=1, keepdims=True)
        pv = jnp.dot(p.astype(BF16), v_ref[pl.ds(start, bk), :], preferred_element_type=F32)
        return m_new, l, alpha * acc + pv

    init = (jnp.full((m_rows, 1), NEG_BIG, F32), jnp.zeros((m_rows, 1), F32), jnp.zeros((m_rows, dv), F32))
    _, l, acc = lax.fori_loop(0, n_chunks, body, init)
    return acc / l


def _diff_kernel(q_ref, k_ref, v_ref, lq1_ref, lk1_ref, lq2_ref, lk2_ref, sub_ref, o_ref, *,
                 seq, bq, bk, lambda_init):
    q1 = q_ref[:, 0:HEAD_DIM]
    q2 = q_ref[:, HEAD_DIM:2 * HEAD_DIM]

    def score(start):
        s1 = _dot_nt(q1, k_ref[pl.ds(start, bk), 0:HEAD_DIM])
        s2 = _dot_nt(q2, k_ref[pl.ds(start, bk), HEAD_DIM:2 * HEAD_DIM])
        return jnp.concatenate([s1, s2], axis=0)

    o = _attend(score, v_ref, 2 * bq, 2 * HEAD_DIM, seq // bk, bk)
    lam = (jnp.exp(jnp.sum(lq1_ref[...] * lk1_ref[...], axis=-1, keepdims=True))
           - jnp.exp(jnp.sum(lq2_ref[...] * lk2_ref[...], axis=-1, keepdims=True)) + lambda_init)
    d = o[:bq] - lam * o[bq:]
    y = d * lax.rsqrt(jnp.mean(d * d, axis=-1, keepdims=True) + SUBLN_EPS) * sub_ref[...]
    o_ref[...] = (y * (1.0 - lambda_init)).astype(o_ref.dtype)


def _diff_attention(qk3, qkv3, b_off, nb, lam_params, subln, lambda_init, *, bq=256, bk=512):
    _, seq, two_d = qk3.shape
    d = two_d // 2
    n_pairs = d // (2 * HEAD_DIM)
    w = 2 * HEAD_DIM
    bq, bk = min(bq, seq), min(bk, seq)
    vec = pl.BlockSpec((1, HEAD_DIM), lambda b, t, i: (0, 0))
    blk = 2 * (2 * seq * w * 2 + 2 * bq * w * 2) + 6 * bq * bk * 4
    return pl.pallas_call(
        functools.partial(_diff_kernel, seq=seq, bq=bq, bk=bk, lambda_init=lambda_init),
        grid=(nb, n_pairs, seq // bq),
        in_specs=[pl.BlockSpec((None, bq, w), lambda b, t, i: (b + b_off, i, t)),
                  pl.BlockSpec((None, seq, w), lambda b, t, i: (b + b_off, 0, d // w + t)),
                  pl.BlockSpec((None, seq, w), lambda b, t, i: (b + b_off, 0, 2 * d // w + t)),
                  vec, vec, vec, vec,
                  pl.BlockSpec((1, w), lambda b, t, i: (0, 0))],
        out_specs=pl.BlockSpec((None, bq, w), lambda b, t, i: (b, i, t)),
        out_shape=jax.ShapeDtypeStruct((nb, seq, d), BF16),
        compiler_params=_params(("parallel", "parallel", "arbitrary"), blk),
        name="diff_attention",
    )(qk3, qk3, qkv3, *[p.reshape(1, HEAD_DIM).astype(F32) for p in lam_params],
      subln.reshape(1, w).astype(F32))


def _gqa_kernel(q_ref, k_ref, v_ref, o_ref, *, seq, bq, bk, group):
    q = jnp.concatenate([q_ref[:, g * HEAD_DIM:(g + 1) * HEAD_DIM] for g in range(group)], axis=0)

    def score(start):
        return _dot_nt(q, k_ref[pl.ds(start, bk), :])

    o = _attend(score, v_ref, group * bq, HEAD_DIM, seq // bk, bk)
    for g in range(group):
        o_ref[:, g * HEAD_DIM:(g + 1) * HEAD_DIM] = o[g * bq:(g + 1) * bq].astype(o_ref.dtype)


def _gqa_attention(qk3, qkv3, b_off, nb, d, *, bq=128, bk=512):
    _, seq, qk_cols = qk3.shape
    kv_dim = qk_cols - d
    n_kv = kv_dim // HEAD_DIM
    group = d // kv_dim
    w = group * HEAD_DIM
    bq, bk = min(bq, seq), min(bk, seq)
    blk = 2 * (2 * seq * HEAD_DIM * 2 + 2 * bq * w * 2) + 6 * group * bq * bk * 4
    return pl.pallas_call(
        functools.partial(_gqa_kernel, seq=seq, bq=bq, bk=bk, group=group),
        grid=(nb, n_kv, seq // bq),
        in_specs=[pl.BlockSpec((None, bq, w), lambda b, n, i: (b + b_off, i, n)),
                  pl.BlockSpec((None, seq, HEAD_DIM), lambda b, n, i: (b + b_off, 0, d // HEAD_DIM + n)),
                  pl.BlockSpec((None, seq, HEAD_DIM),
                               lambda b, n, i: (b + b_off, 0, (d + kv_dim) // HEAD_DIM + n))],
        out_specs=pl.BlockSpec((None, bq, w), lambda b, n, i: (b, i, n)),
        out_shape=jax.ShapeDtypeStruct((nb, seq, d), BF16),
        compiler_params=_params(("parallel", "parallel", "arbitrary"), blk),
        name="gqa_attention",
    )(qk3, qk3, qkv3)


def _nat_bias_table(rel_bias, rows):
    n_heads = rel_bias.shape[0]
    kh = min(NA_WIN_H, rows)
    col = np.arange(GRID_W)
    col_start = np.clip(col - NA_WIN_W // 2, 0, GRID_W - NA_WIN_W)
    col_ok = (col[None, :] >= col_start[:, None]) & (col[None, :] < col_start[:, None] + NA_WIN_W)
    dc = np.clip(col[None, :] - col[:, None], -(NA_WIN_W - 1), NA_WIN_W - 1) + NA_WIN_W - 1
    n_dc = 2 * NA_WIN_W - 1
    flat = np.zeros((3, NAT_G * GRID_W, NAT_KROWS * GRID_W), np.int32)
    valid = np.zeros((3, NAT_G * GRID_W, NAT_KROWS * GRID_W), bool)
    for var, r_base in enumerate((0, NAT_G, rows - NAT_G)):
        a = int(np.clip(r_base - NA_WIN_H // 2, 0, rows - NAT_KROWS))
        for g in range(NAT_G):
            r = r_base + g
            r0 = int(np.clip(r - kh // 2, 0, rows - kh))
            for j in range(NAT_KROWS):
                kr = a + j
                if not (r0 <= kr < r0 + kh):
                    continue
                dr = kr - r + NA_WIN_H - 1
                qs = slice(g * GRID_W, (g + 1) * GRID_W)
                ks = slice(j * GRID_W, (j + 1) * GRID_W)
                flat[var, qs, ks] = dr * n_dc + dc
                valid[var, qs, ks] = col_ok
    tbl = jnp.take(rel_bias.reshape(n_heads, -1).astype(F32), jnp.asarray(flat), axis=1)
    return jnp.where(jnp.asarray(valid)[None], tbl, NEG_BIG)


def _nat_kernel(q_ref, k_ref, v_ref, bias_ref, o_ref, *, rows, scale):
    n_groups = rows // NAT_G
    gq = NAT_G * GRID_W
    gk = NAT_KROWS * GRID_W

    def body(gi, carry):
        a = jnp.clip(gi * NAT_G - NA_WIN_H // 2, 0, rows - NAT_KROWS)
        var = jnp.where(gi == 0, 0, jnp.where(gi == n_groups - 1, 2, 1))
        qs = pl.multiple_of(gi * gq, gq)
        ks = pl.multiple_of(a * GRID_W, GRID_W)
        q = q_ref[pl.ds(qs, gq), :]
        s = _dot_nt(q, k_ref[pl.ds(ks, gk), :]) * scale + bias_ref[var]
        e = jnp.exp(s - jnp.max(s, axis=1, keepdims=True))
        l = jnp.sum(e, axis=1, keepdims=True)
        o = jnp.dot(e.astype(BF16), v_ref[pl.ds(ks, gk), :], preferred_element_type=F32) / l
        o_ref[pl.ds(qs, gq), :] = o.astype(o_ref.dtype)
        return carry

    lax.fori_loop(0, n_groups, body, 0)


def _nat_attention(qkv3, b_off, nb, bias_tbl):
    _, seq, three_d = qkv3.shape
    d = three_d // 3
    n_heads = d // HEAD_DIM
    rows = seq // GRID_W
    assert rows % NAT_G == 0 and rows >= NAT_KROWS and rows >= 3 * NAT_G
    gq, gk = NAT_G * GRID_W, NAT_KROWS * GRID_W
    blk = 2 * (4 * seq * HEAD_DIM * 2 + 3 * gq * gk * 4) + 6 * gq * gk * 4

    def col_spec(col0):
        return pl.BlockSpec((None, seq, HEAD_DIM), lambda b, h: (b + b_off, 0, col0 + h))

    return pl.pallas_call(
        functools.partial(_nat_kernel, rows=rows, scale=HEAD_DIM ** -0.5),
        grid=(nb, n_heads),
        in_specs=[col_spec(0), col_spec(n_heads), col_spec(2 * n_heads),
                  pl.BlockSpec((None, 3, gq, gk), lambda b, h: (h, 0, 0, 0))],
        out_specs=pl.BlockSpec((None, seq, HEAD_DIM), lambda b, h: (b, 0, h)),
        out_shape=jax.ShapeDtypeStruct((nb, seq, d), BF16),
        compiler_params=_params(("parallel", "parallel"), blk),
        name="nat_attention",
    )(qkv3, qkv3, qkv3, bias_tbl)


def _lambda_init(layer_idx):
    return 0.8 - 0.6 * math.exp(-0.3 * layer_idx)


def _seq_view(arr, seq):
    t, c = arr.shape
    return arr.reshape(t // seq, seq, c)


def _trunk(x_prompt, x_sample, layers, final_norm):
    bp, sp, d = x_prompt.shape
    bs, ss, _ = x_sample.shape
    tp, ts = bp * sp, bs * ss
    t = tp + ts
    assert tp % ss == 0 and t % sp == 0 and t % ss == 0
    groups = ((sp, 0, bp), (ss, tp // ss, bs))
    x = jnp.concatenate([x_prompt.reshape(tp, d), x_sample.reshape(ts, d)], axis=0)
    pos = jnp.concatenate([jnp.tile(jnp.arange(sp), bp), jnp.tile(jnp.arange(ss), bs)])

    def per_group(fn):
        outs = [fn(seq, b_off, nb).reshape(nb * seq, d) for seq, b_off, nb in groups]
        return jnp.concatenate(outs, axis=0)

    for li, p in enumerate(layers):
        kind = p["kind"]
        h = _rmsnorm(x, p["norm_mix"], BF16)
        qkv = _matmul(h, p["w_qkv"].astype(BF16))
        if kind == "nat":
            tbls = {seq: _nat_bias_table(p["rel_bias"], seq // GRID_W) for seq, _, _ in groups}
            o = per_group(lambda seq, b_off, nb: _nat_attention(_seq_view(qkv, seq), b_off, nb, tbls[seq]))
        elif kind == "diff":
            qk = _prep_qk(qkv, _partial_rope_tables(pos), n_q_cols=d, n_k_cols=d)
            lam_params = (p["lq1"], p["lk1"], p["lq2"], p["lk2"])
            o = per_group(lambda seq, b_off, nb: _diff_attention(
                _seq_view(qk, seq), _seq_view(qkv, seq), b_off, nb, lam_params, p["subln"], _lambda_init(li)))
        else:
            kv_dim = (qkv.shape[1] - d) // 2
            norm_w = jnp.stack([p["q_norm"], p["k_norm"]]).astype(F32)
            qk = _prep_qk(qkv, _axial_rope_tables(pos), n_q_cols=d, n_k_cols=kv_dim, norm_w=norm_w)
            o = per_group(lambda seq, b_off, nb: _gqa_attention(
                _seq_view(qk, seq), _seq_view(qkv, seq), b_off, nb, d))
        x = _matmul(o, p["w_o"].astype(BF16), epilogue="residual", residual=x, out_dtype=F32)
        h = _rmsnorm(x, p["norm_mlp"], BF16)
        u = _matmul(h, p["w_up"].astype(BF16), epilogue="relu2")
        x = _matmul(u, p["w_down"].astype(BF16), epilogue="residual", residual=x, out_dtype=F32)

    y = _rmsnorm(x, final_norm, F32)
    return y[:tp].reshape(bp, sp, d), y[tp:].reshape(bs, ss, d)


def kernel(x_prompt, x_sample, l0_norm_mix, l0_w_qkv, l0_rel_bias, l0_w_o, l0_norm_mlp, l0_w_up, l0_w_down, l1_norm_mix, l1_w_qkv, l1_lambda_q1, l1_lambda_k1, l1_lambda_q2, l1_lambda_k2, l1_subln, l1_w_o, l1_norm_mlp, l1_w_up, l1_w_down, l2_norm_mix, l2_w_qkv, l2_q_norm, l2_k_norm, l2_w_o, l2_norm_mlp, l2_w_up, l2_w_down, l3_norm_mix, l3_w_qkv, l3_rel_bias, l3_w_o, l3_norm_mlp, l3_w_up, l3_w_down, final_norm):
    layers = [
        {"kind": "nat", "norm_mix": l0_norm_mix, "w_qkv": l0_w_qkv, "rel_bias": l0_rel_bias, "w_o": l0_w_o,
         "norm_mlp": l0_norm_mlp, "w_up": l0_w_up, "w_down": l0_w_down},
        {"kind": "diff", "norm_mix": l1_norm_mix, "w_qkv": l1_w_qkv, "lq1": l1_lambda_q1, "lk1": l1_lambda_k1,
         "lq2": l1_lambda_q2, "lk2": l1_lambda_k2, "subln": l1_subln, "w_o": l1_w_o,
         "norm_mlp": l1_norm_mlp, "w_up": l1_w_up, "w_down": l1_w_down},
        {"kind": "gqa", "norm_mix": l2_norm_mix, "w_qkv": l2_w_qkv, "q_norm": l2_q_norm, "k_norm": l2_k_norm,
         "w_o": l2_w_o, "norm_mlp": l2_norm_mlp, "w_up": l2_w_up, "w_down": l2_w_down},
        {"kind": "nat", "norm_mix": l3_norm_mix, "w_qkv": l3_w_qkv, "rel_bias": l3_rel_bias, "w_o": l3_w_o,
         "norm_mlp": l3_norm_mlp, "w_up": l3_w_up, "w_down": l3_w_down},
    ]
    return _trunk(x_prompt, x_sample, layers, final_norm)
```

```python
import functools
import math

import numpy as np
import jax
import jax.numpy as jnp
from jax import lax
from jax.experimental import pallas as pl
from jax.experimental.pallas import tpu as pltpu

F32 = jnp.float32
BF16 = jnp.bfloat16

HEAD_DIM = 128
GRID_W = 64
NA_WIN_H = 8
NA_WIN_W = 16
GQA_GROUP = 4
ROPE_THETA = 500000.0
PARTIAL_ROT = HEAD_DIM // 4
AXIAL_THETA = 10000.0
AXIAL_DIM = HEAD_DIM // 2
NORM_EPS = 1e-6
SUBLN_EPS = 1e-5
NEG_BIG = -1e30
LOG2_E = math.log2(math.e)
QK_SCALE_LOG2 = HEAD_DIM ** -0.5 * LOG2_E

NAT_G = 4
NAT_KROWS = NAT_G + NA_WIN_H

V7X_VMEM_BYTES = 64 * 1024 * 1024
VMEM_CAP_BYTES = V7X_VMEM_BYTES - 6 * 1024 * 1024


def _params(semantics, block_bytes):
    limit = min(VMEM_CAP_BYTES, int(block_bytes * 1.2) + (8 << 20))
    return pltpu.CompilerParams(dimension_semantics=semantics, vmem_limit_bytes=limit)


def _divisor_block(n, target, align):
    best = None
    for b in range(align, min(n, target) + 1, align):
        if n % b == 0:
            best = b
    assert best is not None, (n, target, align)
    return best


def _dot_nt(a, b):
    return lax.dot_general(a, b, (((1,), (1,)), ((), ())), preferred_element_type=F32)


def _rmsnorm_kernel(x_ref, w_ref, o_ref, *, eps):
    x = x_ref[...]
    y = x * lax.rsqrt(jnp.mean(x * x, axis=-1, keepdims=True) + eps)
    o_ref[...] = (y * w_ref[...]).astype(o_ref.dtype)


def _rmsnorm(x, w, out_dtype, *, row0=0, n_rows=None, bt=256):
    t, d = x.shape
    n_rows = t if n_rows is None else n_rows
    bt = math.gcd(math.gcd(bt, n_rows), row0) if row0 else min(bt, n_rows)
    blk0 = row0 // bt
    blk = bt * d * (4 + jnp.dtype(out_dtype).itemsize) * 2
    return pl.pallas_call(
        functools.partial(_rmsnorm_kernel, eps=NORM_EPS),
        grid=(n_rows // bt,),
        in_specs=[pl.BlockSpec((bt, d), lambda i: (i + blk0, 0)),
                  pl.BlockSpec((1, d), lambda i: (0, 0))],
        out_specs=pl.BlockSpec((bt, d), lambda i: (i, 0)),
        out_shape=jax.ShapeDtypeStruct((n_rows, d), out_dtype),
        compiler_params=_params(("parallel",), blk),
        name="rmsnorm",
    )(x, w.reshape(1, d).astype(F32))


def _mm_kernel(*refs, nk, epilogue, n_scaled_blocks, col_scale):
    if epilogue == "residual":
        a_ref, b_ref, r_ref, o_ref = refs
    else:
        a_ref, b_ref, o_ref = refs
    part = jnp.dot(a_ref[...], b_ref[...], preferred_element_type=F32)
    if epilogue == "residual":
        if nk == 1:
            o_ref[...] = r_ref[...] + part
            return
        k = pl.program_id(2)

        @pl.when(k == 0)
        def _():
            o_ref[...] = r_ref[...] + part

        @pl.when(k > 0)
        def _():
            o_ref[...] += part

        return
    assert nk == 1
    if epilogue == "relu2":
        o_ref[...] = jnp.square(jnp.maximum(part, 0.0)).astype(o_ref.dtype)
    else:
        if n_scaled_blocks:
            part = part * jnp.where(pl.program_id(1) < n_scaled_blocks, col_scale, 1.0).astype(F32)
        o_ref[...] = part.astype(o_ref.dtype)


def _matmul(a, b, *, epilogue="cast", residual=None, out_dtype=BF16, bm=1024, bn=1024, bk_max=4096,
            scaled_cols=0, col_scale=1.0):
    m, k = a.shape
    _, n = b.shape
    bm, bn = _divisor_block(m, bm, 8), _divisor_block(n, bn, HEAD_DIM)
    if scaled_cols:
        bn = math.gcd(bn, scaled_cols)
    bk = _divisor_block(k, bk_max, HEAD_DIM)
    nk = k // bk
    in_specs = [pl.BlockSpec((bm, bk), lambda i, j, kk: (i, kk)),
                pl.BlockSpec((bk, bn), lambda i, j, kk: (kk, j))]
    args = [a, b]
    osz = jnp.dtype(out_dtype).itemsize
    blk = 2 * (bm * bk * 2 + bk * bn * 2 + bm * bn * osz)
    if epilogue == "residual":
        in_specs.append(pl.BlockSpec((bm, bn), lambda i, j, kk: (i, j)))
        args.append(residual)
        blk += 2 * bm * bn * 4
    return pl.pallas_call(
        functools.partial(_mm_kernel, nk=nk, epilogue=epilogue, n_scaled_blocks=scaled_cols // bn,
                          col_scale=col_scale),
        grid=(m // bm, n // bn, nk),
        in_specs=in_specs,
        out_specs=pl.BlockSpec((bm, bn), lambda i, j, kk: (i, j)),
        out_shape=jax.ShapeDtypeStruct((m, n), out_dtype),
        compiler_params=_params(("parallel", "parallel", "arbitrary"), blk),
        name="matmul_" + epilogue,
    )(*args)


def _rope_angles(pos, dim, theta):
    inv = theta ** (-jnp.arange(0, dim, 2, dtype=F32) / dim)
    ang = pos.astype(F32)[:, None] * inv[None, :]
    return jnp.cos(ang), jnp.sin(ang)


def _partial_rope_tables(pos):
    cos, sin = _rope_angles(pos, PARTIAL_ROT, ROPE_THETA)
    t, h = cos.shape
    rest = HEAD_DIM - 2 * h
    c = jnp.concatenate([cos, cos, jnp.ones((t, rest), F32)], axis=1)
    sa = jnp.concatenate([-sin, jnp.zeros((t, HEAD_DIM - h), F32)], axis=1)
    sb = jnp.concatenate([jnp.zeros((t, h), F32), sin, jnp.zeros((t, rest), F32)], axis=1)
    return c, sa, sb, h


def _axial_rope_tables(pos):
    rcos, rsin = _rope_angles(pos // GRID_W, AXIAL_DIM, AXIAL_THETA)
    ccos, csin = _rope_angles(pos % GRID_W, AXIAL_DIM, AXIAL_THETA)
    z = jnp.zeros_like(rsin)
    c = jnp.concatenate([rcos, rcos, ccos, ccos], axis=1)
    sa = jnp.concatenate([-rsin, z, -csin, z], axis=1)
    sb = jnp.concatenate([z, rsin, z, csin], axis=1)
    return c, sa, sb, rcos.shape[1]


def _prep_kernel(*refs, heads_per_block, shift, n_q_blocks, scale, use_norm):
    if use_norm:
        x_ref, c_ref, sa_ref, sb_ref, w_ref, o_ref = refs
    else:
        x_ref, c_ref, sa_ref, sb_ref, o_ref = refs
    is_q = pl.program_id(1) < n_q_blocks
    sc = jnp.where(is_q, scale, 1.0).astype(F32)
    c = c_ref[...]
    sa = sa_ref[...]
    sb = sb_ref[...]
    if use_norm:
        w = jnp.where(is_q, w_ref[0:1, :], w_ref[1:2, :])
    for hh in range(heads_per_block):
        lanes = slice(hh * HEAD_DIM, (hh + 1) * HEAD_DIM)
        x = x_ref[:, lanes].astype(F32)
        if use_norm:
            x = x * lax.rsqrt(jnp.mean(x * x, axis=-1, keepdims=True) + NORM_EPS) * w
        y = x * c + pltpu.roll(x, HEAD_DIM - shift, 1) * sa + pltpu.roll(x, shift, 1) * sb
        o_ref[:, lanes] = (y * sc).astype(o_ref.dtype)


def _prep_qk(qkv, tables, *, n_q_cols, n_k_cols, norm_w=None, bt=512, bc=1024):
    t = qkv.shape[0]
    c, sa, sb, shift = tables
    bt = min(bt, t)
    bc = math.gcd(math.gcd(bc, n_q_cols), n_k_cols)
    n_cols = n_q_cols + n_k_cols
    tab_spec = pl.BlockSpec((bt, HEAD_DIM), lambda i, j: (i, 0))
    in_specs = [pl.BlockSpec((bt, bc), lambda i, j: (i, j)), tab_spec, tab_spec, tab_spec]
    args = [qkv, c, sa, sb]
    if norm_w is not None:
        in_specs.append(pl.BlockSpec((2, HEAD_DIM), lambda i, j: (0, 0)))
        args.append(norm_w)
    blk = 2 * (2 * bt * bc * 2 + 3 * bt * HEAD_DIM * 4)
    return pl.pallas_call(
        functools.partial(_prep_kernel, heads_per_block=bc // HEAD_DIM, shift=shift,
                          n_q_blocks=n_q_cols // bc, scale=QK_SCALE_LOG2, use_norm=norm_w is not None),
        grid=(t // bt, n_cols // bc),
        in_specs=in_specs,
        out_specs=pl.BlockSpec((bt, bc), lambda i, j: (i, j)),
        out_shape=jax.ShapeDtypeStruct((t, n_cols), BF16),
        compiler_params=_params(("parallel", "arbitrary"), blk),
        name="prep_qk",
    )(*args)


def _rows(start, size):
    if isinstance(start, int):
        return pl.ds(start, size)
    return pl.ds(pl.multiple_of(start, size), size)


def _attention_call(kernel_fn, *, grid, in_specs, args, out_block, out_index, b_off, shared_out, out_view,
                    scratch=(), semantics, block_bytes, name):
    n_in = len(args)
    aliases = {}
    if shared_out is not None:
        inner = kernel_fn

        def kernel_fn(*refs):
            return inner(*refs[:n_in], *refs[n_in + 1:])

        in_specs = list(in_specs) + [pl.BlockSpec(memory_space=pl.ANY)]
        args = list(args) + [shared_out.reshape(out_view)]
        aliases = {n_in: 0}

    def out_map(b, *rest):
        return (b + b_off,) + tuple(out_index(b, *rest))

    return pl.pallas_call(
        kernel_fn,
        grid=grid,
        in_specs=in_specs,
        out_specs=pl.BlockSpec(out_block, out_map),
        out_shape=jax.ShapeDtypeStruct(out_view, BF16),
        scratch_shapes=list(scratch),
        input_output_aliases=aliases,
        compiler_params=_params(semantics, block_bytes),
        name=name,
    )(*args)


def _transpose_bf16(x):
    return x.astype(F32).T.astype(BF16)


def _build_vt(v_ref, vt_ref, n_chunks, bk):
    @pl.when(pl.program_id(2) == 0)
    def _():
        def body(c, carry):
            vt_ref[c] = _transpose_bf16(v_ref[_rows(c * bk, bk), :])
            return carry

        lax.fori_loop(0, n_chunks, body, 0)


def _attend_t(score_t, vt_ref, s_ref, m_ref, l_ref, acc_ref, n_chunks):
    assert n_chunks % 2 == 0

    def scores(c, slot):
        s = score_t(c)
        s_ref[slot] = s
        return jnp.max(s, axis=0, keepdims=True)

    def softmax_pv(c, slot, chunk_max):
        m_old = m_ref[...]
        m_new = jnp.maximum(m_old, chunk_max)
        alpha = jnp.exp2(m_old - m_new)
        p = jnp.exp2(s_ref[slot] - m_new)
        l_ref[...] = alpha * l_ref[...] + jnp.sum(p, axis=0, keepdims=True)
        acc_ref[...] = alpha * acc_ref[...] + jnp.dot(vt_ref[c], p.astype(BF16), preferred_element_type=F32)
        m_ref[...] = m_new

    def pair(c0, max0, last):
        max1 = scores(c0 + 1, 1)
        softmax_pv(c0, 0, max0)
        next_max = None if last else scores(c0 + 2, 0)
        softmax_pv(c0 + 1, 1, max1)
        return next_max

    m_ref[...] = jnp.full(m_ref.shape, NEG_BIG, F32)
    l_ref[...] = jnp.zeros(l_ref.shape, F32)
    acc_ref[...] = jnp.zeros(acc_ref.shape, F32)
    max0 = scores(0, 0)
    max0 = lax.fori_loop(0, n_chunks // 2 - 1, lambda ci, mx: pair(2 * ci, mx, False), max0)
    pair(n_chunks - 2, max0, True)
    return acc_ref[...] / l_ref[...]


def _diff_kernel(q_ref, k_ref, v_ref, lq1_ref, lk1_ref, lq2_ref, lk2_ref, sub_ref, o_ref,
                 vt_ref, qt_ref, s_ref, m_ref, l_ref, acc_ref, *, n_chunks, bq, bk, lambda_init):
    _build_vt(v_ref, vt_ref, n_chunks, bk)
    for h in range(2):
        qt_ref[h] = _transpose_bf16(q_ref[:, h * HEAD_DIM:(h + 1) * HEAD_DIM])

    def score_t(c):
        rows = _rows(c * bk, bk)
        s1 = jnp.dot(k_ref[rows, 0:HEAD_DIM], qt_ref[0], preferred_element_type=F32)
        s2 = jnp.dot(k_ref[rows, HEAD_DIM:2 * HEAD_DIM], qt_ref[1], preferred_element_type=F32)
        return jnp.concatenate([s1, s2], axis=1)

    o = _attend_t(score_t, vt_ref, s_ref, m_ref, l_ref, acc_ref, n_chunks).T
    lam = (jnp.exp(jnp.sum(lq1_ref[...] * lk1_ref[...], axis=-1, keepdims=True))
           - jnp.exp(jnp.sum(lq2_ref[...] * lk2_ref[...], axis=-1, keepdims=True)) + lambda_init)
    d = o[:bq] - lam * o[bq:]
    y = d * lax.rsqrt(jnp.mean(d * d, axis=-1, keepdims=True) + SUBLN_EPS) * sub_ref[...]
    o_ref[...] = (y * (1.0 - lambda_init)).astype(o_ref.dtype)


def _flash_scratch(n_chunks, dv, bk, m_rows, qt_shape):
    return [pltpu.VMEM((n_chunks, dv, bk), BF16), pltpu.VMEM(qt_shape, BF16),
            pltpu.VMEM((2, bk, m_rows), F32), pltpu.VMEM((1, m_rows), F32), pltpu.VMEM((1, m_rows), F32),
            pltpu.VMEM((dv, m_rows), F32)]


def _diff_attention(qk3, qkv3, b_off, nb, shared_out, lam_params, subln, lambda_init, *, bq=256, bk=512):
    n_seq, seq, two_d = qk3.shape
    d = two_d // 2
    n_pairs = d // (2 * HEAD_DIM)
    w = 2 * HEAD_DIM
    bq, bk = min(bq, seq), min(bk, seq)
    n_chunks = seq // bk
    vec = pl.BlockSpec((1, HEAD_DIM), lambda b, t, i: (0, 0))
    blk = 2 * (2 * seq * w * 2 + 2 * bq * w * 2) + seq * w * 2 + 8 * bk * 2 * bq * 4
    return _attention_call(
        functools.partial(_diff_kernel, n_chunks=n_chunks, bq=bq, bk=bk, lambda_init=lambda_init),
        grid=(nb, n_pairs, seq // bq),
        in_specs=[pl.BlockSpec((None, bq, w), lambda b, t, i: (b + b_off, i, t)),
                  pl.BlockSpec((None, seq, w), lambda b, t, i: (b + b_off, 0, d // w + t)),
                  pl.BlockSpec((None, seq, w), lambda b, t, i: (b + b_off, 0, 2 * d // w + t)),
                  vec, vec, vec, vec,
                  pl.BlockSpec((1, w), lambda b, t, i: (0, 0))],
        args=[qk3, qk3, qkv3, *[p.reshape(1, HEAD_DIM).astype(F32) for p in lam_params],
              subln.reshape(1, w).astype(F32)],
        out_block=(None, bq, w), out_index=lambda b, t, i: (i, t),
        b_off=b_off, shared_out=shared_out, out_view=(n_seq, seq, d),
        scratch=_flash_scratch(n_chunks, w, bk, 2 * bq, (2, HEAD_DIM, bq)),
        semantics=("parallel", "parallel", "arbitrary"), block_bytes=blk, name="diff_attention")


def _gqa_kernel(q_ref, k_ref, v_ref, o_ref, vt_ref, qt_ref, s_ref, m_ref, l_ref, acc_ref, *,
                n_chunks, bq, bk, group):
    _build_vt(v_ref, vt_ref, n_chunks, bk)
    for g in range(group):
        qt_ref[:, g * bq:(g + 1) * bq] = _transpose_bf16(q_ref[:, g * HEAD_DIM:(g + 1) * HEAD_DIM])

    def score_t(c):
        return jnp.dot(k_ref[_rows(c * bk, bk), :], qt_ref[...], preferred_element_type=F32)

    ot = _attend_t(score_t, vt_ref, s_ref, m_ref, l_ref, acc_ref, n_chunks)
    for g in range(group):
        o_ref[:, g * HEAD_DIM:(g + 1) * HEAD_DIM] = ot[:, g * bq:(g + 1) * bq].T.astype(o_ref.dtype)


def _gqa_attention(qk3, qkv3, b_off, nb, shared_out, d, *, bq=128, bk=512):
    n_seq, seq, qk_cols = qk3.shape
    kv_dim = qk_cols - d
    n_kv = kv_dim // HEAD_DIM
    group = d // kv_dim
    w = group * HEAD_DIM
    bq, bk = min(bq, seq), min(bk, seq)
    n_chunks = seq // bk
    blk = 2 * (2 * seq * HEAD_DIM * 2 + 2 * bq * w * 2) + seq * HEAD_DIM * 2 + 8 * bk * group * bq * 4
    return _attention_call(
        functools.partial(_gqa_kernel, n_chunks=n_chunks, bq=bq, bk=bk, group=group),
        grid=(nb, n_kv, seq // bq),
        in_specs=[pl.BlockSpec((None, bq, w), lambda b, n, i: (b + b_off, i, n)),
                  pl.BlockSpec((None, seq, HEAD_DIM), lambda b, n, i: (b + b_off, 0, d // HEAD_DIM + n)),
                  pl.BlockSpec((None, seq, HEAD_DIM),
                               lambda b, n, i: (b + b_off, 0, (d + kv_dim) // HEAD_DIM + n))],
        args=[qk3, qk3, qkv3],
        out_block=(None, bq, w), out_index=lambda b, n, i: (i, n),
        b_off=b_off, shared_out=shared_out, out_view=(n_seq, seq, d),
        scratch=_flash_scratch(n_chunks, HEAD_DIM, bk, group * bq, (HEAD_DIM, group * bq)),
        semantics=("parallel", "parallel", "arbitrary"), block_bytes=blk, name="gqa_attention")


def _nat_bias_table(rel_bias, rows):
    n_heads = rel_bias.shape[0]
    kh = min(NA_WIN_H, rows)
    n_dr = 2 * NA_WIN_H - 1
    col = np.arange(GRID_W)
    col_start = np.clip(col - NA_WIN_W // 2, 0, GRID_W - NA_WIN_W)
    col_ok = (col[None, :] >= col_start[:, None]) & (col[None, :] < col_start[:, None] + NA_WIN_W)
    dc = np.clip(col[None, :] - col[:, None], -(NA_WIN_W - 1), NA_WIN_W - 1) + NA_WIN_W - 1
    planes = jnp.take(rel_bias.astype(F32) * LOG2_E, jnp.asarray(dc), axis=2)
    planes = jnp.where(jnp.asarray(col_ok)[None, None], planes, NEG_BIG)
    planes = jnp.concatenate([planes, jnp.full((n_heads, 1, GRID_W, GRID_W), NEG_BIG, F32)], axis=1)
    plane_idx = np.full((3, NAT_G, NAT_KROWS), n_dr, np.int32)
    for var, r_base in enumerate((0, NAT_G, rows - NAT_G)):
        a = int(np.clip(r_base - NA_WIN_H // 2, 0, rows - NAT_KROWS))
        for g in range(NAT_G):
            r = r_base + g
            r0 = int(np.clip(r - kh // 2, 0, rows - kh))
            for j in range(NAT_KROWS):
                if r0 <= a + j < r0 + kh:
                    plane_idx[var, g, j] = a + j - r + NA_WIN_H - 1
    tbl = jnp.take(planes, jnp.asarray(plane_idx), axis=1)
    tbl = jnp.transpose(tbl, (0, 1, 2, 4, 3, 5))
    return tbl.reshape(n_heads, 3, NAT_G * GRID_W, NAT_KROWS * GRID_W)


def _nat_kernel(q_ref, k_ref, v_ref, bias_ref, o_ref, *, rows):
    n_groups = rows // NAT_G
    gq = NAT_G * GRID_W
    gk = NAT_KROWS * GRID_W

    def body(gi, carry):
        a = jnp.clip(gi * NAT_G - NA_WIN_H // 2, 0, rows - NAT_KROWS)
        var = jnp.where(gi == 0, 0, jnp.where(gi == n_groups - 1, 2, 1))
        qs = pl.multiple_of(gi * gq, gq)
        ks = pl.multiple_of(a * GRID_W, GRID_W)
        q = q_ref[pl.ds(qs, gq), :]
        s = _dot_nt(q, k_ref[pl.ds(ks, gk), :]) + bias_ref[var]
        e = jnp.exp2(s - jnp.max(s, axis=1, keepdims=True))
        l = jnp.sum(e, axis=1, keepdims=True)
        o = jnp.dot(e.astype(BF16), v_ref[pl.ds(ks, gk), :], preferred_element_type=F32) / l
        o_ref[pl.ds(qs, gq), :] = o.astype(o_ref.dtype)
        return carry

    lax.fori_loop(0, n_groups, body, 0, unroll=2)


def _nat_attention(qkv3, b_off, nb, shared_out, bias_tbl):
    n_seq, seq, three_d = qkv3.shape
    d = three_d // 3
    n_heads = d // HEAD_DIM
    rows = seq // GRID_W
    assert rows % NAT_G == 0 and rows >= NAT_KROWS and rows >= 3 * NAT_G
    gq, gk = NAT_G * GRID_W, NAT_KROWS * GRID_W
    blk = 2 * (4 * seq * HEAD_DIM * 2 + 3 * gq * gk * 4) + 6 * gq * gk * 4

    def col_spec(col0):
        return pl.BlockSpec((None, seq, HEAD_DIM), lambda b, h: (b + b_off, 0, col0 + h))

    return _attention_call(
        functools.partial(_nat_kernel, rows=rows),
        grid=(nb, n_heads),
        in_specs=[col_spec(0), col_spec(n_heads), col_spec(2 * n_heads),
                  pl.BlockSpec((None, 3, gq, gk), lambda b, h: (h, 0, 0, 0))],
        args=[qkv3, qkv3, qkv3, bias_tbl],
        out_block=(None, seq, HEAD_DIM), out_index=lambda b, h: (0, h),
        b_off=b_off, shared_out=shared_out, out_view=(n_seq, seq, d),
        semantics=("parallel", "parallel"), block_bytes=blk, name="nat_attention")


def _lambda_init(layer_idx):
    return 0.8 - 0.6 * math.exp(-0.3 * layer_idx)


def _seq_view(arr, seq):
    t, c = arr.shape
    return arr.reshape(t // seq, seq, c)


def _trunk(x_prompt, x_sample, layers, final_norm):
    bp, sp, d = x_prompt.shape
    bs, ss, _ = x_sample.shape
    tp, ts = bp * sp, bs * ss
    t = tp + ts
    assert tp % ss == 0 and t % sp == 0 and t % ss == 0
    groups = ((sp, 0, bp), (ss, tp // ss, bs))
    x = jnp.concatenate([x_prompt.reshape(tp, d), x_sample.reshape(ts, d)], axis=0)
    pos = jnp.concatenate([jnp.tile(jnp.arange(sp), bp), jnp.tile(jnp.arange(ss), bs)])

    def per_group(fn):
        out = None
        for seq, b_off, nb in groups:
            out = fn(seq, b_off, nb, out).reshape(t, d)
        return out

    for li, p in enumerate(layers):
        kind = p["kind"]
        h = _rmsnorm(x, p["norm_mix"], BF16)
        nat_scale = dict(scaled_cols=d, col_scale=QK_SCALE_LOG2) if kind == "nat" else {}
        qkv = _matmul(h, p["w_qkv"].astype(BF16), **nat_scale)
        if kind == "nat":
            tbls = {seq: _nat_bias_table(p["rel_bias"], seq // GRID_W) for seq, _, _ in groups}
            o = per_group(lambda seq, b_off, nb, out: _nat_attention(
                _seq_view(qkv, seq), b_off, nb, out, tbls[seq]))
        elif kind == "diff":
            qk = _prep_qk(qkv, _partial_rope_tables(pos), n_q_cols=d, n_k_cols=d)
            lam_params = (p["lq1"], p["lk1"], p["lq2"], p["lk2"])
            o = per_group(lambda seq, b_off, nb, out: _diff_attention(
                _seq_view(qk, seq), _seq_view(qkv, seq), b_off, nb, out, lam_params, p["subln"],
                _lambda_init(li)))
        else:
            kv_dim = (qkv.shape[1] - d) // 2
            norm_w = jnp.stack([p["q_norm"], p["k_norm"]]).astype(F32)
            qk = _prep_qk(qkv, _axial_rope_tables(pos), n_q_cols=d, n_k_cols=kv_dim, norm_w=norm_w)
            o = per_group(lambda seq, b_off, nb, out: _gqa_attention(
                _seq_view(qk, seq), _seq_view(qkv, seq), b_off, nb, out, d))
        x = _matmul(o, p["w_o"].astype(BF16), epilogue="residual", residual=x, out_dtype=F32)
        h = _rmsnorm(x, p["norm_mlp"], BF16)
        u = _matmul(h, p["w_up"].astype(BF16), epilogue="relu2")
        x = _matmul(u, p["w_down"].astype(BF16), epilogue="residual", residual=x, out_dtype=F32)

    y_prompt = _rmsnorm(x, final_norm, F32, row0=0, n_rows=tp)
    y_sample = _rmsnorm(x, final_norm, F32, row0=tp, n_rows=ts)
    return y_prompt.reshape(bp, sp, d), y_sample.reshape(bs, ss, d)


def kernel(x_prompt, x_sample, l0_norm_mix, l0_w_qkv, l0_rel_bias, l0_w_o, l0_norm_mlp, l0_w_up, l0_w_down, l1_norm_mix, l1_w_qkv, l1_lambda_q1, l1_lambda_k1, l1_lambda_q2, l1_lambda_k2, l1_subln, l1_w_o, l1_norm_mlp, l1_w_up, l1_w_down, l2_norm_mix, l2_w_qkv, l2_q_norm, l2_k_norm, l2_w_o, l2_norm_mlp, l2_w_up, l2_w_down, l3_norm_mix, l3_w_qkv, l3_rel_bias, l3_w_o, l3_norm_mlp, l3_w_up, l3_w_down, final_norm):
    layers = [
        {"kind": "nat", "norm_mix": l0_norm_mix, "w_qkv": l0_w_qkv, "rel_bias": l0_rel_bias, "w_o": l0_w_o,
         "norm_mlp": l0_norm_mlp, "w_up": l0_w_up, "w_down": l0_w_down},
        {"kind": "diff", "norm_mix": l1_norm_mix, "w_qkv": l1_w_qkv, "lq1": l1_lambda_q1, "lk1": l1_lambda_k1,
         "lq2": l1_lambda_q2, "lk2": l1_lambda_k2, "subln": l1_subln, "w_o": l1_w_o,
         "norm_mlp": l1_norm_mlp, "w_up": l1_w_up, "w_down": l1_w_down},
        {"kind": "gqa", "norm_mix": l2_norm_mix, "w_qkv": l2_w_qkv, "q_norm": l2_q_norm, "k_norm": l2_k_norm,
         "w_o": l2_w_o, "norm_mlp": l2_norm_mlp, "w_up": l2_w_up, "w_down": l2_w_down},
        {"kind": "nat", "norm_mix": l3_norm_mix, "w_qkv": l3_w_qkv, "rel_bias": l3_rel_bias, "w_o": l3_w_o,
         "norm_mlp": l3_norm_mlp, "w_up": l3_w_up, "w_down": l3_w_down},
    ]
    return _trunk(x_prompt, x_sample, layers, final_norm)
```

```python
import functools
import math

import numpy as np
import jax
import jax.numpy as jnp
from jax import lax
from jax.experimental import pallas as pl
from jax.experimental.pallas import tpu as pltpu

F32 = jnp.float32
BF16 = jnp.bfloat16

HEAD_DIM = 128
GRID_W = 64
NA_WIN_H = 8
NA_WIN_W = 16
GQA_GROUP = 4
ROPE_THETA = 500000.0
PARTIAL_ROT = HEAD_DIM // 4
AXIAL_THETA = 10000.0
AXIAL_DIM = HEAD_DIM // 2
NORM_EPS = 1e-6
SUBLN_EPS = 1e-5
NEG_BIG = -1e30
LOG2_E = math.log2(math.e)
QK_SCALE_LOG2 = HEAD_DIM ** -0.5 * LOG2_E

NAT_G = 4
NAT_KROWS = NAT_G + NA_WIN_H

V7X_VMEM_BYTES = 64 * 1024 * 1024
VMEM_CAP_BYTES = V7X_VMEM_BYTES - 6 * 1024 * 1024


def _params(semantics, block_bytes):
    limit = min(VMEM_CAP_BYTES, int(block_bytes * 1.2) + (8 << 20))
    return pltpu.CompilerParams(dimension_semantics=semantics, vmem_limit_bytes=limit)


def _divisor_block(n, target, align):
    best = None
    for b in range(align, min(n, target) + 1, align):
        if n % b == 0:
            best = b
    assert best is not None, (n, target, align)
    return best


def _dot_nt(a, b):
    return lax.dot_general(a, b, (((1,), (1,)), ((), ())), preferred_element_type=F32)


def _rmsnorm_kernel(x_ref, w_ref, o_ref, *, eps):
    x = x_ref[...]
    y = x * lax.rsqrt(jnp.mean(x * x, axis=-1, keepdims=True) + eps)
    o_ref[...] = (y * w_ref[...]).astype(o_ref.dtype)


def _rmsnorm(x, w, out_dtype, *, row0=0, n_rows=None, bt=256):
    t, d = x.shape
    n_rows = t if n_rows is None else n_rows
    bt = math.gcd(math.gcd(bt, n_rows), row0) if row0 else min(bt, n_rows)
    blk0 = row0 // bt
    blk = bt * d * (4 + jnp.dtype(out_dtype).itemsize) * 2
    return pl.pallas_call(
        functools.partial(_rmsnorm_kernel, eps=NORM_EPS),
        grid=(n_rows // bt,),
        in_specs=[pl.BlockSpec((bt, d), lambda i: (i + blk0, 0)),
                  pl.BlockSpec((1, d), lambda i: (0, 0))],
        out_specs=pl.BlockSpec((bt, d), lambda i: (i, 0)),
        out_shape=jax.ShapeDtypeStruct((n_rows, d), out_dtype),
        compiler_params=_params(("parallel",), blk),
        name="rmsnorm",
    )(x, w.reshape(1, d).astype(F32))


def _fold_lanes(v):
    acc = v[:, 0:HEAD_DIM]
    for g in range(1, v.shape[1] // HEAD_DIM):
        acc = acc + v[:, g * HEAD_DIM:(g + 1) * HEAD_DIM]
    return acc


def _cast_stats_kernel(x_ref, xb_ref, ss_ref):
    x = x_ref[...]
    xb_ref[...] = x.astype(BF16)
    ss_ref[...] = _fold_lanes(x * x)


def _cast_stats(x, *, bt=256):
    t, d = x.shape
    bt = min(bt, t)
    return pl.pallas_call(
        _cast_stats_kernel,
        grid=(t // bt,),
        in_specs=[pl.BlockSpec((bt, d), lambda i: (i, 0))],
        out_specs=[pl.BlockSpec((bt, d), lambda i: (i, 0)), pl.BlockSpec((bt, HEAD_DIM), lambda i: (i, 0))],
        out_shape=[jax.ShapeDtypeStruct((t, d), BF16), jax.ShapeDtypeStruct((t, HEAD_DIM), F32)],
        compiler_params=_params(("parallel",), bt * d * 6 * 2),
        name="cast_stats",
    )(x)


def _fold_gain(gain, w):
    return (gain.astype(F32)[:, None] * w).astype(BF16)


def _proj_kernel(*refs, d_model, epilogue, n_q_blocks, n_k_blocks, q_scale, rope_shift, qk_norm,
                 heads_per_block):
    if epilogue == "qkv_rope":
        a_ref, b_ref, ss_ref, c_ref, sa_ref, sb_ref = refs[:6]
        nw_ref = refs[6] if qk_norm else None
    else:
        a_ref, b_ref, ss_ref = refs[:3]
    o_ref = refs[-1]
    rstd = lax.rsqrt(jnp.sum(ss_ref[...], axis=-1, keepdims=True) / d_model + NORM_EPS)
    y = jnp.dot(a_ref[...], b_ref[...], preferred_element_type=F32) * rstd
    if epilogue == "relu2":
        o_ref[...] = jnp.square(jnp.maximum(y, 0.0)).astype(o_ref.dtype)
        return
    j = pl.program_id(1)
    is_q = j < n_q_blocks
    sc = jnp.where(is_q, q_scale, 1.0).astype(F32)
    if epilogue == "qkv":
        o_ref[...] = (y * sc).astype(o_ref.dtype)
        return
    is_qk = j < n_q_blocks + n_k_blocks
    c = c_ref[...]
    sa = sa_ref[...]
    sb = sb_ref[...]
    if qk_norm:
        gain = jnp.where(is_q, nw_ref[0:1, :], jnp.where(is_qk, nw_ref[1:2, :], 1.0))
    for hh in range(heads_per_block):
        lanes = slice(hh * HEAD_DIM, (hh + 1) * HEAD_DIM)
        x = y[:, lanes]
        if qk_norm:
            inv = lax.rsqrt(jnp.mean(x * x, axis=-1, keepdims=True) + NORM_EPS)
            x = x * jnp.where(is_qk, inv, 1.0) * gain
        x = x * c + pltpu.roll(x, HEAD_DIM - rope_shift, 1) * sa + pltpu.roll(x, rope_shift, 1) * sb
        o_ref[:, lanes] = (x * sc).astype(o_ref.dtype)


def _project(xb, sumsq, w, *, epilogue, n_q_cols=0, n_k_cols=0, q_scale=1.0, rope=None, qk_norm_w=None,
             bm=1024, bn=1024):
    m, k = xb.shape
    _, n = w.shape
    bm, bn = _divisor_block(m, bm, 8), _divisor_block(n, bn, HEAD_DIM)
    if n_q_cols:
        bn = math.gcd(math.gcd(bn, n_q_cols), n_k_cols) if n_k_cols else math.gcd(bn, n_q_cols)
    in_specs = [pl.BlockSpec((bm, k), lambda i, j: (i, 0)),
                pl.BlockSpec((k, bn), lambda i, j: (0, j)),
                pl.BlockSpec((bm, HEAD_DIM), lambda i, j: (i, 0))]
    args = [xb, w, sumsq]
    shift = 0
    n_qk_blocks = (n_q_cols + n_k_cols) // bn
    if epilogue == "qkv_rope":
        c, sa, sb, shift = rope
        tab_spec = pl.BlockSpec((None, bm, HEAD_DIM), lambda i, j: (jnp.where(j < n_qk_blocks, 0, 1), i, 0))
        in_specs += [tab_spec, tab_spec, tab_spec]
        args += [jnp.stack([c, jnp.ones_like(c)]), jnp.stack([sa, jnp.zeros_like(sa)]),
                 jnp.stack([sb, jnp.zeros_like(sb)])]
        if qk_norm_w is not None:
            in_specs.append(pl.BlockSpec((2, HEAD_DIM), lambda i, j: (0, 0)))
            args.append(qk_norm_w)
    blk = 2 * (bm * k * 2 + k * bn * 2 + bm * bn * 2 + 4 * bm * HEAD_DIM * 4) + 2 * bm * bn * 4
    return pl.pallas_call(
        functools.partial(_proj_kernel, d_model=k, epilogue=epilogue, n_q_blocks=n_q_cols // bn,
                          n_k_blocks=n_k_cols // bn, q_scale=q_scale, rope_shift=shift,
                          qk_norm=qk_norm_w is not None, heads_per_block=bn // HEAD_DIM),
        grid=(m // bm, n // bn),
        in_specs=in_specs,
        out_specs=pl.BlockSpec((bm, bn), lambda i, j: (i, j)),
        out_shape=jax.ShapeDtypeStruct((m, n), BF16),
        compiler_params=_params(("parallel", "arbitrary"), blk),
        name="proj_" + epilogue,
    )(*args)


def _resid_kernel(*refs, nk, emit_stats):
    if emit_stats:
        a_ref, b_ref, r_ref, x_ref, xb_ref, ss_ref = refs
    else:
        a_ref, b_ref, r_ref, x_ref = refs
    part = jnp.dot(a_ref[...], b_ref[...], preferred_element_type=F32)
    j = pl.program_id(1)

    def finish(val):
        x_ref[...] = val
        if not emit_stats:
            return
        xb_ref[...] = val.astype(BF16)
        sq = _fold_lanes(val * val)

        @pl.when(j == 0)
        def _():
            ss_ref[...] = sq

        @pl.when(j > 0)
        def _():
            ss_ref[...] += sq

    if nk == 1:
        finish(r_ref[...] + part)
        return
    k = pl.program_id(2)

    @pl.when(k == 0)
    def _():
        x_ref[...] = r_ref[...] + part

    @pl.when(jnp.logical_and(k > 0, k < nk - 1))
    def _():
        x_ref[...] += part

    @pl.when(k == nk - 1)
    def _():
        finish(x_ref[...] + part)


def _residual_matmul(a, w, x, *, emit_stats=True, bm=1024, bn=1024, bk_max=4096):
    m, k = a.shape
    _, n = w.shape
    bm, bn = _divisor_block(m, bm, 8), _divisor_block(n, bn, HEAD_DIM)

    def footprint(bk):
        byt = 2 * (bm * bk * 2 + bk * bn * 2 + 2 * bm * bn * 4) + 2 * bm * bn * 4
        if emit_stats:
            byt += 2 * (bm * bn * 2 + bm * HEAD_DIM * 4)
        return byt

    bk = _divisor_block(k, bk_max, HEAD_DIM)
    while footprint(bk) > VMEM_CAP_BYTES and bk % (2 * HEAD_DIM) == 0:
        bk //= 2
    nk = k // bk
    blk = footprint(bk)
    tile = pl.BlockSpec((bm, bn), lambda i, j, kk: (i, j))
    out_specs = [tile]
    out_shape = [jax.ShapeDtypeStruct((m, n), F32)]
    if emit_stats:
        out_specs += [tile, pl.BlockSpec((bm, HEAD_DIM), lambda i, j, kk: (i, 0))]
        out_shape += [jax.ShapeDtypeStruct((m, n), BF16), jax.ShapeDtypeStruct((m, HEAD_DIM), F32)]
    out = pl.pallas_call(
        functools.partial(_resid_kernel, nk=nk, emit_stats=emit_stats),
        grid=(m // bm, n // bn, nk),
        in_specs=[pl.BlockSpec((bm, bk), lambda i, j, kk: (i, kk)),
                  pl.BlockSpec((bk, bn), lambda i, j, kk: (kk, j)),
                  tile],
        out_specs=out_specs,
        out_shape=out_shape,
        compiler_params=_params(("parallel", "arbitrary", "arbitrary"), blk),
        name="resid_matmul",
    )(a, w, x)
    return tuple(out) if emit_stats else (out[0], None, None)


def _rope_angles(pos, dim, theta):
    inv = theta ** (-jnp.arange(0, dim, 2, dtype=F32) / dim)
    ang = pos.astype(F32)[:, None] * inv[None, :]
    return jnp.cos(ang), jnp.sin(ang)


def _partial_rope_tables(pos):
    cos, sin = _rope_angles(pos, PARTIAL_ROT, ROPE_THETA)
    t, h = cos.shape
    rest = HEAD_DIM - 2 * h
    c = jnp.concatenate([cos, cos, jnp.ones((t, rest), F32)], axis=1)
    sa = jnp.concatenate([-sin, jnp.zeros((t, HEAD_DIM - h), F32)], axis=1)
    sb = jnp.concatenate([jnp.zeros((t, h), F32), sin, jnp.zeros((t, rest), F32)], axis=1)
    return c, sa, sb, h


def _axial_rope_tables(pos):
    rcos, rsin = _rope_angles(pos // GRID_W, AXIAL_DIM, AXIAL_THETA)
    ccos, csin = _rope_angles(pos % GRID_W, AXIAL_DIM, AXIAL_THETA)
    z = jnp.zeros_like(rsin)
    c = jnp.concatenate([rcos, rcos, ccos, ccos], axis=1)
    sa = jnp.concatenate([-rsin, z, -csin, z], axis=1)
    sb = jnp.concatenate([z, rsin, z, csin], axis=1)
    return c, sa, sb, rcos.shape[1]


def _rows(start, size):
    if isinstance(start, int):
        return pl.ds(start, size)
    return pl.ds(pl.multiple_of(start, size), size)


def _attention_call(kernel_fn, *, grid, in_specs, args, out_block, out_index, b_off, shared_out, out_view,
                    scratch=(), semantics, block_bytes, name):
    n_in = len(args)
    aliases = {}
    if shared_out is not None:
        inner = kernel_fn

        def kernel_fn(*refs):
            return inner(*refs[:n_in], *refs[n_in + 1:])

        in_specs = list(in_specs) + [pl.BlockSpec(memory_space=pl.ANY)]
        args = list(args) + [shared_out.reshape(out_view)]
        aliases = {n_in: 0}

    def out_map(b, *rest):
        return (b + b_off,) + tuple(out_index(b, *rest))

    return pl.pallas_call(
        kernel_fn,
        grid=grid,
        in_specs=in_specs,
        out_specs=pl.BlockSpec(out_block, out_map),
        out_shape=jax.ShapeDtypeStruct(out_view, BF16),
        scratch_shapes=list(scratch),
        input_output_aliases=aliases,
        compiler_params=_params(semantics, block_bytes),
        name=name,
    )(*args)


def _transpose_bf16(x):
    return x.astype(F32).T.astype(BF16)


def _build_vt(v_ref, vt_ref, n_chunks, bk):
    @pl.when(pl.program_id(2) == 0)
    def _():
        def body(c, carry):
            vt_ref[c] = _transpose_bf16(v_ref[_rows(c * bk, bk), :])
            return carry

        lax.fori_loop(0, n_chunks, body, 0)


def _attend_t(score_t, vt_ref, s_ref, m_ref, l_ref, acc_ref, n_chunks):
    assert n_chunks % 2 == 0

    def scores(c, slot):
        s = score_t(c)
        s_ref[slot] = s
        return jnp.max(s, axis=0, keepdims=True)

    def softmax_pv(c, slot, chunk_max):
        m_old = m_ref[...]
        m_new = jnp.maximum(m_old, chunk_max)
        alpha = jnp.exp2(m_old - m_new)
        p = jnp.exp2(s_ref[slot] - m_new)
        l_ref[...] = alpha * l_ref[...] + jnp.sum(p, axis=0, keepdims=True)
        acc_ref[...] = alpha * acc_ref[...] + jnp.dot(vt_ref[c], p.astype(BF16), preferred_element_type=F32)
        m_ref[...] = m_new

    def pair(c0, max0, last):
        max1 = scores(c0 + 1, 1)
        softmax_pv(c0, 0, max0)
        next_max = None if last else scores(c0 + 2, 0)
        softmax_pv(c0 + 1, 1, max1)
        return next_max

    m_ref[...] = jnp.full(m_ref.shape, NEG_BIG, F32)
    l_ref[...] = jnp.zeros(l_ref.shape, F32)
    acc_ref[...] = jnp.zeros(acc_ref.shape, F32)
    max0 = scores(0, 0)
    max0 = lax.fori_loop(0, n_chunks // 2 - 1, lambda ci, mx: pair(2 * ci, mx, False), max0)
    pair(n_chunks - 2, max0, True)
    return acc_ref[...] / l_ref[...]


def _diff_kernel(q_ref, k_ref, v_ref, lq1_ref, lk1_ref, lq2_ref, lk2_ref, sub_ref, o_ref,
                 vt_ref, qt_ref, s_ref, m_ref, l_ref, acc_ref, *, n_chunks, bq, bk, lambda_init):
    _build_vt(v_ref, vt_ref, n_chunks, bk)
    for h in range(2):
        qt_ref[h] = _transpose_bf16(q_ref[:, h * HEAD_DIM:(h + 1) * HEAD_DIM])

    def score_t(c):
        rows = _rows(c * bk, bk)
        s1 = jnp.dot(k_ref[rows, 0:HEAD_DIM], qt_ref[0], preferred_element_type=F32)
        s2 = jnp.dot(k_ref[rows, HEAD_DIM:2 * HEAD_DIM], qt_ref[1], preferred_element_type=F32)
        return jnp.concatenate([s1, s2], axis=1)

    o = _attend_t(score_t, vt_ref, s_ref, m_ref, l_ref, acc_ref, n_chunks).T
    lam = (jnp.exp(jnp.sum(lq1_ref[...] * lk1_ref[...], axis=-1, keepdims=True))
           - jnp.exp(jnp.sum(lq2_ref[...] * lk2_ref[...], axis=-1, keepdims=True)) + lambda_init)
    d = o[:bq] - lam * o[bq:]
    y = d * lax.rsqrt(jnp.mean(d * d, axis=-1, keepdims=True) + SUBLN_EPS) * sub_ref[...]
    o_ref[...] = (y * (1.0 - lambda_init)).astype(o_ref.dtype)


def _flash_scratch(n_chunks, dv, bk, m_rows, qt_shape):
    return [pltpu.VMEM((n_chunks, dv, bk), BF16), pltpu.VMEM(qt_shape, BF16),
            pltpu.VMEM((2, bk, m_rows), F32), pltpu.VMEM((1, m_rows), F32), pltpu.VMEM((1, m_rows), F32),
            pltpu.VMEM((dv, m_rows), F32)]


def _diff_attention(qkv3, b_off, nb, shared_out, lam_params, subln, lambda_init, *, bq=512, bk=1024):
    n_seq, seq, three_d = qkv3.shape
    d = three_d // 3
    n_pairs = d // (2 * HEAD_DIM)
    w = 2 * HEAD_DIM
    bq, bk = min(bq, seq), min(bk, seq // 2)
    n_chunks = seq // bk
    vec = pl.BlockSpec((1, HEAD_DIM), lambda b, t, i: (0, 0))
    blk = 2 * (2 * seq * w * 2 + 2 * bq * w * 2) + seq * w * 2 + 8 * bk * 2 * bq * 4
    return _attention_call(
        functools.partial(_diff_kernel, n_chunks=n_chunks, bq=bq, bk=bk, lambda_init=lambda_init),
        grid=(nb, n_pairs, seq // bq),
        in_specs=[pl.BlockSpec((None, bq, w), lambda b, t, i: (b + b_off, i, t)),
                  pl.BlockSpec((None, seq, w), lambda b, t, i: (b + b_off, 0, d // w + t)),
                  pl.BlockSpec((None, seq, w), lambda b, t, i: (b + b_off, 0, 2 * d // w + t)),
                  vec, vec, vec, vec,
                  pl.BlockSpec((1, w), lambda b, t, i: (0, 0))],
        args=[qkv3, qkv3, qkv3, *[p.reshape(1, HEAD_DIM).astype(F32) for p in lam_params],
              subln.reshape(1, w).astype(F32)],
        out_block=(None, bq, w), out_index=lambda b, t, i: (i, t),
        b_off=b_off, shared_out=shared_out, out_view=(n_seq, seq, d),
        scratch=_flash_scratch(n_chunks, w, bk, 2 * bq, (2, HEAD_DIM, bq)),
        semantics=("parallel", "parallel", "arbitrary"), block_bytes=blk, name="diff_attention")


def _gqa_kernel(q_ref, k_ref, v_ref, o_ref, vt_ref, qt_ref, s_ref, m_ref, l_ref, acc_ref, *,
                n_chunks, bq, bk, group):
    _build_vt(v_ref, vt_ref, n_chunks, bk)
    for g in range(group):
        qt_ref[:, g * bq:(g + 1) * bq] = _transpose_bf16(q_ref[:, g * HEAD_DIM:(g + 1) * HEAD_DIM])

    def score_t(c):
        return jnp.dot(k_ref[_rows(c * bk, bk), :], qt_ref[...], preferred_element_type=F32)

    ot = _attend_t(score_t, vt_ref, s_ref, m_ref, l_ref, acc_ref, n_chunks)
    for g in range(group):
        o_ref[:, g * HEAD_DIM:(g + 1) * HEAD_DIM] = ot[:, g * bq:(g + 1) * bq].T.astype(o_ref.dtype)


def _gqa_attention(qkv3, b_off, nb, shared_out, d, *, bq=256, bk=1024):
    n_seq, seq, n_cols = qkv3.shape
    kv_dim = (n_cols - d) // 2
    n_kv = kv_dim // HEAD_DIM
    group = d // kv_dim
    w = group * HEAD_DIM
    bq, bk = min(bq, seq), min(bk, seq // 2)
    n_chunks = seq // bk
    blk = 2 * (2 * seq * HEAD_DIM * 2 + 2 * bq * w * 2) + seq * HEAD_DIM * 2 + 8 * bk * group * bq * 4
    return _attention_call(
        functools.partial(_gqa_kernel, n_chunks=n_chunks, bq=bq, bk=bk, group=group),
        grid=(nb, n_kv, seq // bq),
        in_specs=[pl.BlockSpec((None, bq, w), lambda b, n, i: (b + b_off, i, n)),
                  pl.BlockSpec((None, seq, HEAD_DIM), lambda b, n, i: (b + b_off, 0, d // HEAD_DIM + n)),
                  pl.BlockSpec((None, seq, HEAD_DIM),
                               lambda b, n, i: (b + b_off, 0, (d + kv_dim) // HEAD_DIM + n))],
        args=[qkv3, qkv3, qkv3],
        out_block=(None, bq, w), out_index=lambda b, n, i: (i, n),
        b_off=b_off, shared_out=shared_out, out_view=(n_seq, seq, d),
        scratch=_flash_scratch(n_chunks, HEAD_DIM, bk, group * bq, (HEAD_DIM, group * bq)),
        semantics=("parallel", "parallel", "arbitrary"), block_bytes=blk, name="gqa_attention")


def _nat_bias_table(rel_bias, rows):
    n_heads = rel_bias.shape[0]
    kh = min(NA_WIN_H, rows)
    n_dr = 2 * NA_WIN_H - 1
    col = np.arange(GRID_W)
    col_start = np.clip(col - NA_WIN_W // 2, 0, GRID_W - NA_WIN_W)
    col_ok = (col[None, :] >= col_start[:, None]) & (col[None, :] < col_start[:, None] + NA_WIN_W)
    dc = np.clip(col[None, :] - col[:, None], -(NA_WIN_W - 1), NA_WIN_W - 1) + NA_WIN_W - 1
    planes = jnp.take(rel_bias.astype(F32) * LOG2_E, jnp.asarray(dc), axis=2)
    planes = jnp.where(jnp.asarray(col_ok)[None, None], planes, NEG_BIG)
    planes = jnp.concatenate([planes, jnp.full((n_heads, 1, GRID_W, GRID_W), NEG_BIG, F32)], axis=1)
    plane_idx = np.full((3, NAT_G, NAT_KROWS), n_dr, np.int32)
    for var, r_base in enumerate((0, NAT_G, rows - NAT_G)):
        a = int(np.clip(r_base - NA_WIN_H // 2, 0, rows - NAT_KROWS))
        for g in range(NAT_G):
            r = r_base + g
            r0 = int(np.clip(r - kh // 2, 0, rows - kh))
            for j in range(NAT_KROWS):
                if r0 <= a + j < r0 + kh:
                    plane_idx[var, g, j] = a + j - r + NA_WIN_H - 1
    tbl = jnp.take(planes, jnp.asarray(plane_idx), axis=1)
    tbl = jnp.transpose(tbl, (0, 1, 2, 4, 3, 5))
    return tbl.reshape(n_heads, 3, NAT_G * GRID_W, NAT_KROWS * GRID_W)


def _nat_kernel(q_ref, k_ref, v_ref, bias_ref, o_ref, *, rows):
    n_groups = rows // NAT_G
    gq = NAT_G * GRID_W
    gk = NAT_KROWS * GRID_W

    def body(gi, carry):
        a = jnp.clip(gi * NAT_G - NA_WIN_H // 2, 0, rows - NAT_KROWS)
        var = jnp.where(gi == 0, 0, jnp.where(gi == n_groups - 1, 2, 1))
        qs = pl.multiple_of(gi * gq, gq)
        ks = pl.multiple_of(a * GRID_W, GRID_W)
        q = q_ref[pl.ds(qs, gq), :]
        s = _dot_nt(q, k_ref[pl.ds(ks, gk), :]) + bias_ref[var]
        e = jnp.exp2(s - jnp.max(s, axis=1, keepdims=True))
        l = jnp.sum(e, axis=1, keepdims=True)
        o = jnp.dot(e.astype(BF16), v_ref[pl.ds(ks, gk), :], preferred_element_type=F32) / l
        o_ref[pl.ds(qs, gq), :] = o.astype(o_ref.dtype)
        return carry

    lax.fori_loop(0, n_groups, body, 0, unroll=2)


def _nat_attention(qkv3, b_off, nb, shared_out, bias_tbl):
    n_seq, seq, three_d = qkv3.shape
    d = three_d // 3
    n_heads = d // HEAD_DIM
    rows = seq // GRID_W
    assert rows % NAT_G == 0 and rows >= NAT_KROWS and rows >= 3 * NAT_G
    gq, gk = NAT_G * GRID_W, NAT_KROWS * GRID_W
    blk = 2 * (4 * seq * HEAD_DIM * 2 + 3 * gq * gk * 4) + 6 * gq * gk * 4

    def col_spec(col0):
        return pl.BlockSpec((None, seq, HEAD_DIM), lambda b, h: (b + b_off, 0, col0 + h))

    return _attention_call(
        functools.partial(_nat_kernel, rows=rows),
        grid=(nb, n_heads),
        in_specs=[col_spec(0), col_spec(n_heads), col_spec(2 * n_heads),
                  pl.BlockSpec((None, 3, gq, gk), lambda b, h: (h, 0, 0, 0))],
        args=[qkv3, qkv3, qkv3, bias_tbl],
        out_block=(None, seq, HEAD_DIM), out_index=lambda b, h: (0, h),
        b_off=b_off, shared_out=shared_out, out_view=(n_seq, seq, d),
        semantics=("parallel", "parallel"), block_bytes=blk, name="nat_attention")


def _lambda_init(layer_idx):
    return 0.8 - 0.6 * math.exp(-0.3 * layer_idx)


def _seq_view(arr, seq):
    t, c = arr.shape
    return arr.reshape(t // seq, seq, c)


def _trunk(x_prompt, x_sample, layers, final_norm):
    bp, sp, d = x_prompt.shape
    bs, ss, _ = x_sample.shape
    tp, ts = bp * sp, bs * ss
    t = tp + ts
    assert tp % ss == 0 and t % sp == 0 and t % ss == 0
    groups = ((sp, 0, bp), (ss, tp // ss, bs))
    x = jnp.concatenate([x_prompt.reshape(tp, d), x_sample.reshape(ts, d)], axis=0)
    pos = jnp.concatenate([jnp.tile(jnp.arange(sp), bp), jnp.tile(jnp.arange(ss), bs)])

    def per_group(fn):
        out = None
        for seq, b_off, nb in groups:
            out = fn(seq, b_off, nb, out).reshape(t, d)
        return out

    xb, sumsq = _cast_stats(x)
    for li, p in enumerate(layers):
        kind = p["kind"]
        w_qkv = _fold_gain(p["norm_mix"], p["w_qkv"])
        if kind == "nat":
            qkv = _project(xb, sumsq, w_qkv, epilogue="qkv", n_q_cols=d, q_scale=QK_SCALE_LOG2)
            tbls = {seq: _nat_bias_table(p["rel_bias"], seq // GRID_W) for seq, _, _ in groups}
            o = per_group(lambda seq, b_off, nb, out: _nat_attention(
                _seq_view(qkv, seq), b_off, nb, out, tbls[seq]))
        elif kind == "diff":
            qkv = _project(xb, sumsq, w_qkv, epilogue="qkv_rope", n_q_cols=d, n_k_cols=d, q_scale=QK_SCALE_LOG2,
                           rope=_partial_rope_tables(pos))
            lam_params = (p["lq1"], p["lk1"], p["lq2"], p["lk2"])
            o = per_group(lambda seq, b_off, nb, out: _diff_attention(
                _seq_view(qkv, seq), b_off, nb, out, lam_params, p["subln"], _lambda_init(li)))
        else:
            kv_dim = (w_qkv.shape[1] - d) // 2
            norm_w = jnp.stack([p["q_norm"], p["k_norm"]]).astype(F32)
            qkv = _project(xb, sumsq, w_qkv, epilogue="qkv_rope", n_q_cols=d, n_k_cols=kv_dim,
                           q_scale=QK_SCALE_LOG2, rope=_axial_rope_tables(pos), qk_norm_w=norm_w)
            o = per_group(lambda seq, b_off, nb, out: _gqa_attention(
                _seq_view(qkv, seq), b_off, nb, out, d))
        x, xb, sumsq = _residual_matmul(o, p["w_o"].astype(BF16), x)
        u = _project(xb, sumsq, _fold_gain(p["norm_mlp"], p["w_up"]), epilogue="relu2")
        x, xb, sumsq = _residual_matmul(u, p["w_down"].astype(BF16), x, emit_stats=li + 1 < len(layers))

    y_prompt = _rmsnorm(x, final_norm, F32, row0=0, n_rows=tp)
    y_sample = _rmsnorm(x, final_norm, F32, row0=tp, n_rows=ts)
    return y_prompt.reshape(bp, sp, d), y_sample.reshape(bs, ss, d)


def kernel(x_prompt, x_sample, l0_norm_mix, l0_w_qkv, l0_rel_bias, l0_w_o, l0_norm_mlp, l0_w_up, l0_w_down, l1_norm_mix, l1_w_qkv, l1_lambda_q1, l1_lambda_k1, l1_lambda_q2, l1_lambda_k2, l1_subln, l1_w_o, l1_norm_mlp, l1_w_up, l1_w_down, l2_norm_mix, l2_w_qkv, l2_q_norm, l2_k_norm, l2_w_o, l2_norm_mlp, l2_w_up, l2_w_down, l3_norm_mix, l3_w_qkv, l3_rel_bias, l3_w_o, l3_norm_mlp, l3_w_up, l3_w_down, final_norm):
    layers = [
        {"kind": "nat", "norm_mix": l0_norm_mix, "w_qkv": l0_w_qkv, "rel_bias": l0_rel_bias, "w_o": l0_w_o,
         "norm_mlp": l0_norm_mlp, "w_up": l0_w_up, "w_down": l0_w_down},
        {"kind": "diff", "norm_mix": l1_norm_mix, "w_qkv": l1_w_qkv, "lq1": l1_lambda_q1, "lk1": l1_lambda_k1,
         "lq2": l1_lambda_q2, "lk2": l1_lambda_k2, "subln": l1_subln, "w_o": l1_w_o,
         "norm_mlp": l1_norm_mlp, "w_up": l1_w_up, "w_down": l1_w_down},
        {"kind": "gqa", "norm_mix": l2_norm_mix, "w_qkv": l2_w_qkv, "q_norm": l2_q_norm, "k_norm": l2_k_norm,
         "w_o": l2_w_o, "norm_mlp": l2_norm_mlp, "w_up": l2_w_up, "w_down": l2_w_down},
        {"kind": "nat", "norm_mix": l3_norm_mix, "w_qkv": l3_w_qkv, "rel_bias": l3_rel_bias, "w_o": l3_w_o,
         "norm_mlp": l3_norm_mlp, "w_up": l3_w_up, "w_down": l3_w_down},
    ]
    return _trunk(x_prompt, x_sample, layers, final_norm)
```

```python
import functools
import math

import numpy as np
import jax
import jax.numpy as jnp
from jax import lax
from jax.experimental import pallas as pl
from jax.experimental.pallas import tpu as pltpu

F32 = jnp.float32
BF16 = jnp.bfloat16

HEAD_DIM = 128
GRID_W = 64
NA_WIN_H = 8
NA_WIN_W = 16
GQA_GROUP = 4
ROPE_THETA = 500000.0
PARTIAL_ROT = HEAD_DIM // 4
AXIAL_THETA = 10000.0
AXIAL_DIM = HEAD_DIM // 2
NORM_EPS = 1e-6
SUBLN_EPS = 1e-5
NEG_BIG = -1e30
LOG2_E = math.log2(math.e)
QK_SCALE_LOG2 = HEAD_DIM ** -0.5 * LOG2_E

NAT_G = 4
NAT_KROWS = NAT_G + NA_WIN_H

V7X_VMEM_BYTES = 64 * 1024 * 1024
VMEM_CAP_BYTES = V7X_VMEM_BYTES - 2 * 1024 * 1024


def _params(semantics, block_bytes):
    limit = min(VMEM_CAP_BYTES, int(block_bytes * 1.2) + (8 << 20))
    return pltpu.CompilerParams(dimension_semantics=semantics, vmem_limit_bytes=limit)


def _divisor_block(n, target, align):
    best = None
    for b in range(align, min(n, target) + 1, align):
        if n % b == 0:
            best = b
    assert best is not None, (n, target, align)
    return best


def _dot_nt(a, b):
    return lax.dot_general(a, b, (((1,), (1,)), ((), ())), preferred_element_type=F32)


def _rmsnorm_kernel(x_ref, w_ref, o_ref, *, eps):
    x = x_ref[...]
    y = x * lax.rsqrt(jnp.mean(x * x, axis=-1, keepdims=True) + eps)
    o_ref[...] = (y * w_ref[...]).astype(o_ref.dtype)


def _rmsnorm(x, w, out_dtype, *, row0=0, n_rows=None, bt=256):
    t, d = x.shape
    n_rows = t if n_rows is None else n_rows
    bt = math.gcd(math.gcd(bt, n_rows), row0) if row0 else min(bt, n_rows)
    blk0 = row0 // bt
    blk = bt * d * (4 + jnp.dtype(out_dtype).itemsize) * 2
    return pl.pallas_call(
        functools.partial(_rmsnorm_kernel, eps=NORM_EPS),
        grid=(n_rows // bt,),
        in_specs=[pl.BlockSpec((bt, d), lambda i: (i + blk0, 0)),
                  pl.BlockSpec((1, d), lambda i: (0, 0))],
        out_specs=pl.BlockSpec((bt, d), lambda i: (i, 0)),
        out_shape=jax.ShapeDtypeStruct((n_rows, d), out_dtype),
        compiler_params=_params(("parallel",), blk),
        name="rmsnorm",
    )(x, w.reshape(1, d).astype(F32))


def _fold_lanes(v):
    acc = v[:, 0:HEAD_DIM]
    for g in range(1, v.shape[1] // HEAD_DIM):
        acc = acc + v[:, g * HEAD_DIM:(g + 1) * HEAD_DIM]
    return acc


def _cast_stats_kernel(x_ref, xb_ref, ss_ref):
    x = x_ref[...]
    xb_ref[...] = x.astype(BF16)
    ss_ref[...] = _fold_lanes(x * x)


def _cast_stats(x, *, bt=256):
    t, d = x.shape
    bt = min(bt, t)
    return pl.pallas_call(
        _cast_stats_kernel,
        grid=(t // bt,),
        in_specs=[pl.BlockSpec((bt, d), lambda i: (i, 0))],
        out_specs=[pl.BlockSpec((bt, d), lambda i: (i, 0)), pl.BlockSpec((bt, HEAD_DIM), lambda i: (i, 0))],
        out_shape=[jax.ShapeDtypeStruct((t, d), BF16), jax.ShapeDtypeStruct((t, HEAD_DIM), F32)],
        compiler_params=_params(("parallel",), bt * d * 6 * 2),
        name="cast_stats",
    )(x)


def _fold_gain(gain, w):
    return (gain.astype(F32)[:, None] * w).astype(BF16)


def _proj_kernel(*refs, d_model, epilogue, n_q_blocks, n_k_blocks, q_scale, rope_shift, qk_norm,
                 heads_per_block):
    if epilogue == "qkv_rope":
        a_ref, b_ref, ss_ref, c_ref, sa_ref, sb_ref = refs[:6]
        nw_ref = refs[6] if qk_norm else None
    else:
        a_ref, b_ref, ss_ref = refs[:3]
    o_ref = refs[-1]
    rstd = lax.rsqrt(jnp.sum(ss_ref[...], axis=-1, keepdims=True) / d_model + NORM_EPS)
    y = jnp.dot(a_ref[...], b_ref[...], preferred_element_type=F32) * rstd
    if epilogue == "relu2":
        o_ref[...] = jnp.square(jnp.maximum(y, 0.0)).astype(o_ref.dtype)
        return
    j = pl.program_id(1)
    is_q = j < n_q_blocks
    sc = jnp.where(is_q, q_scale, 1.0).astype(F32)
    if epilogue == "qkv":
        o_ref[...] = (y * sc).astype(o_ref.dtype)
        return
    is_qk = j < n_q_blocks + n_k_blocks
    c = c_ref[...]
    sa = sa_ref[...]
    sb = sb_ref[...]
    if qk_norm:
        gain = jnp.where(is_q, nw_ref[0:1, :], jnp.where(is_qk, nw_ref[1:2, :], 1.0))
    for hh in range(heads_per_block):
        lanes = slice(hh * HEAD_DIM, (hh + 1) * HEAD_DIM)
        x = y[:, lanes]
        if qk_norm:
            inv = lax.rsqrt(jnp.mean(x * x, axis=-1, keepdims=True) + NORM_EPS)
            x = x * jnp.where(is_qk, inv, 1.0) * gain
        x = x * c + pltpu.roll(x, HEAD_DIM - rope_shift, 1) * sa + pltpu.roll(x, rope_shift, 1) * sb
        o_ref[:, lanes] = (x * sc).astype(o_ref.dtype)


def _project(xb, sumsq, w, *, epilogue, n_q_cols=0, n_k_cols=0, q_scale=1.0, rope=None, qk_norm_w=None,
             bm=1024, bn=1024):
    m, k = xb.shape
    _, n = w.shape
    bm, bn = _divisor_block(m, bm, 8), _divisor_block(n, bn, HEAD_DIM)
    if n_q_cols:
        bn = math.gcd(math.gcd(bn, n_q_cols), n_k_cols) if n_k_cols else math.gcd(bn, n_q_cols)
    in_specs = [pl.BlockSpec((bm, k), lambda i, j: (i, 0)),
                pl.BlockSpec((k, bn), lambda i, j: (0, j)),
                pl.BlockSpec((bm, HEAD_DIM), lambda i, j: (i, 0))]
    args = [xb, w, sumsq]
    shift = 0
    n_qk_blocks = (n_q_cols + n_k_cols) // bn
    if epilogue == "qkv_rope":
        c, sa, sb, shift = rope
        tab_spec = pl.BlockSpec((None, bm, HEAD_DIM), lambda i, j: (jnp.where(j < n_qk_blocks, 0, 1), i, 0))
        in_specs += [tab_spec, tab_spec, tab_spec]
        args += [jnp.stack([c, jnp.ones_like(c)]), jnp.stack([sa, jnp.zeros_like(sa)]),
                 jnp.stack([sb, jnp.zeros_like(sb)])]
        if qk_norm_w is not None:
            in_specs.append(pl.BlockSpec((2, HEAD_DIM), lambda i, j: (0, 0)))
            args.append(qk_norm_w)
    blk = 2 * (bm * k * 2 + k * bn * 2 + bm * bn * 2 + 4 * bm * HEAD_DIM * 4) + 2 * bm * bn * 4
    return pl.pallas_call(
        functools.partial(_proj_kernel, d_model=k, epilogue=epilogue, n_q_blocks=n_q_cols // bn,
                          n_k_blocks=n_k_cols // bn, q_scale=q_scale, rope_shift=shift,
                          qk_norm=qk_norm_w is not None, heads_per_block=bn // HEAD_DIM),
        grid=(m // bm, n // bn),
        in_specs=in_specs,
        out_specs=pl.BlockSpec((bm, bn), lambda i, j: (i, j)),
        out_shape=jax.ShapeDtypeStruct((m, n), BF16),
        compiler_params=_params(("parallel", "arbitrary"), blk),
        name="proj_" + epilogue,
    )(*args)


def _resid_kernel(*refs, nk, emit_stats):
    if emit_stats:
        a_ref, b_ref, r_ref, x_ref, xb_ref, ss_ref = refs
    else:
        a_ref, b_ref, r_ref, x_ref = refs
    part = jnp.dot(a_ref[...], b_ref[...], preferred_element_type=F32)
    j = pl.program_id(1)

    def finish(val):
        x_ref[...] = val
        if not emit_stats:
            return
        xb_ref[...] = val.astype(BF16)
        sq = _fold_lanes(val * val)

        @pl.when(j == 0)
        def _():
            ss_ref[...] = sq

        @pl.when(j > 0)
        def _():
            ss_ref[...] += sq

    if nk == 1:
        finish(r_ref[...] + part)
        return
    k = pl.program_id(2)

    @pl.when(k == 0)
    def _():
        x_ref[...] = r_ref[...] + part

    @pl.when(jnp.logical_and(k > 0, k < nk - 1))
    def _():
        x_ref[...] += part

    @pl.when(k == nk - 1)
    def _():
        finish(x_ref[...] + part)


def _residual_matmul(a, w, x, *, emit_stats=True, bm=1024, bn=1024, bk_max=4096):
    m, k = a.shape
    _, n = w.shape
    bm, bn = _divisor_block(m, bm, 8), _divisor_block(n, bn, HEAD_DIM)
    if emit_stats and k > bk_max:
        x_new, _, _ = _residual_matmul(a, w, x, emit_stats=False, bm=bm, bn=bn, bk_max=bk_max)
        return (x_new,) + tuple(_cast_stats(x_new))

    def footprint(bk):
        byt = 2 * (bm * bk * 2 + bk * bn * 2 + 2 * bm * bn * 4) + 2 * bm * bn * 4
        if emit_stats:
            byt += 2 * (bm * bn * 2 + bm * HEAD_DIM * 4)
        return byt

    bk = _divisor_block(k, bk_max, HEAD_DIM)
    while footprint(bk) > VMEM_CAP_BYTES and bk % (2 * HEAD_DIM) == 0:
        bk //= 2
    nk = k // bk
    blk = footprint(bk)
    tile = pl.BlockSpec((bm, bn), lambda i, j, kk: (i, j))
    out_specs = [tile]
    out_shape = [jax.ShapeDtypeStruct((m, n), F32)]
    if emit_stats:
        out_specs += [tile, pl.BlockSpec((bm, HEAD_DIM), lambda i, j, kk: (i, 0))]
        out_shape += [jax.ShapeDtypeStruct((m, n), BF16), jax.ShapeDtypeStruct((m, HEAD_DIM), F32)]
    out = pl.pallas_call(
        functools.partial(_resid_kernel, nk=nk, emit_stats=emit_stats),
        grid=(m // bm, n // bn, nk),
        in_specs=[pl.BlockSpec((bm, bk), lambda i, j, kk: (i, kk)),
                  pl.BlockSpec((bk, bn), lambda i, j, kk: (kk, j)),
                  tile],
        out_specs=out_specs,
        out_shape=out_shape,
        compiler_params=_params(("parallel", "arbitrary", "arbitrary"), blk),
        name="resid_matmul",
    )(a, w, x)
    return tuple(out) if emit_stats else (out[0], None, None)


def _rope_angles(pos, dim, theta):
    inv = theta ** (-jnp.arange(0, dim, 2, dtype=F32) / dim)
    ang = pos.astype(F32)[:, None] * inv[None, :]
    return jnp.cos(ang), jnp.sin(ang)


def _partial_rope_tables(pos):
    cos, sin = _rope_angles(pos, PARTIAL_ROT, ROPE_THETA)
    t, h = cos.shape
    rest = HEAD_DIM - 2 * h
    c = jnp.concatenate([cos, cos, jnp.ones((t, rest), F32)], axis=1)
    sa = jnp.concatenate([-sin, jnp.zeros((t, HEAD_DIM - h), F32)], axis=1)
    sb = jnp.concatenate([jnp.zeros((t, h), F32), sin, jnp.zeros((t, rest), F32)], axis=1)
    return c, sa, sb, h


def _axial_rope_tables(pos):
    rcos, rsin = _rope_angles(pos // GRID_W, AXIAL_DIM, AXIAL_THETA)
    ccos, csin = _rope_angles(pos % GRID_W, AXIAL_DIM, AXIAL_THETA)
    z = jnp.zeros_like(rsin)
    c = jnp.concatenate([rcos, rcos, ccos, ccos], axis=1)
    sa = jnp.concatenate([-rsin, z, -csin, z], axis=1)
    sb = jnp.concatenate([z, rsin, z, csin], axis=1)
    return c, sa, sb, rcos.shape[1]


def _rows(start, size):
    if isinstance(start, int):
        return pl.ds(start, size)
    return pl.ds(pl.multiple_of(start, size), size)


def _attention_call(kernel_fn, *, grid, in_specs, args, out_block, out_index, b_off, shared_out, out_view,
                    scratch=(), semantics, block_bytes, name):
    n_in = len(args)
    aliases = {}
    if shared_out is not None:
        inner = kernel_fn

        def kernel_fn(*refs):
            return inner(*refs[:n_in], *refs[n_in + 1:])

        in_specs = list(in_specs) + [pl.BlockSpec(memory_space=pl.ANY)]
        args = list(args) + [shared_out.reshape(out_view)]
        aliases = {n_in: 0}

    def out_map(b, *rest):
        return (b + b_off,) + tuple(out_index(b, *rest))

    return pl.pallas_call(
        kernel_fn,
        grid=grid,
        in_specs=in_specs,
        out_specs=pl.BlockSpec(out_block, out_map),
        out_shape=jax.ShapeDtypeStruct(out_view, BF16),
        scratch_shapes=list(scratch),
        input_output_aliases=aliases,
        compiler_params=_params(semantics, block_bytes),
        name=name,
    )(*args)


def _transpose_bf16(x):
    return x.astype(F32).T.astype(BF16)


def _build_vt(v_ref, vt_ref, n_chunks, bk):
    @pl.when(pl.program_id(2) == 0)
    def _():
        def body(c, carry):
            vt_ref[c] = _transpose_bf16(v_ref[_rows(c * bk, bk), :])
            return carry

        lax.fori_loop(0, n_chunks, body, 0)


def _attend_t(score_t, vt_ref, s_ref, m_ref, l_ref, acc_ref, n_chunks):
    assert n_chunks % 2 == 0

    def scores(c, slot):
        s = score_t(c)
        s_ref[slot] = s
        return jnp.max(s, axis=0, keepdims=True)

    def softmax_pv(c, slot, chunk_max):
        m_old = m_ref[...]
        m_new = jnp.maximum(m_old, chunk_max)
        alpha = jnp.exp2(m_old - m_new)
        p = jnp.exp2(s_ref[slot] - m_new)
        l_ref[...] = alpha * l_ref[...] + jnp.sum(p, axis=0, keepdims=True)
        acc_ref[...] = alpha * acc_ref[...] + jnp.dot(vt_ref[c], p.astype(BF16), preferred_element_type=F32)
        m_ref[...] = m_new

    def pair(c0, max0, last):
        max1 = scores(c0 + 1, 1)
        softmax_pv(c0, 0, max0)
        next_max = None if last else scores(c0 + 2, 0)
        softmax_pv(c0 + 1, 1, max1)
        return next_max

    m_ref[...] = jnp.full(m_ref.shape, NEG_BIG, F32)
    l_ref[...] = jnp.zeros(l_ref.shape, F32)
    acc_ref[...] = jnp.zeros(acc_ref.shape, F32)
    max0 = scores(0, 0)
    max0 = lax.fori_loop(0, n_chunks // 2 - 1, lambda ci, mx: pair(2 * ci, mx, False), max0)
    pair(n_chunks - 2, max0, True)
    return acc_ref[...] / l_ref[...]


def _diff_kernel(q_ref, k_ref, v_ref, lq1_ref, lk1_ref, lq2_ref, lk2_ref, sub_ref, o_ref,
                 vt_ref, qt_ref, s_ref, m_ref, l_ref, acc_ref, *, n_chunks, bq, bk, lambda_init):
    _build_vt(v_ref, vt_ref, n_chunks, bk)
    for h in range(2):
        qt_ref[h] = _transpose_bf16(q_ref[:, h * HEAD_DIM:(h + 1) * HEAD_DIM])

    def score_t(c):
        rows = _rows(c * bk, bk)
        s1 = jnp.dot(k_ref[rows, 0:HEAD_DIM], qt_ref[0], preferred_element_type=F32)
        s2 = jnp.dot(k_ref[rows, HEAD_DIM:2 * HEAD_DIM], qt_ref[1], preferred_element_type=F32)
        return jnp.concatenate([s1, s2], axis=1)

    o = _attend_t(score_t, vt_ref, s_ref, m_ref, l_ref, acc_ref, n_chunks).T
    lam = (jnp.exp(jnp.sum(lq1_ref[...] * lk1_ref[...], axis=-1, keepdims=True))
           - jnp.exp(jnp.sum(lq2_ref[...] * lk2_ref[...], axis=-1, keepdims=True)) + lambda_init)
    d = o[:bq] - lam * o[bq:]
    y = d * lax.rsqrt(jnp.mean(d * d, axis=-1, keepdims=True) + SUBLN_EPS) * sub_ref[...]
    o_ref[...] = (y * (1.0 - lambda_init)).astype(o_ref.dtype)


def _flash_scratch(n_chunks, dv, bk, m_rows, qt_shape):
    return [pltpu.VMEM((n_chunks, dv, bk), BF16), pltpu.VMEM(qt_shape, BF16),
            pltpu.VMEM((2, bk, m_rows), F32), pltpu.VMEM((1, m_rows), F32), pltpu.VMEM((1, m_rows), F32),
            pltpu.VMEM((dv, m_rows), F32)]


def _diff_attention(qkv3, b_off, nb, shared_out, lam_params, subln, lambda_init, *, bq=512, bk=1024):
    n_seq, seq, three_d = qkv3.shape
    d = three_d // 3
    n_pairs = d // (2 * HEAD_DIM)
    w = 2 * HEAD_DIM
    bq, bk = min(bq, seq), min(bk, seq // 2)
    n_chunks = seq // bk
    vec = pl.BlockSpec((1, HEAD_DIM), lambda b, t, i: (0, 0))
    blk = 2 * (2 * seq * w * 2 + 2 * bq * w * 2) + seq * w * 2 + 8 * bk * 2 * bq * 4
    return _attention_call(
        functools.partial(_diff_kernel, n_chunks=n_chunks, bq=bq, bk=bk, lambda_init=lambda_init),
        grid=(nb, n_pairs, seq // bq),
        in_specs=[pl.BlockSpec((None, bq, w), lambda b, t, i: (b + b_off, i, t)),
                  pl.BlockSpec((None, seq, w), lambda b, t, i: (b + b_off, 0, d // w + t)),
                  pl.BlockSpec((None, seq, w), lambda b, t, i: (b + b_off, 0, 2 * d // w + t)),
                  vec, vec, vec, vec,
                  pl.BlockSpec((1, w), lambda b, t, i: (0, 0))],
        args=[qkv3, qkv3, qkv3, *[p.reshape(1, HEAD_DIM).astype(F32) for p in lam_params],
              subln.reshape(1, w).astype(F32)],
        out_block=(None, bq, w), out_index=lambda b, t, i: (i, t),
        b_off=b_off, shared_out=shared_out, out_view=(n_seq, seq, d),
        scratch=_flash_scratch(n_chunks, w, bk, 2 * bq, (2, HEAD_DIM, bq)),
        semantics=("parallel", "parallel", "arbitrary"), block_bytes=blk, name="diff_attention")


def _gqa_kernel(q_ref, k_ref, v_ref, o_ref, vt_ref, qt_ref, s_ref, m_ref, l_ref, acc_ref, *,
                n_chunks, bq, bk, group):
    _build_vt(v_ref, vt_ref, n_chunks, bk)
    for g in range(group):
        qt_ref[:, g * bq:(g + 1) * bq] = _transpose_bf16(q_ref[:, g * HEAD_DIM:(g + 1) * HEAD_DIM])

    def score_t(c):
        return jnp.dot(k_ref[_rows(c * bk, bk), :], qt_ref[...], preferred_element_type=F32)

    ot = _attend_t(score_t, vt_ref, s_ref, m_ref, l_ref, acc_ref, n_chunks)
    for g in range(group):
        o_ref[:, g * HEAD_DIM:(g + 1) * HEAD_DIM] = ot[:, g * bq:(g + 1) * bq].T.astype(o_ref.dtype)


def _gqa_attention(qkv3, b_off, nb, shared_out, d, *, bq=256, bk=1024):
    n_seq, seq, n_cols = qkv3.shape
    kv_dim = (n_cols - d) // 2
    n_kv = kv_dim // HEAD_DIM
    group = d // kv_dim
    w = group * HEAD_DIM
    bq, bk = min(bq, seq), min(bk, seq // 2)
    n_chunks = seq // bk
    blk = 2 * (2 * seq * HEAD_DIM * 2 + 2 * bq * w * 2) + seq * HEAD_DIM * 2 + 8 * bk * group * bq * 4
    return _attention_call(
        functools.partial(_gqa_kernel, n_chunks=n_chunks, bq=bq, bk=bk, group=group),
        grid=(nb, n_kv, seq // bq),
        in_specs=[pl.BlockSpec((None, bq, w), lambda b, n, i: (b + b_off, i, n)),
                  pl.BlockSpec((None, seq, HEAD_DIM), lambda b, n, i: (b + b_off, 0, d // HEAD_DIM + n)),
                  pl.BlockSpec((None, seq, HEAD_DIM),
                               lambda b, n, i: (b + b_off, 0, (d + kv_dim) // HEAD_DIM + n))],
        args=[qkv3, qkv3, qkv3],
        out_block=(None, bq, w), out_index=lambda b, n, i: (i, n),
        b_off=b_off, shared_out=shared_out, out_view=(n_seq, seq, d),
        scratch=_flash_scratch(n_chunks, HEAD_DIM, bk, group * bq, (HEAD_DIM, group * bq)),
        semantics=("parallel", "parallel", "arbitrary"), block_bytes=blk, name="gqa_attention")


def _nat_bias_table(rel_bias, rows):
    n_heads = rel_bias.shape[0]
    kh = min(NA_WIN_H, rows)
    n_dr = 2 * NA_WIN_H - 1
    col = np.arange(GRID_W)
    col_start = np.clip(col - NA_WIN_W // 2, 0, GRID_W - NA_WIN_W)
    col_ok = (col[None, :] >= col_start[:, None]) & (col[None, :] < col_start[:, None] + NA_WIN_W)
    dc = np.clip(col[None, :] - col[:, None], -(NA_WIN_W - 1), NA_WIN_W - 1) + NA_WIN_W - 1
    planes = jnp.take(rel_bias.astype(F32) * LOG2_E, jnp.asarray(dc), axis=2)
    planes = jnp.where(jnp.asarray(col_ok)[None, None], planes, NEG_BIG)
    planes = jnp.concatenate([planes, jnp.full((n_heads, 1, GRID_W, GRID_W), NEG_BIG, F32)], axis=1)
    plane_idx = np.full((3, NAT_G, NAT_KROWS), n_dr, np.int32)
    for var, r_base in enumerate((0, NAT_G, rows - NAT_G)):
        a = int(np.clip(r_base - NA_WIN_H // 2, 0, rows - NAT_KROWS))
        for g in range(NAT_G):
            r = r_base + g
            r0 = int(np.clip(r - kh // 2, 0, rows - kh))
            for j in range(NAT_KROWS):
                if r0 <= a + j < r0 + kh:
                    plane_idx[var, g, j] = a + j - r + NA_WIN_H - 1
    tbl = jnp.take(planes, jnp.asarray(plane_idx), axis=1)
    tbl = jnp.transpose(tbl, (0, 1, 2, 4, 3, 5))
    return tbl.reshape(n_heads, 3, NAT_G * GRID_W, NAT_KROWS * GRID_W)


def _nat_kernel(q_ref, k_ref, v_ref, bias_ref, o_ref, s_ref, *, rows):
    n_groups = rows // NAT_G
    gq = NAT_G * GRID_W
    gk = NAT_KROWS * GRID_W
    assert n_groups % 2 == 0

    def window(gi):
        a = jnp.clip(gi * NAT_G - NA_WIN_H // 2, 0, rows - NAT_KROWS)
        return pl.ds(pl.multiple_of(a * GRID_W, GRID_W), gk)

    def scores(gi, slot):
        var = jnp.where(gi == 0, 0, jnp.where(gi == n_groups - 1, 2, 1))
        s_ref[slot] = _dot_nt(q_ref[_rows(gi * gq, gq), :], k_ref[window(gi), :]) + bias_ref[var]

    def softmax_pv(gi, slot):
        s = s_ref[slot]
        e = jnp.exp2(s - jnp.max(s, axis=1, keepdims=True))
        l = jnp.sum(e, axis=1, keepdims=True)
        o = jnp.dot(e.astype(BF16), v_ref[window(gi), :], preferred_element_type=F32) / l
        o_ref[_rows(gi * gq, gq), :] = o.astype(o_ref.dtype)

    def pair(g0, last):
        scores(g0 + 1, 1)
        softmax_pv(g0, 0)
        if not last:
            scores(g0 + 2, 0)
        softmax_pv(g0 + 1, 1)

    scores(0, 0)

    def body(pi, carry):
        pair(2 * pi, False)
        return carry

    lax.fori_loop(0, n_groups // 2 - 1, body, 0)
    pair(n_groups - 2, True)


def _nat_attention(qkv3, b_off, nb, shared_out, bias_tbl):
    n_seq, seq, three_d = qkv3.shape
    d = three_d // 3
    n_heads = d // HEAD_DIM
    rows = seq // GRID_W
    assert rows % NAT_G == 0 and rows >= NAT_KROWS and rows >= 3 * NAT_G
    gq, gk = NAT_G * GRID_W, NAT_KROWS * GRID_W
    blk = 2 * (4 * seq * HEAD_DIM * 2 + 3 * gq * gk * 4) + 6 * gq * gk * 4

    def col_spec(col0):
        return pl.BlockSpec((None, seq, HEAD_DIM), lambda b, h: (b + b_off, 0, col0 + h))

    return _attention_call(
        functools.partial(_nat_kernel, rows=rows),
        grid=(nb, n_heads),
        in_specs=[col_spec(0), col_spec(n_heads), col_spec(2 * n_heads),
                  pl.BlockSpec((None, 3, gq, gk), lambda b, h: (h, 0, 0, 0))],
        args=[qkv3, qkv3, qkv3, bias_tbl],
        out_block=(None, seq, HEAD_DIM), out_index=lambda b, h: (0, h),
        b_off=b_off, shared_out=shared_out, out_view=(n_seq, seq, d),
        scratch=[pltpu.VMEM((2, gq, gk), F32)],
        semantics=("parallel", "parallel"), block_bytes=blk, name="nat_attention")


def _lambda_init(layer_idx):
    return 0.8 - 0.6 * math.exp(-0.3 * layer_idx)


def _seq_view(arr, seq):
    t, c = arr.shape
    return arr.reshape(t // seq, seq, c)


def _trunk(x_prompt, x_sample, layers, final_norm):
    bp, sp, d = x_prompt.shape
    bs, ss, _ = x_sample.shape
    tp, ts = bp * sp, bs * ss
    t = tp + ts
    assert tp % ss == 0 and t % sp == 0 and t % ss == 0
    groups = ((sp, 0, bp), (ss, tp // ss, bs))
    x = jnp.concatenate([x_prompt.reshape(tp, d), x_sample.reshape(ts, d)], axis=0)
    pos = jnp.concatenate([jnp.tile(jnp.arange(sp), bp), jnp.tile(jnp.arange(ss), bs)])

    def per_group(fn):
        out = None
        for seq, b_off, nb in groups:
            out = fn(seq, b_off, nb, out).reshape(t, d)
        return out

    xb, sumsq = _cast_stats(x)
    for li, p in enumerate(layers):
        kind = p["kind"]
        w_qkv = _fold_gain(p["norm_mix"], p["w_qkv"])
        if kind == "nat":
            qkv = _project(xb, sumsq, w_qkv, epilogue="qkv", n_q_cols=d, q_scale=QK_SCALE_LOG2)
            tbls = {seq: _nat_bias_table(p["rel_bias"], seq // GRID_W) for seq, _, _ in groups}
            o = per_group(lambda seq, b_off, nb, out: _nat_attention(
                _seq_view(qkv, seq), b_off, nb, out, tbls[seq]))
        elif kind == "diff":
            qkv = _project(xb, sumsq, w_qkv, epilogue="qkv_rope", n_q_cols=d, n_k_cols=d, q_scale=QK_SCALE_LOG2,
                           rope=_partial_rope_tables(pos))
            lam_params = (p["lq1"], p["lk1"], p["lq2"], p["lk2"])
            o = per_group(lambda seq, b_off, nb, out: _diff_attention(
                _seq_view(qkv, seq), b_off, nb, out, lam_params, p["subln"], _lambda_init(li)))
        else:
            kv_dim = (w_qkv.shape[1] - d) // 2
            norm_w = jnp.stack([p["q_norm"], p["k_norm"]]).astype(F32)
            qkv = _project(xb, sumsq, w_qkv, epilogue="qkv_rope", n_q_cols=d, n_k_cols=kv_dim,
                           q_scale=QK_SCALE_LOG2, rope=_axial_rope_tables(pos), qk_norm_w=norm_w)
            o = per_group(lambda seq, b_off, nb, out: _gqa_attention(
                _seq_view(qkv, seq), b_off, nb, out, d))
        x, xb, sumsq = _residual_matmul(o, p["w_o"].astype(BF16), x)
        u = _project(xb, sumsq, _fold_gain(p["norm_mlp"], p["w_up"]), epilogue="relu2")
        x, xb, sumsq = _residual_matmul(u, p["w_down"].astype(BF16), x, emit_stats=li + 1 < len(layers))

    y_prompt = _rmsnorm(x, final_norm, F32, row0=0, n_rows=tp)
    y_sample = _rmsnorm(x, final_norm, F32, row0=tp, n_rows=ts)
    return y_prompt.reshape(bp, sp, d), y_sample.reshape(bs, ss, d)


def kernel(x_prompt, x_sample, l0_norm_mix, l0_w_qkv, l0_rel_bias, l0_w_o, l0_norm_mlp, l0_w_up, l0_w_down, l1_norm_mix, l1_w_qkv, l1_lambda_q1, l1_lambda_k1, l1_lambda_q2, l1_lambda_k2, l1_subln, l1_w_o, l1_norm_mlp, l1_w_up, l1_w_down, l2_norm_mix, l2_w_qkv, l2_q_norm, l2_k_norm, l2_w_o, l2_norm_mlp, l2_w_up, l2_w_down, l3_norm_mix, l3_w_qkv, l3_rel_bias, l3_w_o, l3_norm_mlp, l3_w_up, l3_w_down, final_norm):
    layers = [
        {"kind": "nat", "norm_mix": l0_norm_mix, "w_qkv": l0_w_qkv, "rel_bias": l0_rel_bias, "w_o": l0_w_o,
         "norm_mlp": l0_norm_mlp, "w_up": l0_w_up, "w_down": l0_w_down},
        {"kind": "diff", "norm_mix": l1_norm_mix, "w_qkv": l1_w_qkv, "lq1": l1_lambda_q1, "lk1": l1_lambda_k1,
         "lq2": l1_lambda_q2, "lk2": l1_lambda_k2, "subln": l1_subln, "w_o": l1_w_o,
         "norm_mlp": l1_norm_mlp, "w_up": l1_w_up, "w_down": l1_w_down},
        {"kind": "gqa", "norm_mix": l2_norm_mix, "w_qkv": l2_w_qkv, "q_norm": l2_q_norm, "k_norm": l2_k_norm,
         "w_o": l2_w_o, "norm_mlp": l2_norm_mlp, "w_up": l2_w_up, "w_down": l2_w_down},
        {"kind": "nat", "norm_mix": l3_norm_mix, "w_qkv": l3_w_qkv, "rel_bias": l3_rel_bias, "w_o": l3_w_o,
         "norm_mlp": l3_norm_mlp, "w_up": l3_w_up, "w_down": l3_w_down},
    ]
    return _trunk(x_prompt, x_sample, layers, final_norm)
```

```python
import functools
import math

import numpy as np
import jax
import jax.numpy as jnp
from jax import lax
from jax.experimental import pallas as pl
from jax.experimental.pallas import tpu as pltpu

F32 = jnp.float32
BF16 = jnp.bfloat16

HEAD_DIM = 128
GRID_W = 64
NA_WIN_H = 8
NA_WIN_W = 16
ROPE_THETA = 500000.0
PARTIAL_ROT = HEAD_DIM // 4
AXIAL_THETA = 10000.0
AXIAL_DIM = HEAD_DIM // 2
NORM_EPS = 1e-6
SUBLN_EPS = 1e-5
NEG_BIG = -1e30
LOG2_E = math.log2(math.e)
QK_SCALE_LOG2 = HEAD_DIM ** -0.5 * LOG2_E

NAT_G = 4
NAT_KROWS = NAT_G + NA_WIN_H

V7X_VMEM_BYTES = 64 * 1024 * 1024
VMEM_CAP_BYTES = V7X_VMEM_BYTES - 2 * 1024 * 1024


def _params(semantics, block_bytes):
    limit = min(VMEM_CAP_BYTES, int(block_bytes * 1.2) + (8 << 20))
    return pltpu.CompilerParams(dimension_semantics=semantics, vmem_limit_bytes=limit)


def _divisor_block(n, target, align):
    best = None
    for b in range(align, min(n, target) + 1, align):
        if n % b == 0:
            best = b
    assert best is not None, (n, target, align)
    return best


def _dot_nt(a, b):
    return lax.dot_general(a, b, (((1,), (1,)), ((), ())), preferred_element_type=F32)


def _rmsnorm_kernel(x_ref, w_ref, o_ref, *, eps):
    x = x_ref[...]
    y = x * lax.rsqrt(jnp.mean(x * x, axis=-1, keepdims=True) + eps)
    o_ref[...] = (y * w_ref[...]).astype(o_ref.dtype)


def _rmsnorm(x, w, out_dtype, *, row0=0, n_rows=None, bt=256):
    t, d = x.shape
    n_rows = t if n_rows is None else n_rows
    bt = math.gcd(math.gcd(bt, n_rows), row0) if row0 else min(bt, n_rows)
    blk0 = row0 // bt
    blk = bt * d * (4 + jnp.dtype(out_dtype).itemsize) * 2
    return pl.pallas_call(
        functools.partial(_rmsnorm_kernel, eps=NORM_EPS),
        grid=(n_rows // bt,),
        in_specs=[pl.BlockSpec((bt, d), lambda i: (i + blk0, 0)),
                  pl.BlockSpec((1, d), lambda i: (0, 0))],
        out_specs=pl.BlockSpec((bt, d), lambda i: (i, 0)),
        out_shape=jax.ShapeDtypeStruct((n_rows, d), out_dtype),
        compiler_params=_params(("parallel",), blk),
        name="rmsnorm",
    )(x, w.reshape(1, d).astype(F32))


def _fold_lanes(v):
    acc = v[:, 0:HEAD_DIM]
    for g in range(1, v.shape[1] // HEAD_DIM):
        acc = acc + v[:, g * HEAD_DIM:(g + 1) * HEAD_DIM]
    return acc


def _cast_stats_kernel(x_ref, xb_ref, ss_ref):
    x = x_ref[...]
    xb_ref[...] = x.astype(BF16)
    ss_ref[...] = _fold_lanes(x * x)


def _cast_stats(x, *, bt=256):
    t, d = x.shape
    bt = min(bt, t)
    return pl.pallas_call(
        _cast_stats_kernel,
        grid=(t // bt,),
        in_specs=[pl.BlockSpec((bt, d), lambda i: (i, 0))],
        out_specs=[pl.BlockSpec((bt, d), lambda i: (i, 0)), pl.BlockSpec((bt, HEAD_DIM), lambda i: (i, 0))],
        out_shape=[jax.ShapeDtypeStruct((t, d), BF16), jax.ShapeDtypeStruct((t, HEAD_DIM), F32)],
        compiler_params=_params(("parallel",), bt * d * 6 * 2),
        name="cast_stats",
    )(x)


def _proj_kernel(*refs, d_model, epilogue, fold_weight, n_q_blocks, n_k_blocks, q_scale, rope_shift, qk_norm,
                 heads_per_block):
    refs = list(refs)
    a_ref, w_ref = refs[:2]
    del refs[:2]
    gain_ref = refs.pop(0) if fold_weight else None
    ss_ref = refs.pop(0)
    if epilogue == "qkv_rope":
        c_ref, sa_ref, sb_ref = refs[:3]
        del refs[:3]
        nw_ref = refs.pop(0) if qk_norm else None
    o_ref = refs.pop(0)
    if fold_weight:
        b = (gain_ref[...] * w_ref[...]).astype(BF16)
        refs.pop(0)[...] = b
    else:
        b = w_ref[...]
    rstd = lax.rsqrt(jnp.sum(ss_ref[...], axis=-1, keepdims=True) / d_model + NORM_EPS)
    if epilogue != "qkv_rope":
        y = jnp.dot(a_ref[...], b, preferred_element_type=F32) * rstd
        if epilogue == "relu2":
            o_ref[...] = jnp.square(jnp.maximum(y, 0.0)).astype(o_ref.dtype)
        else:
            sc = jnp.where(pl.program_id(1) < n_q_blocks, q_scale, 1.0).astype(F32)
            o_ref[...] = (y * sc).astype(o_ref.dtype)
        return
    j = pl.program_id(1)
    is_q = j < n_q_blocks
    is_qk = j < n_q_blocks + n_k_blocks
    sc = jnp.where(is_q, q_scale, 1.0).astype(F32)
    c = c_ref[...]
    sa = sa_ref[...]
    sb = sb_ref[...]
    if qk_norm:
        gain = jnp.where(is_q, nw_ref[0:1, :], jnp.where(is_qk, nw_ref[1:2, :], 1.0))
    y = jnp.dot(a_ref[...], b, preferred_element_type=F32) * rstd
    for hh in range(heads_per_block):
        lanes = slice(hh * HEAD_DIM, (hh + 1) * HEAD_DIM)
        x = y[:, lanes]
        if qk_norm:
            inv = lax.rsqrt(jnp.mean(x * x, axis=-1, keepdims=True) + NORM_EPS)
            x = x * jnp.where(is_qk, inv, 1.0) * gain
        x = x * c + pltpu.roll(x, HEAD_DIM - rope_shift, 1) * sa + pltpu.roll(x, rope_shift, 1) * sb
        o_ref[:, lanes] = (x * sc).astype(o_ref.dtype)


def _project(xb, sumsq, w, gain, *, epilogue, n_q_cols=0, n_k_cols=0, q_scale=1.0, rope=None, qk_norm_w=None,
             bm=1024, bn=1024, bn_fold=512):
    m, k = xb.shape
    _, n = w.shape
    bm = _divisor_block(m, bm, 8)
    if epilogue == "qkv_rope":
        c, sa, sb, shift = rope
        tables = [jnp.stack([c, jnp.ones_like(c)]), jnp.stack([sa, jnp.zeros_like(sa)]),
                  jnp.stack([sb, jnp.zeros_like(sb)])]
    else:
        shift, tables = 0, []

    def call(fold_weight, weight, row_blk0, n_row_blks, bn, shared_out):
        bn = _divisor_block(n, bn, HEAD_DIM)
        if n_q_cols:
            bn = math.gcd(math.gcd(bn, n_q_cols), n_k_cols) if n_k_cols else math.gcd(bn, n_q_cols)
        n_qk_blocks = (n_q_cols + n_k_cols) // bn
        in_specs = [pl.BlockSpec((bm, k), lambda i, j: (i + row_blk0, 0)),
                    pl.BlockSpec((k, bn), lambda i, j: (0, j))]
        args = [xb, weight]
        if fold_weight:
            in_specs.append(pl.BlockSpec((k, 1), lambda i, j: (0, 0)))
            args.append(gain.reshape(k, 1).astype(F32))
        in_specs.append(pl.BlockSpec((bm, HEAD_DIM), lambda i, j: (i + row_blk0, 0)))
        args.append(sumsq)
        if tables:
            tab_spec = pl.BlockSpec((None, bm, HEAD_DIM),
                                    lambda i, j: (jnp.where(j < n_qk_blocks, 0, 1), i + row_blk0, 0))
            in_specs += [tab_spec] * 3
            args += tables
            if qk_norm_w is not None:
                in_specs.append(pl.BlockSpec((2, HEAD_DIM), lambda i, j: (0, 0)))
                args.append(qk_norm_w)
        out_specs = [pl.BlockSpec((bm, bn), lambda i, j: (i + row_blk0, j))]
        out_shape = [jax.ShapeDtypeStruct((m, n), BF16)]
        w_bytes = 4 if fold_weight else 2
        blk = 2 * (bm * k * 2 + k * bn * w_bytes + bm * bn * 2 + 4 * bm * HEAD_DIM * 4) + 2 * bm * bn * 4
        aliases = {}
        if fold_weight:
            out_specs.append(pl.BlockSpec((k, bn), lambda i, j: (0, j)))
            out_shape.append(jax.ShapeDtypeStruct((k, n), BF16))
            blk += 3 * k * bn * 2 + k * HEAD_DIM * 4
        kernel_fn = functools.partial(
            _proj_kernel, d_model=k, epilogue=epilogue, fold_weight=fold_weight, n_q_blocks=n_q_cols // bn,
            n_k_blocks=n_k_cols // bn, q_scale=q_scale, rope_shift=shift, qk_norm=qk_norm_w is not None,
            heads_per_block=bn // HEAD_DIM)
        if shared_out is not None:
            n_in = len(args)
            inner = kernel_fn

            def kernel_fn(*refs):
                return inner(*refs[:n_in], *refs[n_in + 1:])

            in_specs.append(pl.BlockSpec(memory_space=pl.ANY))
            args.append(shared_out)
            aliases = {n_in: 0}
        return pl.pallas_call(
            kernel_fn,
            grid=(n_row_blks, n // bn),
            in_specs=in_specs,
            out_specs=out_specs,
            out_shape=out_shape,
            input_output_aliases=aliases,
            compiler_params=_params(("parallel", "arbitrary"), blk),
            name="proj_" + epilogue + ("_fold" if fold_weight else ""),
        )(*args)

    out, w_bf16 = call(True, w, 0, 1, bn_fold, None)
    if m // bm > 1:
        out = call(False, w_bf16, 1, m // bm - 1, bn, out)[0]
    return out


def _resid_kernel(*refs, nk, emit_stats):
    if emit_stats:
        a_ref, b_ref, r_ref, x_ref, xb_ref, ss_ref = refs
    else:
        a_ref, b_ref, r_ref, x_ref = refs
    part = jnp.dot(a_ref[...], b_ref[...], preferred_element_type=F32)
    j = pl.program_id(1)

    def finish(val):
        x_ref[...] = val
        if not emit_stats:
            return
        xb_ref[...] = val.astype(BF16)
        sq = _fold_lanes(val * val)

        @pl.when(j == 0)
        def _():
            ss_ref[...] = sq

        @pl.when(j > 0)
        def _():
            ss_ref[...] += sq

    if nk == 1:
        finish(r_ref[...] + part)
        return
    k = pl.program_id(2)

    @pl.when(k == 0)
    def _():
        x_ref[...] = r_ref[...] + part

    @pl.when(jnp.logical_and(k > 0, k < nk - 1))
    def _():
        x_ref[...] += part

    @pl.when(k == nk - 1)
    def _():
        finish(x_ref[...] + part)


def _residual_matmul(a, w, x, *, emit_stats=True, bm=1024, bn=1024, bk_max=4096):
    m, k = a.shape
    _, n = w.shape
    bm, bn = _divisor_block(m, bm, 8), _divisor_block(n, bn, HEAD_DIM)
    if emit_stats and k > bk_max:
        x_new, _, _ = _residual_matmul(a, w, x, emit_stats=False, bm=bm, bn=bn, bk_max=bk_max)
        return (x_new,) + tuple(_cast_stats(x_new))

    def footprint(bk):
        byt = 2 * (bm * bk * 2 + bk * bn * 2 + 2 * bm * bn * 4) + 2 * bm * bn * 4
        if emit_stats:
            byt += 2 * (bm * bn * 2 + bm * HEAD_DIM * 4)
        return byt

    bk = _divisor_block(k, bk_max, HEAD_DIM)
    while footprint(bk) > VMEM_CAP_BYTES and bk % (2 * HEAD_DIM) == 0:
        bk //= 2
    nk = k // bk
    blk = footprint(bk)
    tile = pl.BlockSpec((bm, bn), lambda i, j, kk: (i, j))
    out_specs = [tile]
    out_shape = [jax.ShapeDtypeStruct((m, n), F32)]
    if emit_stats:
        out_specs += [tile, pl.BlockSpec((bm, HEAD_DIM), lambda i, j, kk: (i, 0))]
        out_shape += [jax.ShapeDtypeStruct((m, n), BF16), jax.ShapeDtypeStruct((m, HEAD_DIM), F32)]
    out = pl.pallas_call(
        functools.partial(_resid_kernel, nk=nk, emit_stats=emit_stats),
        grid=(m // bm, n // bn, nk),
        in_specs=[pl.BlockSpec((bm, bk), lambda i, j, kk: (i, kk)),
                  pl.BlockSpec((bk, bn), lambda i, j, kk: (kk, j)),
                  tile],
        out_specs=out_specs,
        out_shape=out_shape,
        compiler_params=_params(("parallel", "arbitrary", "arbitrary"), blk),
        name="resid_matmul",
    )(a, w, x)
    return tuple(out) if emit_stats else (out[0], None, None)


def _rope_angles(pos, dim, theta):
    inv = theta ** (-jnp.arange(0, dim, 2, dtype=F32) / dim)
    ang = pos.astype(F32)[:, None] * inv[None, :]
    return jnp.cos(ang), jnp.sin(ang)


def _partial_rope_tables(pos):
    cos, sin = _rope_angles(pos, PARTIAL_ROT, ROPE_THETA)
    t, h = cos.shape
    rest = HEAD_DIM - 2 * h
    c = jnp.concatenate([cos, cos, jnp.ones((t, rest), F32)], axis=1)
    sa = jnp.concatenate([-sin, jnp.zeros((t, HEAD_DIM - h), F32)], axis=1)
    sb = jnp.concatenate([jnp.zeros((t, h), F32), sin, jnp.zeros((t, rest), F32)], axis=1)
    return c, sa, sb, h


def _axial_rope_tables(pos):
    rcos, rsin = _rope_angles(pos // GRID_W, AXIAL_DIM, AXIAL_THETA)
    ccos, csin = _rope_angles(pos % GRID_W, AXIAL_DIM, AXIAL_THETA)
    z = jnp.zeros_like(rsin)
    c = jnp.concatenate([rcos, rcos, ccos, ccos], axis=1)
    sa = jnp.concatenate([-rsin, z, -csin, z], axis=1)
    sb = jnp.concatenate([z, rsin, z, csin], axis=1)
    return c, sa, sb, rcos.shape[1]


def _rows(start, size):
    if isinstance(start, int):
        return pl.ds(start, size)
    return pl.ds(pl.multiple_of(start, size), size)


def _attention_call(kernel_fn, *, grid, in_specs, args, out_block, out_index, b_off, shared_out, out_view,
                    scratch=(), semantics, block_bytes, name):
    n_in = len(args)
    aliases = {}
    if shared_out is not None:
        inner = kernel_fn

        def kernel_fn(*refs):
            return inner(*refs[:n_in], *refs[n_in + 1:])

        in_specs = list(in_specs) + [pl.BlockSpec(memory_space=pl.ANY)]
        args = list(args) + [shared_out.reshape(out_view)]
        aliases = {n_in: 0}

    def out_map(b, *rest):
        return (b + b_off,) + tuple(out_index(b, *rest))

    return pl.pallas_call(
        kernel_fn,
        grid=grid,
        in_specs=in_specs,
        out_specs=pl.BlockSpec(out_block, out_map),
        out_shape=jax.ShapeDtypeStruct(out_view, BF16),
        scratch_shapes=list(scratch),
        input_output_aliases=aliases,
        compiler_params=_params(semantics, block_bytes),
        name=name,
    )(*args)


def _transpose_bf16(x):
    return x.astype(F32).T.astype(BF16)


def _build_vt(v_ref, vt_ref, n_chunks, bk):
    @pl.when(pl.program_id(2) == 0)
    def _():
        def body(c, carry):
            vt_ref[c] = _transpose_bf16(v_ref[_rows(c * bk, bk), :])
            return carry

        lax.fori_loop(0, n_chunks, body, 0)


def _attend_t(score_t, vt_ref, s_ref, m_ref, l_ref, acc_ref, n_chunks):
    assert n_chunks % 2 == 0

    def scores(c, slot):
        s = score_t(c)
        s_ref[slot] = s
        return jnp.max(s, axis=0, keepdims=True)

    def softmax_pv(c, slot, chunk_max):
        m_old = m_ref[...]
        m_new = jnp.maximum(m_old, chunk_max)
        alpha = jnp.exp2(m_old - m_new)
        p = jnp.exp2(s_ref[slot] - m_new)
        l_ref[...] = alpha * l_ref[...] + jnp.sum(p, axis=0, keepdims=True)
        acc_ref[...] = alpha * acc_ref[...] + jnp.dot(vt_ref[c], p.astype(BF16), preferred_element_type=F32)
        m_ref[...] = m_new

    def pair(c0, max0, last):
        max1 = scores(c0 + 1, 1)
        softmax_pv(c0, 0, max0)
        next_max = None if last else scores(c0 + 2, 0)
        softmax_pv(c0 + 1, 1, max1)
        return next_max

    m_ref[...] = jnp.full(m_ref.shape, NEG_BIG, F32)
    l_ref[...] = jnp.zeros(l_ref.shape, F32)
    acc_ref[...] = jnp.zeros(acc_ref.shape, F32)
    max0 = scores(0, 0)
    max0 = lax.fori_loop(0, n_chunks // 2 - 1, lambda ci, mx: pair(2 * ci, mx, False), max0)
    pair(n_chunks - 2, max0, True)
    return acc_ref[...] / l_ref[...]


def _diff_kernel(q_ref, k_ref, v_ref, lq1_ref, lk1_ref, lq2_ref, lk2_ref, sub_ref, o_ref,
                 vt_ref, qt_ref, s_ref, m_ref, l_ref, acc_ref, *, n_chunks, bq, bk, lambda_init):
    _build_vt(v_ref, vt_ref, n_chunks, bk)
    for h in range(2):
        qt_ref[h] = _transpose_bf16(q_ref[:, h * HEAD_DIM:(h + 1) * HEAD_DIM])

    def score_t(c):
        rows = _rows(c * bk, bk)
        s1 = jnp.dot(k_ref[rows, 0:HEAD_DIM], qt_ref[0], preferred_element_type=F32)
        s2 = jnp.dot(k_ref[rows, HEAD_DIM:2 * HEAD_DIM], qt_ref[1], preferred_element_type=F32)
        return jnp.concatenate([s1, s2], axis=1)

    o = _attend_t(score_t, vt_ref, s_ref, m_ref, l_ref, acc_ref, n_chunks).T
    lam = (jnp.exp(jnp.sum(lq1_ref[...] * lk1_ref[...], axis=-1, keepdims=True))
           - jnp.exp(jnp.sum(lq2_ref[...] * lk2_ref[...], axis=-1, keepdims=True)) + lambda_init)
    d = o[:bq] - lam * o[bq:]
    y = d * lax.rsqrt(jnp.mean(d * d, axis=-1, keepdims=True) + SUBLN_EPS) * sub_ref[...]
    o_ref[...] = (y * (1.0 - lambda_init)).astype(o_ref.dtype)


def _flash_scratch(n_chunks, dv, bk, m_rows, qt_shape):
    return [pltpu.VMEM((n_chunks, dv, bk), BF16), pltpu.VMEM(qt_shape, BF16),
            pltpu.VMEM((2, bk, m_rows), F32), pltpu.VMEM((1, m_rows), F32), pltpu.VMEM((1, m_rows), F32),
            pltpu.VMEM((dv, m_rows), F32)]


def _diff_attention(qkv3, b_off, nb, shared_out, lam_params, subln, lambda_init, *, bq=512, bk=1024):
    n_seq, seq, three_d = qkv3.shape
    d = three_d // 3
    n_pairs = d // (2 * HEAD_DIM)
    w = 2 * HEAD_DIM
    bq, bk = min(bq, seq), min(bk, seq // 2)
    n_chunks = seq // bk
    vec = pl.BlockSpec((1, HEAD_DIM), lambda b, t, i: (0, 0))
    blk = 2 * (2 * seq * w * 2 + 2 * bq * w * 2) + seq * w * 2 + 8 * bk * 2 * bq * 4
    return _attention_call(
        functools.partial(_diff_kernel, n_chunks=n_chunks, bq=bq, bk=bk, lambda_init=lambda_init),
        grid=(nb, n_pairs, seq // bq),
        in_specs=[pl.BlockSpec((None, bq, w), lambda b, t, i: (b + b_off, i, t)),
                  pl.BlockSpec((None, seq, w), lambda b, t, i: (b + b_off, 0, d // w + t)),
                  pl.BlockSpec((None, seq, w), lambda b, t, i: (b + b_off, 0, 2 * d // w + t)),
                  vec, vec, vec, vec,
                  pl.BlockSpec((1, w), lambda b, t, i: (0, 0))],
        args=[qkv3, qkv3, qkv3, *[p.reshape(1, HEAD_DIM).astype(F32) for p in lam_params],
              subln.reshape(1, w).astype(F32)],
        out_block=(None, bq, w), out_index=lambda b, t, i: (i, t),
        b_off=b_off, shared_out=shared_out, out_view=(n_seq, seq, d),
        scratch=_flash_scratch(n_chunks, w, bk, 2 * bq, (2, HEAD_DIM, bq)),
        semantics=("parallel", "parallel", "arbitrary"), block_bytes=blk, name="diff_attention")


def _gqa_kernel(q_ref, k_ref, v_ref, o_ref, vt_ref, qt_ref, s_ref, m_ref, l_ref, acc_ref, *,
                n_chunks, bq, bk, group):
    _build_vt(v_ref, vt_ref, n_chunks, bk)
    for g in range(group):
        qt_ref[:, g * bq:(g + 1) * bq] = _transpose_bf16(q_ref[:, g * HEAD_DIM:(g + 1) * HEAD_DIM])

    def score_t(c):
        return jnp.dot(k_ref[_rows(c * bk, bk), :], qt_ref[...], preferred_element_type=F32)

    ot = _attend_t(score_t, vt_ref, s_ref, m_ref, l_ref, acc_ref, n_chunks)
    for g in range(group):
        o_ref[:, g * HEAD_DIM:(g + 1) * HEAD_DIM] = ot[:, g * bq:(g + 1) * bq].T.astype(o_ref.dtype)


def _gqa_attention(qkv3, b_off, nb, shared_out, d, *, bq=256, bk=1024):
    n_seq, seq, n_cols = qkv3.shape
    kv_dim = (n_cols - d) // 2
    n_kv = kv_dim // HEAD_DIM
    group = d // kv_dim
    w = group * HEAD_DIM
    bq, bk = min(bq, seq), min(bk, seq // 2)
    n_chunks = seq // bk
    blk = 2 * (2 * seq * HEAD_DIM * 2 + 2 * bq * w * 2) + seq * HEAD_DIM * 2 + 8 * bk * group * bq * 4
    return _attention_call(
        functools.partial(_gqa_kernel, n_chunks=n_chunks, bq=bq, bk=bk, group=group),
        grid=(nb, n_kv, seq // bq),
        in_specs=[pl.BlockSpec((None, bq, w), lambda b, n, i: (b + b_off, i, n)),
                  pl.BlockSpec((None, seq, HEAD_DIM), lambda b, n, i: (b + b_off, 0, d // HEAD_DIM + n)),
                  pl.BlockSpec((None, seq, HEAD_DIM),
                               lambda b, n, i: (b + b_off, 0, (d + kv_dim) // HEAD_DIM + n))],
        args=[qkv3, qkv3, qkv3],
        out_block=(None, bq, w), out_index=lambda b, n, i: (i, n),
        b_off=b_off, shared_out=shared_out, out_view=(n_seq, seq, d),
        scratch=_flash_scratch(n_chunks, HEAD_DIM, bk, group * bq, (HEAD_DIM, group * bq)),
        semantics=("parallel", "parallel", "arbitrary"), block_bytes=blk, name="gqa_attention")


def _nat_bias_table(rel_bias, rows):
    n_heads = rel_bias.shape[0]
    kh = min(NA_WIN_H, rows)
    n_dr = 2 * NA_WIN_H - 1
    col = np.arange(GRID_W)
    col_start = np.clip(col - NA_WIN_W // 2, 0, GRID_W - NA_WIN_W)
    col_ok = (col[None, :] >= col_start[:, None]) & (col[None, :] < col_start[:, None] + NA_WIN_W)
    dc = np.clip(col[None, :] - col[:, None], -(NA_WIN_W - 1), NA_WIN_W - 1) + NA_WIN_W - 1
    planes = jnp.take(rel_bias.astype(F32) * LOG2_E, jnp.asarray(dc), axis=2)
    planes = jnp.where(jnp.asarray(col_ok)[None, None], planes, NEG_BIG)
    pad = jnp.full((n_heads, GRID_W, NAT_G * GRID_W), NEG_BIG, F32)
    wide = jnp.concatenate([pad, planes.transpose(0, 2, 1, 3).reshape(n_heads, GRID_W, n_dr * GRID_W), pad],
                           axis=2)
    blocks = []
    for r_base in (0, NAT_G, rows - NAT_G):
        a = int(np.clip(r_base - NA_WIN_H // 2, 0, rows - NAT_KROWS))
        for g in range(NAT_G):
            r = r_base + g
            r0 = int(np.clip(r - kh // 2, 0, rows - kh))
            dr0 = a - r + NA_WIN_H - 1
            assert -NAT_G <= dr0 and dr0 + NAT_KROWS <= n_dr + NAT_G
            row_ok = np.array([r0 <= a + j < r0 + kh for j in range(NAT_KROWS)])
            start = (dr0 + NAT_G) * GRID_W
            window = wide[:, :, start:start + NAT_KROWS * GRID_W]
            blocks.append(jnp.where(jnp.asarray(np.repeat(row_ok, GRID_W))[None, None], window, NEG_BIG))
    return jnp.stack(blocks, axis=1).reshape(n_heads, 3, NAT_G * GRID_W, NAT_KROWS * GRID_W)


def _nat_kernel(q_ref, k_ref, v_ref, bias_ref, o_ref, s_ref, *, rows):
    n_groups = rows // NAT_G
    gq = NAT_G * GRID_W
    gk = NAT_KROWS * GRID_W
    assert n_groups % 2 == 0

    def window(gi):
        a = jnp.clip(gi * NAT_G - NA_WIN_H // 2, 0, rows - NAT_KROWS)
        return pl.ds(pl.multiple_of(a * GRID_W, GRID_W), gk)

    def scores(gi, slot):
        var = jnp.where(gi == 0, 0, jnp.where(gi == n_groups - 1, 2, 1))
        s_ref[slot] = _dot_nt(q_ref[_rows(gi * gq, gq), :], k_ref[window(gi), :]) + bias_ref[var]

    def softmax_pv(gi, slot):
        s = s_ref[slot]
        e = jnp.exp2(s - jnp.max(s, axis=1, keepdims=True))
        l = jnp.sum(e, axis=1, keepdims=True)
        o = jnp.dot(e.astype(BF16), v_ref[window(gi), :], preferred_element_type=F32) / l
        o_ref[_rows(gi * gq, gq), :] = o.astype(o_ref.dtype)

    def pair(g0, last):
        scores(g0 + 1, 1)
        softmax_pv(g0, 0)
        if not last:
            scores(g0 + 2, 0)
        softmax_pv(g0 + 1, 1)

    scores(0, 0)

    def body(pi, carry):
        pair(2 * pi, False)
        return carry

    lax.fori_loop(0, n_groups // 2 - 1, body, 0)
    pair(n_groups - 2, True)


def _nat_attention(qkv3, b_off, nb, shared_out, bias_tbl):
    n_seq, seq, three_d = qkv3.shape
    d = three_d // 3
    n_heads = d // HEAD_DIM
    rows = seq // GRID_W
    assert rows % NAT_G == 0 and rows >= NAT_KROWS and rows >= 3 * NAT_G
    gq, gk = NAT_G * GRID_W, NAT_KROWS * GRID_W
    blk = 2 * (4 * seq * HEAD_DIM * 2 + 3 * gq * gk * 4) + 6 * gq * gk * 4

    def col_spec(col0):
        return pl.BlockSpec((None, seq, HEAD_DIM), lambda b, h: (b + b_off, 0, col0 + h))

    return _attention_call(
        functools.partial(_nat_kernel, rows=rows),
        grid=(nb, n_heads),
        in_specs=[col_spec(0), col_spec(n_heads), col_spec(2 * n_heads),
                  pl.BlockSpec((None, 3, gq, gk), lambda b, h: (h, 0, 0, 0))],
        args=[qkv3, qkv3, qkv3, bias_tbl],
        out_block=(None, seq, HEAD_DIM), out_index=lambda b, h: (0, h),
        b_off=b_off, shared_out=shared_out, out_view=(n_seq, seq, d),
        scratch=[pltpu.VMEM((2, gq, gk), F32)],
        semantics=("parallel", "parallel"), block_bytes=blk, name="nat_attention")


def _lambda_init(layer_idx):
    return 0.8 - 0.6 * math.exp(-0.3 * layer_idx)


def _seq_view(arr, seq):
    t, c = arr.shape
    return arr.reshape(t // seq, seq, c)


def _trunk(x_prompt, x_sample, layers, final_norm):
    bp, sp, d = x_prompt.shape
    bs, ss, _ = x_sample.shape
    tp, ts = bp * sp, bs * ss
    t = tp + ts
    assert tp % ss == 0 and t % sp == 0 and t % ss == 0
    groups = ((sp, 0, bp), (ss, tp // ss, bs))
    x = jnp.concatenate([x_prompt.reshape(tp, d), x_sample.reshape(ts, d)], axis=0)
    pos = jnp.concatenate([jnp.tile(jnp.arange(sp), bp), jnp.tile(jnp.arange(ss), bs)])

    def per_group(fn):
        out = None
        for seq, b_off, nb in groups:
            out = fn(seq, b_off, nb, out).reshape(t, d)
        return out

    xb, sumsq = _cast_stats(x)
    for li, p in enumerate(layers):
        kind = p["kind"]
        w_qkv, g_mix = p["w_qkv"], p["norm_mix"]
        if kind == "nat":
            qkv = _project(xb, sumsq, w_qkv, g_mix, epilogue="qkv", n_q_cols=d, q_scale=QK_SCALE_LOG2)
            tbls = {seq: _nat_bias_table(p["rel_bias"], seq // GRID_W) for seq, _, _ in groups}
            o = per_group(lambda seq, b_off, nb, out: _nat_attention(
                _seq_view(qkv, seq), b_off, nb, out, tbls[seq]))
        elif kind == "diff":
            qkv = _project(xb, sumsq, w_qkv, g_mix, epilogue="qkv_rope", n_q_cols=d, n_k_cols=d,
                           q_scale=QK_SCALE_LOG2, rope=_partial_rope_tables(pos))
            lam_params = (p["lq1"], p["lk1"], p["lq2"], p["lk2"])
            o = per_group(lambda seq, b_off, nb, out: _diff_attention(
                _seq_view(qkv, seq), b_off, nb, out, lam_params, p["subln"], _lambda_init(li)))
        else:
            kv_dim = (w_qkv.shape[1] - d) // 2
            norm_w = jnp.stack([p["q_norm"], p["k_norm"]]).astype(F32)
            qkv = _project(xb, sumsq, w_qkv, g_mix, epilogue="qkv_rope", n_q_cols=d, n_k_cols=kv_dim,
                           q_scale=QK_SCALE_LOG2, rope=_axial_rope_tables(pos), qk_norm_w=norm_w)
            o = per_group(lambda seq, b_off, nb, out: _gqa_attention(
                _seq_view(qkv, seq), b_off, nb, out, d))
        x, xb, sumsq = _residual_matmul(o, p["w_o"].astype(BF16), x)
        u = _project(xb, sumsq, p["w_up"], p["norm_mlp"], epilogue="relu2")
        x, xb, sumsq = _residual_matmul(u, p["w_down"].astype(BF16), x, emit_stats=li + 1 < len(layers))

    y_prompt = _rmsnorm(x, final_norm, F32, row0=0, n_rows=tp)
    y_sample = _rmsnorm(x, final_norm, F32, row0=tp, n_rows=ts)
    return y_prompt.reshape(bp, sp, d), y_sample.reshape(bs, ss, d)


def kernel(x_prompt, x_sample, l0_norm_mix, l0_w_qkv, l0_rel_bias, l0_w_o, l0_norm_mlp, l0_w_up, l0_w_down, l1_norm_mix, l1_w_qkv, l1_lambda_q1, l1_lambda_k1, l1_lambda_q2, l1_lambda_k2, l1_subln, l1_w_o, l1_norm_mlp, l1_w_up, l1_w_down, l2_norm_mix, l2_w_qkv, l2_q_norm, l2_k_norm, l2_w_o, l2_norm_mlp, l2_w_up, l2_w_down, l3_norm_mix, l3_w_qkv, l3_rel_bias, l3_w_o, l3_norm_mlp, l3_w_up, l3_w_down, final_norm):
    layers = [
        {"kind": "nat", "norm_mix": l0_norm_mix, "w_qkv": l0_w_qkv, "rel_bias": l0_rel_bias, "w_o": l0_w_o,
         "norm_mlp": l0_norm_mlp, "w_up": l0_w_up, "w_down": l0_w_down},
        {"kind": "diff", "norm_mix": l1_norm_mix, "w_qkv": l1_w_qkv, "lq1": l1_lambda_q1, "lk1": l1_lambda_k1,
         "lq2": l1_lambda_q2, "lk2": l1_lambda_k2, "subln": l1_subln, "w_o": l1_w_o,
         "norm_mlp": l1_norm_mlp, "w_up": l1_w_up, "w_down": l1_w_down},
        {"kind": "gqa", "norm_mix": l2_norm_mix, "w_qkv": l2_w_qkv, "q_norm": l2_q_norm, "k_norm": l2_k_norm,
         "w_o": l2_w_o, "norm_mlp": l2_norm_mlp, "w_up": l2_w_up, "w_down": l2_w_down},
        {"kind": "nat", "norm_mix": l3_norm_mix, "w_qkv": l3_w_qkv, "rel_bias": l3_rel_bias, "w_o": l3_w_o,
         "norm_mlp": l3_norm_mlp, "w_up": l3_w_up, "w_down": l3_w_down},
    ]
    return _trunk(x_prompt, x_sample, layers, final_norm)
```

```python
import functools
import math

import numpy as np
import jax
import jax.numpy as jnp
from jax import lax
from jax.experimental import pallas as pl
from jax.experimental.pallas import tpu as pltpu

F32 = jnp.float32
BF16 = jnp.bfloat16

HEAD_DIM = 128
GRID_W = 64
NA_WIN_H = 8
NA_WIN_W = 16
ROPE_THETA = 500000.0
PARTIAL_ROT = HEAD_DIM // 4
AXIAL_THETA = 10000.0
AXIAL_DIM = HEAD_DIM // 2
NORM_EPS = 1e-6
SUBLN_EPS = 1e-5
NEG_BIG = -1e30
LOG2_E = math.log2(math.e)
QK_SCALE_LOG2 = HEAD_DIM ** -0.5 * LOG2_E

NAT_G = 4
NAT_KROWS = NAT_G + NA_WIN_H

V7X_VMEM_BYTES = 64 * 1024 * 1024
VMEM_CAP_BYTES = V7X_VMEM_BYTES - 2 * 1024 * 1024


def _params(semantics, block_bytes):
    limit = min(VMEM_CAP_BYTES, int(block_bytes * 1.2) + (8 << 20))
    return pltpu.CompilerParams(dimension_semantics=semantics, vmem_limit_bytes=limit)


def _divisor_block(n, target, align):
    best = None
    for b in range(align, min(n, target) + 1, align):
        if n % b == 0:
            best = b
    assert best is not None, (n, target, align)
    return best


def _dot_nt(a, b):
    return lax.dot_general(a, b, (((1,), (1,)), ((), ())), preferred_element_type=F32)


def _rmsnorm_kernel(x_ref, w_ref, o_ref, *, eps):
    x = x_ref[...]
    y = x * lax.rsqrt(jnp.mean(x * x, axis=-1, keepdims=True) + eps)
    o_ref[...] = (y * w_ref[...]).astype(o_ref.dtype)


def _rmsnorm(x, w, out_dtype, *, row0=0, n_rows=None, bt=256):
    t, d = x.shape
    n_rows = t if n_rows is None else n_rows
    bt = math.gcd(math.gcd(bt, n_rows), row0) if row0 else min(bt, n_rows)
    blk0 = row0 // bt
    blk = bt * d * (4 + jnp.dtype(out_dtype).itemsize) * 2
    return pl.pallas_call(
        functools.partial(_rmsnorm_kernel, eps=NORM_EPS),
        grid=(n_rows // bt,),
        in_specs=[pl.BlockSpec((bt, d), lambda i: (i + blk0, 0)),
                  pl.BlockSpec((1, d), lambda i: (0, 0))],
        out_specs=pl.BlockSpec((bt, d), lambda i: (i, 0)),
        out_shape=jax.ShapeDtypeStruct((n_rows, d), out_dtype),
        compiler_params=_params(("parallel",), blk),
        name="rmsnorm",
    )(x, w.reshape(1, d).astype(F32))


def _fold_lanes(v):
    acc = v[:, 0:HEAD_DIM]
    for g in range(1, v.shape[1] // HEAD_DIM):
        acc = acc + v[:, g * HEAD_DIM:(g + 1) * HEAD_DIM]
    return acc


def _cast_stats_kernel(x_ref, xb_ref, ss_ref):
    x = x_ref[...]
    xb_ref[...] = x.astype(BF16)
    ss_ref[...] = _fold_lanes(x * x)


def _cast_stats(x_parts, *, bt=256):
    d = x_parts[0].shape[1]
    t = sum(xp.shape[0] for xp in x_parts)
    bt = math.gcd(bt, *[xp.shape[0] for xp in x_parts])
    out = None
    blk0 = 0
    for xp in x_parts:
        in_specs = [pl.BlockSpec((bt, d), lambda i: (i, 0))]
        args = [xp]
        kernel_fn = _cast_stats_kernel
        aliases = {}
        if out is not None:
            def kernel_fn(x_ref, xb_prev, ss_prev, xb_ref, ss_ref):
                _cast_stats_kernel(x_ref, xb_ref, ss_ref)

            in_specs += [pl.BlockSpec(memory_space=pl.ANY)] * 2
            args += list(out)
            aliases = {1: 0, 2: 1}
        out = pl.pallas_call(
            kernel_fn,
            grid=(xp.shape[0] // bt,),
            in_specs=in_specs,
            out_specs=[pl.BlockSpec((bt, d), lambda i, b0=blk0: (i + b0, 0)),
                       pl.BlockSpec((bt, HEAD_DIM), lambda i, b0=blk0: (i + b0, 0))],
            out_shape=[jax.ShapeDtypeStruct((t, d), BF16), jax.ShapeDtypeStruct((t, HEAD_DIM), F32)],
            input_output_aliases=aliases,
            compiler_params=_params(("parallel",), bt * d * 6 * 2),
            name="cast_stats",
        )(*args)
        blk0 += xp.shape[0] // bt
    return tuple(out)


def _proj_kernel(*refs, d_model, epilogue, fold_weight, n_q_blocks, n_k_blocks, q_scale, rope_shift, qk_norm,
                 heads_per_block):
    refs = list(refs)
    a_ref, w_ref = refs[:2]
    del refs[:2]
    gain_ref = refs.pop(0) if fold_weight else None
    ss_ref = refs.pop(0)
    if epilogue == "qkv_rope":
        c_ref, sa_ref, sb_ref = refs[:3]
        del refs[:3]
        nw_ref = refs.pop(0) if qk_norm else None
    o_ref = refs.pop(0)
    if fold_weight:
        b = (gain_ref[...] * w_ref[...]).astype(BF16)
        refs.pop(0)[...] = b
    else:
        b = w_ref[...]
    rstd = lax.rsqrt(jnp.sum(ss_ref[...], axis=-1, keepdims=True) / d_model + NORM_EPS)
    if epilogue != "qkv_rope":
        y = jnp.dot(a_ref[...], b, preferred_element_type=F32) * rstd
        if epilogue == "relu2":
            o_ref[...] = jnp.square(jnp.maximum(y, 0.0)).astype(o_ref.dtype)
        else:
            sc = jnp.where(pl.program_id(1) < n_q_blocks, q_scale, 1.0).astype(F32)
            o_ref[...] = (y * sc).astype(o_ref.dtype)
        return
    j = pl.program_id(1)
    is_q = j < n_q_blocks
    is_qk = j < n_q_blocks + n_k_blocks
    sc = jnp.where(is_q, q_scale, 1.0).astype(F32)
    c = c_ref[...]
    sa = sa_ref[...]
    sb = sb_ref[...]
    if qk_norm:
        gain = jnp.where(is_q, nw_ref[0:1, :], jnp.where(is_qk, nw_ref[1:2, :], 1.0))
    y = jnp.dot(a_ref[...], b, preferred_element_type=F32) * rstd
    for hh in range(heads_per_block):
        lanes = slice(hh * HEAD_DIM, (hh + 1) * HEAD_DIM)
        x = y[:, lanes]
        if qk_norm:
            inv = lax.rsqrt(jnp.mean(x * x, axis=-1, keepdims=True) + NORM_EPS)
            x = x * jnp.where(is_qk, inv, 1.0) * gain
        x = x * c + pltpu.roll(x, HEAD_DIM - rope_shift, 1) * sa + pltpu.roll(x, rope_shift, 1) * sb
        o_ref[:, lanes] = (x * sc).astype(o_ref.dtype)


def _project(xb, sumsq, w, gain, *, epilogue, n_q_cols=0, n_k_cols=0, q_scale=1.0, rope=None, qk_norm_w=None,
             bm=1024, bn=1024, bn_fold=512):
    m, k = xb.shape
    _, n = w.shape
    bm = _divisor_block(m, bm, 8)
    if epilogue == "qkv_rope":
        c, sa, sb, shift = rope
        tables = [jnp.stack([c, jnp.ones_like(c)]), jnp.stack([sa, jnp.zeros_like(sa)]),
                  jnp.stack([sb, jnp.zeros_like(sb)])]
    else:
        shift, tables = 0, []

    def call(fold_weight, weight, row_blk0, n_row_blks, bn, shared_out):
        bn = _divisor_block(n, bn, HEAD_DIM)
        if n_q_cols:
            bn = math.gcd(math.gcd(bn, n_q_cols), n_k_cols) if n_k_cols else math.gcd(bn, n_q_cols)
        n_qk_blocks = (n_q_cols + n_k_cols) // bn
        in_specs = [pl.BlockSpec((bm, k), lambda i, j: (i + row_blk0, 0)),
                    pl.BlockSpec((k, bn), lambda i, j: (0, j))]
        args = [xb, weight]
        if fold_weight:
            in_specs.append(pl.BlockSpec((k, 1), lambda i, j: (0, 0)))
            args.append(gain.reshape(k, 1).astype(F32))
        in_specs.append(pl.BlockSpec((bm, HEAD_DIM), lambda i, j: (i + row_blk0, 0)))
        args.append(sumsq)
        if tables:
            tab_spec = pl.BlockSpec((None, bm, HEAD_DIM),
                                    lambda i, j: (jnp.where(j < n_qk_blocks, 0, 1), i + row_blk0, 0))
            in_specs += [tab_spec] * 3
            args += tables
            if qk_norm_w is not None:
                in_specs.append(pl.BlockSpec((2, HEAD_DIM), lambda i, j: (0, 0)))
                args.append(qk_norm_w)
        out_specs = [pl.BlockSpec((bm, bn), lambda i, j: (i + row_blk0, j))]
        out_shape = [jax.ShapeDtypeStruct((m, n), BF16)]
        w_bytes = 4 if fold_weight else 2
        blk = 2 * (bm * k * 2 + k * bn * w_bytes + bm * bn * 2 + 4 * bm * HEAD_DIM * 4) + 2 * bm * bn * 4
        aliases = {}
        if fold_weight:
            out_specs.append(pl.BlockSpec((k, bn), lambda i, j: (0, j)))
            out_shape.append(jax.ShapeDtypeStruct((k, n), BF16))
            blk += 3 * k * bn * 2 + k * HEAD_DIM * 4
        kernel_fn = functools.partial(
            _proj_kernel, d_model=k, epilogue=epilogue, fold_weight=fold_weight, n_q_blocks=n_q_cols // bn,
            n_k_blocks=n_k_cols // bn, q_scale=q_scale, rope_shift=shift, qk_norm=qk_norm_w is not None,
            heads_per_block=bn // HEAD_DIM)
        if shared_out is not None:
            n_in = len(args)
            inner = kernel_fn

            def kernel_fn(*refs):
                return inner(*refs[:n_in], *refs[n_in + 1:])

            in_specs.append(pl.BlockSpec(memory_space=pl.ANY))
            args.append(shared_out)
            aliases = {n_in: 0}
        return pl.pallas_call(
            kernel_fn,
            grid=(n_row_blks, n // bn),
            in_specs=in_specs,
            out_specs=out_specs,
            out_shape=out_shape,
            input_output_aliases=aliases,
            compiler_params=_params(("parallel", "arbitrary"), blk),
            name="proj_" + epilogue + ("_fold" if fold_weight else ""),
        )(*args)

    out, w_bf16 = call(True, w, 0, 1, bn_fold, None)
    if m // bm > 1:
        out = call(False, w_bf16, 1, m // bm - 1, bn, out)[0]
    return out


def _resid_kernel(*refs, nk, emit_stats, fold_weight):
    refs = list(refs)
    a_ref, w_ref, r_ref, x_ref = refs[:4]
    del refs[:4]
    if emit_stats:
        xb_ref, ss_ref = refs[:2]
        del refs[:2]
    if fold_weight:
        b = w_ref[...].astype(BF16)
        refs.pop(0)[...] = b
    else:
        b = w_ref[...]
    part = jnp.dot(a_ref[...], b, preferred_element_type=F32)
    j = pl.program_id(1)

    def finish(val):
        x_ref[...] = val
        if not emit_stats:
            return
        xb_ref[...] = val.astype(BF16)
        sq = _fold_lanes(val * val)

        @pl.when(j == 0)
        def _():
            ss_ref[...] = sq

        @pl.when(j > 0)
        def _():
            ss_ref[...] += sq

    if nk == 1:
        finish(r_ref[...] + part)
        return
    k = pl.program_id(2)

    @pl.when(k == 0)
    def _():
        x_ref[...] = r_ref[...] + part

    @pl.when(jnp.logical_and(k > 0, k < nk - 1))
    def _():
        x_ref[...] += part

    @pl.when(k == nk - 1)
    def _():
        finish(x_ref[...] + part)


def _residual_matmul(a, w, x_parts, *, emit_stats=True, bm=1024, bn=1024, bk_max=4096, bn_fold=512):
    m, k = a.shape
    _, n = w.shape
    bm = _divisor_block(math.gcd(m, *[xp.shape[0] for xp in x_parts]), bm, 8)
    if emit_stats and k > bk_max:
        x_new, _, _ = _residual_matmul(a, w, x_parts, emit_stats=False, bm=bm, bn=bn, bk_max=bk_max,
                                       bn_fold=bn_fold)
        return (x_new,) + tuple(_cast_stats([x_new]))

    def call(fold_weight, weight, r, row_blk0, r_blk0, n_row_blks, bn, shared):
        bn = _divisor_block(n, bn, HEAD_DIM)
        w_bytes = 4 if fold_weight else 2

        def footprint(bk):
            byt = 2 * (bm * bk * 2 + bk * bn * w_bytes + 2 * bm * bn * 4) + 2 * bm * bn * 4
            if emit_stats:
                byt += 2 * (bm * bn * 2 + bm * HEAD_DIM * 4)
            if fold_weight:
                byt += 3 * bk * bn * 2
            return byt

        bk = _divisor_block(k, bk_max, HEAD_DIM)
        while footprint(bk) > VMEM_CAP_BYTES and bk % (2 * HEAD_DIM) == 0:
            bk //= 2
        nk = k // bk
        in_specs = [pl.BlockSpec((bm, bk), lambda i, j, kk: (i + row_blk0, kk)),
                    pl.BlockSpec((bk, bn), lambda i, j, kk: (kk, j)),
                    pl.BlockSpec((bm, bn), lambda i, j, kk: (i + r_blk0, j))]
        args = [a, weight, r]
        tile = pl.BlockSpec((bm, bn), lambda i, j, kk: (i + row_blk0, j))
        out_specs = [tile]
        out_shape = [jax.ShapeDtypeStruct((m, n), F32)]
        if emit_stats:
            out_specs += [tile, pl.BlockSpec((bm, HEAD_DIM), lambda i, j, kk: (i + row_blk0, 0))]
            out_shape += [jax.ShapeDtypeStruct((m, n), BF16), jax.ShapeDtypeStruct((m, HEAD_DIM), F32)]
        n_shared = len(out_specs)
        if fold_weight:
            out_specs.append(pl.BlockSpec((bk, bn), lambda i, j, kk: (kk, j)))
            out_shape.append(jax.ShapeDtypeStruct((k, n), BF16))
        kernel_fn = functools.partial(_resid_kernel, nk=nk, emit_stats=emit_stats, fold_weight=fold_weight)
        aliases = {}
        if shared is not None:
            n_in = len(args)
            inner = kernel_fn

            def kernel_fn(*refs):
                return inner(*refs[:n_in], *refs[n_in + n_shared:])

            in_specs += [pl.BlockSpec(memory_space=pl.ANY)] * n_shared
            args += list(shared)
            aliases = {n_in + t: t for t in range(n_shared)}
        out = pl.pallas_call(
            kernel_fn,
            grid=(n_row_blks, n // bn, nk),
            in_specs=in_specs,
            out_specs=out_specs,
            out_shape=out_shape,
            input_output_aliases=aliases,
            compiler_params=_params(("parallel", "arbitrary", "arbitrary"), footprint(bk)),
            name="resid_matmul" + ("_fold" if fold_weight else ""),
        )(*args)
        return tuple(out[:n_shared]), (out[n_shared] if fold_weight else weight)

    shared, w_bf16 = call(True, w, x_parts[0], 0, 0, 1, bn_fold, None)
    row_blk0 = 0
    for part_idx, xp in enumerate(x_parts):
        n_blks = xp.shape[0] // bm
        skip = 1 if part_idx == 0 else 0
        if n_blks > skip:
            shared, _ = call(False, w_bf16, xp, row_blk0 + skip, skip, n_blks - skip, bn, shared)
        row_blk0 += n_blks
    return shared if emit_stats else (shared[0], None, None)


def _rope_angles(pos, dim, theta):
    inv = theta ** (-jnp.arange(0, dim, 2, dtype=F32) / dim)
    ang = pos.astype(F32)[:, None] * inv[None, :]
    return jnp.cos(ang), jnp.sin(ang)


def _partial_rope_tables(pos):
    cos, sin = _rope_angles(pos, PARTIAL_ROT, ROPE_THETA)
    t, h = cos.shape
    rest = HEAD_DIM - 2 * h
    c = jnp.concatenate([cos, cos, jnp.ones((t, rest), F32)], axis=1)
    sa = jnp.concatenate([-sin, jnp.zeros((t, HEAD_DIM - h), F32)], axis=1)
    sb = jnp.concatenate([jnp.zeros((t, h), F32), sin, jnp.zeros((t, rest), F32)], axis=1)
    return c, sa, sb, h


def _axial_rope_tables(pos):
    rcos, rsin = _rope_angles(pos // GRID_W, AXIAL_DIM, AXIAL_THETA)
    ccos, csin = _rope_angles(pos % GRID_W, AXIAL_DIM, AXIAL_THETA)
    z = jnp.zeros_like(rsin)
    c = jnp.concatenate([rcos, rcos, ccos, ccos], axis=1)
    sa = jnp.concatenate([-rsin, z, -csin, z], axis=1)
    sb = jnp.concatenate([z, rsin, z, csin], axis=1)
    return c, sa, sb, rcos.shape[1]


def _rows(start, size):
    if isinstance(start, int):
        return pl.ds(start, size)
    return pl.ds(pl.multiple_of(start, size), size)


def _attention_call(kernel_fn, *, grid, in_specs, args, out_block, out_index, b_off, shared_out, out_view,
                    scratch=(), semantics, block_bytes, name):
    n_in = len(args)
    aliases = {}
    if shared_out is not None:
        inner = kernel_fn

        def kernel_fn(*refs):
            return inner(*refs[:n_in], *refs[n_in + 1:])

        in_specs = list(in_specs) + [pl.BlockSpec(memory_space=pl.ANY)]
        args = list(args) + [shared_out.reshape(out_view)]
        aliases = {n_in: 0}

    def out_map(b, *rest):
        return (b + b_off,) + tuple(out_index(b, *rest))

    return pl.pallas_call(
        kernel_fn,
        grid=grid,
        in_specs=in_specs,
        out_specs=pl.BlockSpec(out_block, out_map),
        out_shape=jax.ShapeDtypeStruct(out_view, BF16),
        scratch_shapes=list(scratch),
        input_output_aliases=aliases,
        compiler_params=_params(semantics, block_bytes),
        name=name,
    )(*args)


def _transpose_bf16(x):
    return x.astype(F32).T.astype(BF16)


def _build_vt(v_ref, vt_ref, n_chunks, bk):
    @pl.when(pl.program_id(2) == 0)
    def _():
        def body(c, carry):
            vt_ref[c] = _transpose_bf16(v_ref[_rows(c * bk, bk), :])
            return carry

        lax.fori_loop(0, n_chunks, body, 0)


def _attend_t(score_t, vt_ref, s_ref, m_ref, l_ref, acc_ref, n_chunks):
    assert n_chunks % 2 == 0

    def scores(c, slot):
        s = score_t(c)
        s_ref[slot] = s
        return jnp.max(s, axis=0, keepdims=True)

    def softmax_pv(c, slot, chunk_max):
        m_old = m_ref[...]
        m_new = jnp.maximum(m_old, chunk_max)
        alpha = jnp.exp2(m_old - m_new)
        p = jnp.exp2(s_ref[slot] - m_new)
        l_ref[...] = alpha * l_ref[...] + jnp.sum(p, axis=0, keepdims=True)
        acc_ref[...] = alpha * acc_ref[...] + jnp.dot(vt_ref[c], p.astype(BF16), preferred_element_type=F32)
        m_ref[...] = m_new

    def pair(c0, max0, last):
        max1 = scores(c0 + 1, 1)
        softmax_pv(c0, 0, max0)
        next_max = None if last else scores(c0 + 2, 0)
        softmax_pv(c0 + 1, 1, max1)
        return next_max

    m_ref[...] = jnp.full(m_ref.shape, NEG_BIG, F32)
    l_ref[...] = jnp.zeros(l_ref.shape, F32)
    acc_ref[...] = jnp.zeros(acc_ref.shape, F32)
    max0 = scores(0, 0)
    max0 = lax.fori_loop(0, n_chunks // 2 - 1, lambda ci, mx: pair(2 * ci, mx, False), max0)
    pair(n_chunks - 2, max0, True)
    return acc_ref[...] / l_ref[...]


def _diff_kernel(q_ref, k_ref, v_ref, lq1_ref, lk1_ref, lq2_ref, lk2_ref, sub_ref, o_ref,
                 vt_ref, qt_ref, s_ref, m_ref, l_ref, acc_ref, *, n_chunks, bq, bk, lambda_init):
    _build_vt(v_ref, vt_ref, n_chunks, bk)
    for h in range(2):
        qt_ref[h] = _transpose_bf16(q_ref[:, h * HEAD_DIM:(h + 1) * HEAD_DIM])

    def score_t(c):
        rows = _rows(c * bk, bk)
        s1 = jnp.dot(k_ref[rows, 0:HEAD_DIM], qt_ref[0], preferred_element_type=F32)
        s2 = jnp.dot(k_ref[rows, HEAD_DIM:2 * HEAD_DIM], qt_ref[1], preferred_element_type=F32)
        return jnp.concatenate([s1, s2], axis=1)

    o = _attend_t(score_t, vt_ref, s_ref, m_ref, l_ref, acc_ref, n_chunks).T
    lam = (jnp.exp(jnp.sum(lq1_ref[...] * lk1_ref[...], axis=-1, keepdims=True))
           - jnp.exp(jnp.sum(lq2_ref[...] * lk2_ref[...], axis=-1, keepdims=True)) + lambda_init)
    d = o[:bq] - lam * o[bq:]
    y = d * lax.rsqrt(jnp.mean(d * d, axis=-1, keepdims=True) + SUBLN_EPS) * sub_ref[...]
    o_ref[...] = (y * (1.0 - lambda_init)).astype(o_ref.dtype)


def _flash_scratch(n_chunks, dv, bk, m_rows, qt_shape):
    return [pltpu.VMEM((n_chunks, dv, bk), BF16), pltpu.VMEM(qt_shape, BF16),
            pltpu.VMEM((2, bk, m_rows), F32), pltpu.VMEM((1, m_rows), F32), pltpu.VMEM((1, m_rows), F32),
            pltpu.VMEM((dv, m_rows), F32)]


def _diff_attention(qkv3, b_off, nb, shared_out, lam_params, subln, lambda_init, *, bq=512, bk=1024):
    n_seq, seq, three_d = qkv3.shape
    d = three_d // 3
    n_pairs = d // (2 * HEAD_DIM)
    w = 2 * HEAD_DIM
    bq, bk = min(bq, seq), min(bk, seq // 2)
    n_chunks = seq // bk
    vec = pl.BlockSpec((1, HEAD_DIM), lambda b, t, i: (0, 0))
    blk = 2 * (2 * seq * w * 2 + 2 * bq * w * 2) + seq * w * 2 + 8 * bk * 2 * bq * 4
    return _attention_call(
        functools.partial(_diff_kernel, n_chunks=n_chunks, bq=bq, bk=bk, lambda_init=lambda_init),
        grid=(nb, n_pairs, seq // bq),
        in_specs=[pl.BlockSpec((None, bq, w), lambda b, t, i: (b + b_off, i, t)),
                  pl.BlockSpec((None, seq, w), lambda b, t, i: (b + b_off, 0, d // w + t)),
                  pl.BlockSpec((None, seq, w), lambda b, t, i: (b + b_off, 0, 2 * d // w + t)),
                  vec, vec, vec, vec,
                  pl.BlockSpec((1, w), lambda b, t, i: (0, 0))],
        args=[qkv3, qkv3, qkv3, *[p.reshape(1, HEAD_DIM).astype(F32) for p in lam_params],
              subln.reshape(1, w).astype(F32)],
        out_block=(None, bq, w), out_index=lambda b, t, i: (i, t),
        b_off=b_off, shared_out=shared_out, out_view=(n_seq, seq, d),
        scratch=_flash_scratch(n_chunks, w, bk, 2 * bq, (2, HEAD_DIM, bq)),
        semantics=("parallel", "parallel", "arbitrary"), block_bytes=blk, name="diff_attention")


def _gqa_kernel(q_ref, k_ref, v_ref, o_ref, vt_ref, qt_ref, s_ref, m_ref, l_ref, acc_ref, *,
                n_chunks, bq, bk, group):
    _build_vt(v_ref, vt_ref, n_chunks, bk)
    for g in range(group):
        qt_ref[:, g * bq:(g + 1) * bq] = _transpose_bf16(q_ref[:, g * HEAD_DIM:(g + 1) * HEAD_DIM])

    def score_t(c):
        return jnp.dot(k_ref[_rows(c * bk, bk), :], qt_ref[...], preferred_element_type=F32)

    ot = _attend_t(score_t, vt_ref, s_ref, m_ref, l_ref, acc_ref, n_chunks)
    for g in range(group):
        o_ref[:, g * HEAD_DIM:(g + 1) * HEAD_DIM] = ot[:, g * bq:(g + 1) * bq].T.astype(o_ref.dtype)


def _gqa_attention(qkv3, b_off, nb, shared_out, d, *, bq=256, bk=1024):
    n_seq, seq, n_cols = qkv3.shape
    kv_dim = (n_cols - d) // 2
    n_kv = kv_dim // HEAD_DIM
    group = d // kv_dim
    w = group * HEAD_DIM
    bq, bk = min(bq, seq), min(bk, seq // 2)
    n_chunks = seq // bk
    blk = 2 * (2 * seq * HEAD_DIM * 2 + 2 * bq * w * 2) + seq * HEAD_DIM * 2 + 8 * bk * group * bq * 4
    return _attention_call(
        functools.partial(_gqa_kernel, n_chunks=n_chunks, bq=bq, bk=bk, group=group),
        grid=(nb, n_kv, seq // bq),
        in_specs=[pl.BlockSpec((None, bq, w), lambda b, n, i: (b + b_off, i, n)),
                  pl.BlockSpec((None, seq, HEAD_DIM), lambda b, n, i: (b + b_off, 0, d // HEAD_DIM + n)),
                  pl.BlockSpec((None, seq, HEAD_DIM),
                               lambda b, n, i: (b + b_off, 0, (d + kv_dim) // HEAD_DIM + n))],
        args=[qkv3, qkv3, qkv3],
        out_block=(None, bq, w), out_index=lambda b, n, i: (i, n),
        b_off=b_off, shared_out=shared_out, out_view=(n_seq, seq, d),
        scratch=_flash_scratch(n_chunks, HEAD_DIM, bk, group * bq, (HEAD_DIM, group * bq)),
        semantics=("parallel", "parallel", "arbitrary"), block_bytes=blk, name="gqa_attention")


def _nat_bias_table(rel_bias, rows):
    n_heads = rel_bias.shape[0]
    kh = min(NA_WIN_H, rows)
    n_dr = 2 * NA_WIN_H - 1
    col = np.arange(GRID_W)
    col_start = np.clip(col - NA_WIN_W // 2, 0, GRID_W - NA_WIN_W)
    col_ok = (col[None, :] >= col_start[:, None]) & (col[None, :] < col_start[:, None] + NA_WIN_W)
    dc = np.clip(col[None, :] - col[:, None], -(NA_WIN_W - 1), NA_WIN_W - 1) + NA_WIN_W - 1
    planes = jnp.take(rel_bias.astype(F32) * LOG2_E, jnp.asarray(dc), axis=2)
    planes = jnp.where(jnp.asarray(col_ok)[None, None], planes, NEG_BIG)
    pad = jnp.full((n_heads, GRID_W, NAT_G * GRID_W), NEG_BIG, F32)
    wide = jnp.concatenate([pad, planes.transpose(0, 2, 1, 3).reshape(n_heads, GRID_W, n_dr * GRID_W), pad],
                           axis=2)
    blocks = []
    for r_base in (0, NAT_G, rows - NAT_G):
        a = int(np.clip(r_base - NA_WIN_H // 2, 0, rows - NAT_KROWS))
        for g in range(NAT_G):
            r = r_base + g
            r0 = int(np.clip(r - kh // 2, 0, rows - kh))
            dr0 = a - r + NA_WIN_H - 1
            assert -NAT_G <= dr0 and dr0 + NAT_KROWS <= n_dr + NAT_G
            row_ok = np.array([r0 <= a + j < r0 + kh for j in range(NAT_KROWS)])
            start = (dr0 + NAT_G) * GRID_W
            window = wide[:, :, start:start + NAT_KROWS * GRID_W]
            blocks.append(jnp.where(jnp.asarray(np.repeat(row_ok, GRID_W))[None, None], window, NEG_BIG))
    return jnp.stack(blocks, axis=1).reshape(n_heads, 3, NAT_G * GRID_W, NAT_KROWS * GRID_W)


def _nat_kernel(q_ref, k_ref, v_ref, bias_ref, o_ref, s_ref, *, rows):
    n_groups = rows // NAT_G
    gq = NAT_G * GRID_W
    gk = NAT_KROWS * GRID_W
    assert n_groups % 2 == 0

    def window(gi):
        a = jnp.clip(gi * NAT_G - NA_WIN_H // 2, 0, rows - NAT_KROWS)
        return pl.ds(pl.multiple_of(a * GRID_W, GRID_W), gk)

    def scores(gi, slot):
        var = jnp.where(gi == 0, 0, jnp.where(gi == n_groups - 1, 2, 1))
        s_ref[slot] = _dot_nt(q_ref[_rows(gi * gq, gq), :], k_ref[window(gi), :]) + bias_ref[var]

    def softmax_pv(gi, slot):
        s = s_ref[slot]
        e = jnp.exp2(s - jnp.max(s, axis=1, keepdims=True))
        l = jnp.sum(e, axis=1, keepdims=True)
        o = jnp.dot(e.astype(BF16), v_ref[window(gi), :], preferred_element_type=F32) / l
        o_ref[_rows(gi * gq, gq), :] = o.astype(o_ref.dtype)

    def pair(g0, last):
        scores(g0 + 1, 1)
        softmax_pv(g0, 0)
        if not last:
            scores(g0 + 2, 0)
        softmax_pv(g0 + 1, 1)

    scores(0, 0)

    def body(pi, carry):
        pair(2 * pi, False)
        return carry

    lax.fori_loop(0, n_groups // 2 - 1, body, 0)
    pair(n_groups - 2, True)


def _nat_attention(qkv3, b_off, nb, shared_out, bias_tbl):
    n_seq, seq, three_d = qkv3.shape
    d = three_d // 3
    n_heads = d // HEAD_DIM
    rows = seq // GRID_W
    assert rows % NAT_G == 0 and rows >= NAT_KROWS and rows >= 3 * NAT_G
    gq, gk = NAT_G * GRID_W, NAT_KROWS * GRID_W
    blk = 2 * (4 * seq * HEAD_DIM * 2 + 3 * gq * gk * 4) + 6 * gq * gk * 4

    def col_spec(col0):
        return pl.BlockSpec((None, seq, HEAD_DIM), lambda b, h: (b + b_off, 0, col0 + h))

    return _attention_call(
        functools.partial(_nat_kernel, rows=rows),
        grid=(nb, n_heads),
        in_specs=[col_spec(0), col_spec(n_heads), col_spec(2 * n_heads),
                  pl.BlockSpec((None, 3, gq, gk), lambda b, h: (h, 0, 0, 0))],
        args=[qkv3, qkv3, qkv3, bias_tbl],
        out_block=(None, seq, HEAD_DIM), out_index=lambda b, h: (0, h),
        b_off=b_off, shared_out=shared_out, out_view=(n_seq, seq, d),
        scratch=[pltpu.VMEM((2, gq, gk), F32)],
        semantics=("parallel", "parallel"), block_bytes=blk, name="nat_attention")


def _lambda_init(layer_idx):
    return 0.8 - 0.6 * math.exp(-0.3 * layer_idx)


def _seq_view(arr, seq):
    t, c = arr.shape
    return arr.reshape(t // seq, seq, c)


def _trunk(x_prompt, x_sample, layers, final_norm):
    bp, sp, d = x_prompt.shape
    bs, ss, _ = x_sample.shape
    tp, ts = bp * sp, bs * ss
    t = tp + ts
    assert tp % ss == 0 and t % sp == 0 and t % ss == 0
    groups = ((sp, 0, bp), (ss, tp // ss, bs))
    x_parts = [x_prompt.reshape(tp, d), x_sample.reshape(ts, d)]
    pos = jnp.concatenate([jnp.tile(jnp.arange(sp), bp), jnp.tile(jnp.arange(ss), bs)])

    def per_group(fn):
        out = None
        for seq, b_off, nb in groups:
            out = fn(seq, b_off, nb, out).reshape(t, d)
        return out

    xb, sumsq = _cast_stats(x_parts)
    for li, p in enumerate(layers):
        kind = p["kind"]
        w_qkv, g_mix = p["w_qkv"], p["norm_mix"]
        if kind == "nat":
            qkv = _project(xb, sumsq, w_qkv, g_mix, epilogue="qkv", n_q_cols=d, q_scale=QK_SCALE_LOG2)
            tbls = {seq: _nat_bias_table(p["rel_bias"], seq // GRID_W) for seq, _, _ in groups}
            o = per_group(lambda seq, b_off, nb, out: _nat_attention(
                _seq_view(qkv, seq), b_off, nb, out, tbls[seq]))
        elif kind == "diff":
            qkv = _project(xb, sumsq, w_qkv, g_mix, epilogue="qkv_rope", n_q_cols=d, n_k_cols=d,
                           q_scale=QK_SCALE_LOG2, rope=_partial_rope_tables(pos))
            lam_params = (p["lq1"], p["lk1"], p["lq2"], p["lk2"])
            o = per_group(lambda seq, b_off, nb, out: _diff_attention(
                _seq_view(qkv, seq), b_off, nb, out, lam_params, p["subln"], _lambda_init(li)))
        else:
            kv_dim = (w_qkv.shape[1] - d) // 2
            norm_w = jnp.stack([p["q_norm"], p["k_norm"]]).astype(F32)
            qkv = _project(xb, sumsq, w_qkv, g_mix, epilogue="qkv_rope", n_q_cols=d, n_k_cols=kv_dim,
                           q_scale=QK_SCALE_LOG2, rope=_axial_rope_tables(pos), qk_norm_w=norm_w)
            o = per_group(lambda seq, b_off, nb, out: _gqa_attention(
                _seq_view(qkv, seq), b_off, nb, out, d))
        x, xb, sumsq = _residual_matmul(o, p["w_o"], x_parts)
        u = _project(xb, sumsq, p["w_up"], p["norm_mlp"], epilogue="relu2")
        x, xb, sumsq = _residual_matmul(u, p["w_down"], [x], emit_stats=li + 1 < len(layers))
        x_parts = [x]

    y_prompt = _rmsnorm(x, final_norm, F32, row0=0, n_rows=tp)
    y_sample = _rmsnorm(x, final_norm, F32, row0=tp, n_rows=ts)
    return y_prompt.reshape(bp, sp, d), y_sample.reshape(bs, ss, d)


def kernel(x_prompt, x_sample, l0_norm_mix, l0_w_qkv, l0_rel_bias, l0_w_o, l0_norm_mlp, l0_w_up, l0_w_down, l1_norm_mix, l1_w_qkv, l1_lambda_q1, l1_lambda_k1, l1_lambda_q2, l1_lambda_k2, l1_subln, l1_w_o, l1_norm_mlp, l1_w_up, l1_w_down, l2_norm_mix, l2_w_qkv, l2_q_norm, l2_k_norm, l2_w_o, l2_norm_mlp, l2_w_up, l2_w_down, l3_norm_mix, l3_w_qkv, l3_rel_bias, l3_w_o, l3_norm_mlp, l3_w_up, l3_w_down, final_norm):
    layers = [
        {"kind": "nat", "norm_mix": l0_norm_mix, "w_qkv": l0_w_qkv, "rel_bias": l0_rel_bias, "w_o": l0_w_o,
         "norm_mlp": l0_norm_mlp, "w_up": l0_w_up, "w_down": l0_w_down},
        {"kind": "diff", "norm_mix": l1_norm_mix, "w_qkv": l1_w_qkv, "lq1": l1_lambda_q1, "lk1": l1_lambda_k1,
         "lq2": l1_lambda_q2, "lk2": l1_lambda_k2, "subln": l1_subln, "w_o": l1_w_o,
         "norm_mlp": l1_norm_mlp, "w_up": l1_w_up, "w_down": l1_w_down},
        {"kind": "gqa", "norm_mix": l2_norm_mix, "w_qkv": l2_w_qkv, "q_norm": l2_q_norm, "k_norm": l2_k_norm,
         "w_o": l2_w_o, "norm_mlp": l2_norm_mlp, "w_up": l2_w_up, "w_down": l2_w_down},
        {"kind": "nat", "norm_mix": l3_norm_mix, "w_qkv": l3_w_qkv, "rel_bias": l3_rel_bias, "w_o": l3_w_o,
         "norm_mlp": l3_norm_mlp, "w_up": l3_w_up, "w_down": l3_w_down},
    ]
    return _trunk(x_prompt, x_sample, layers, final_norm)
```

```python
import functools
import math

import numpy as np
import jax
import jax.numpy as jnp
from jax import lax
from jax.experimental import pallas as pl
from jax.experimental.pallas import tpu as pltpu

F32 = jnp.float32
BF16 = jnp.bfloat16

HEAD_DIM = 128
GRID_W = 64
NA_WIN_H = 8
NA_WIN_W = 16
ROPE_THETA = 500000.0
PARTIAL_ROT = HEAD_DIM // 4
AXIAL_THETA = 10000.0
AXIAL_DIM = HEAD_DIM // 2
NORM_EPS = 1e-6
SUBLN_EPS = 1e-5
NEG_BIG = -1e30
LOG2_E = math.log2(math.e)
QK_SCALE_LOG2 = HEAD_DIM ** -0.5 * LOG2_E

NAT_G = 4
NAT_KROWS = NAT_G + NA_WIN_H

V7X_VMEM_BYTES = 64 * 1024 * 1024
VMEM_CAP_BYTES = V7X_VMEM_BYTES - 2 * 1024 * 1024


def _params(semantics, block_bytes):
    limit = min(VMEM_CAP_BYTES, int(block_bytes * 1.2) + (8 << 20))
    return pltpu.CompilerParams(dimension_semantics=semantics, vmem_limit_bytes=limit)


def _divisor_block(n, target, align):
    best = None
    for b in range(align, min(n, target) + 1, align):
        if n % b == 0:
            best = b
    assert best is not None, (n, target, align)
    return best


def _dot_nt(a, b):
    return lax.dot_general(a, b, (((1,), (1,)), ((), ())), preferred_element_type=F32)


def _rmsnorm_kernel(x_ref, w_ref, o_ref, *, eps):
    x = x_ref[...]
    y = x * lax.rsqrt(jnp.mean(x * x, axis=-1, keepdims=True) + eps)
    o_ref[...] = (y * w_ref[...]).astype(o_ref.dtype)


def _rmsnorm(x, w, out_dtype, *, row0=0, n_rows=None, bt=256):
    t, d = x.shape
    n_rows = t if n_rows is None else n_rows
    bt = math.gcd(math.gcd(bt, n_rows), row0) if row0 else min(bt, n_rows)
    blk0 = row0 // bt
    blk = bt * d * (4 + jnp.dtype(out_dtype).itemsize) * 2
    return pl.pallas_call(
        functools.partial(_rmsnorm_kernel, eps=NORM_EPS),
        grid=(n_rows // bt,),
        in_specs=[pl.BlockSpec((bt, d), lambda i: (i + blk0, 0)),
                  pl.BlockSpec((1, d), lambda i: (0, 0))],
        out_specs=pl.BlockSpec((bt, d), lambda i: (i, 0)),
        out_shape=jax.ShapeDtypeStruct((n_rows, d), out_dtype),
        compiler_params=_params(("parallel",), blk),
        name="rmsnorm",
    )(x, w.reshape(1, d).astype(F32))


def _fold_lanes(v):
    acc = v[:, 0:HEAD_DIM]
    for g in range(1, v.shape[1] // HEAD_DIM):
        acc = acc + v[:, g * HEAD_DIM:(g + 1) * HEAD_DIM]
    return acc


def _cast_stats_kernel(x_ref, xb_ref, ss_ref):
    x = x_ref[...]
    xb_ref[...] = x.astype(BF16)
    ss_ref[...] = _fold_lanes(x * x)


def _cast_stats(x_parts, *, bt=256):
    d = x_parts[0].shape[1]
    t = sum(xp.shape[0] for xp in x_parts)
    bt = math.gcd(bt, *[xp.shape[0] for xp in x_parts])
    out = None
    blk0 = 0
    for xp in x_parts:
        in_specs = [pl.BlockSpec((bt, d), lambda i: (i, 0))]
        args = [xp]
        kernel_fn = _cast_stats_kernel
        aliases = {}
        if out is not None:
            def kernel_fn(x_ref, xb_prev, ss_prev, xb_ref, ss_ref):
                _cast_stats_kernel(x_ref, xb_ref, ss_ref)

            in_specs += [pl.BlockSpec(memory_space=pl.ANY)] * 2
            args += list(out)
            aliases = {1: 0, 2: 1}
        out = pl.pallas_call(
            kernel_fn,
            grid=(xp.shape[0] // bt,),
            in_specs=in_specs,
            out_specs=[pl.BlockSpec((bt, d), lambda i, b0=blk0: (i + b0, 0)),
                       pl.BlockSpec((bt, HEAD_DIM), lambda i, b0=blk0: (i + b0, 0))],
            out_shape=[jax.ShapeDtypeStruct((t, d), BF16), jax.ShapeDtypeStruct((t, HEAD_DIM), F32)],
            input_output_aliases=aliases,
            compiler_params=_params(("parallel",), bt * d * 6 * 2),
            name="cast_stats",
        )(*args)
        blk0 += xp.shape[0] // bt
    return tuple(out)


def _proj_kernel(*refs, d_model, epilogue, fold_weight, n_q_blocks, n_k_blocks, q_scale, rope_shift, qk_norm,
                 heads_per_block):
    refs = list(refs)
    a_ref, w_ref = refs[:2]
    del refs[:2]
    gain_ref = refs.pop(0) if fold_weight else None
    ss_ref = refs.pop(0)
    if epilogue == "qkv_rope":
        c_ref, sa_ref, sb_ref = refs[:3]
        del refs[:3]
        nw_ref = refs.pop(0) if qk_norm else None
    o_ref = refs.pop(0)
    if fold_weight:
        b = (gain_ref[...] * w_ref[...]).astype(BF16)
        refs.pop(0)[...] = b
    else:
        b = w_ref[...]
    rstd = lax.rsqrt(jnp.sum(ss_ref[...], axis=-1, keepdims=True) / d_model + NORM_EPS)
    if epilogue != "qkv_rope":
        y = jnp.dot(a_ref[...], b, preferred_element_type=F32) * rstd
        if epilogue == "relu2":
            o_ref[...] = jnp.square(jnp.maximum(y, 0.0)).astype(o_ref.dtype)
        else:
            sc = jnp.where(pl.program_id(1) < n_q_blocks, q_scale, 1.0).astype(F32)
            o_ref[...] = (y * sc).astype(o_ref.dtype)
        return
    j = pl.program_id(1)
    is_q = j < n_q_blocks
    is_qk = j < n_q_blocks + n_k_blocks
    sc = jnp.where(is_q, q_scale, 1.0).astype(F32)
    c = c_ref[...]
    sa = sa_ref[...]
    sb = sb_ref[...]
    if qk_norm:
        gain = jnp.where(is_q, nw_ref[0:1, :], jnp.where(is_qk, nw_ref[1:2, :], 1.0))
    y = jnp.dot(a_ref[...], b, preferred_element_type=F32) * rstd
    for hh in range(heads_per_block):
        lanes = slice(hh * HEAD_DIM, (hh + 1) * HEAD_DIM)
        x = y[:, lanes]
        if qk_norm:
            inv = lax.rsqrt(jnp.mean(x * x, axis=-1, keepdims=True) + NORM_EPS)
            x = x * jnp.where(is_qk, inv, 1.0) * gain
        x = x * c + pltpu.roll(x, HEAD_DIM - rope_shift, 1) * sa + pltpu.roll(x, rope_shift, 1) * sb
        o_ref[:, lanes] = (x * sc).astype(o_ref.dtype)


def _project(xb, sumsq, w, gain, *, epilogue, n_q_cols=0, n_k_cols=0, q_scale=1.0, rope=None, qk_norm_w=None,
             bm=1024, bn=1024, bn_fold=512):
    m, k = xb.shape
    _, n = w.shape
    bm = _divisor_block(m, bm, 8)
    if epilogue == "qkv_rope":
        c, sa, sb, shift = rope
        tables = [jnp.stack([c, jnp.ones_like(c)]), jnp.stack([sa, jnp.zeros_like(sa)]),
                  jnp.stack([sb, jnp.zeros_like(sb)])]
    else:
        shift, tables = 0, []

    def call(fold_weight, weight, row_blk0, n_row_blks, bn, shared_out):
        bn = _divisor_block(n, bn, HEAD_DIM)
        if n_q_cols:
            bn = math.gcd(math.gcd(bn, n_q_cols), n_k_cols) if n_k_cols else math.gcd(bn, n_q_cols)
        n_qk_blocks = (n_q_cols + n_k_cols) // bn
        in_specs = [pl.BlockSpec((bm, k), lambda i, j: (i + row_blk0, 0)),
                    pl.BlockSpec((k, bn), lambda i, j: (0, j))]
        args = [xb, weight]
        if fold_weight:
            in_specs.append(pl.BlockSpec((k, 1), lambda i, j: (0, 0)))
            args.append(gain.reshape(k, 1).astype(F32))
        in_specs.append(pl.BlockSpec((bm, HEAD_DIM), lambda i, j: (i + row_blk0, 0)))
        args.append(sumsq)
        if tables:
            tab_spec = pl.BlockSpec((None, bm, HEAD_DIM),
                                    lambda i, j: (jnp.where(j < n_qk_blocks, 0, 1), i + row_blk0, 0))
            in_specs += [tab_spec] * 3
            args += tables
            if qk_norm_w is not None:
                in_specs.append(pl.BlockSpec((2, HEAD_DIM), lambda i, j: (0, 0)))
                args.append(qk_norm_w)
        out_specs = [pl.BlockSpec((bm, bn), lambda i, j: (i + row_blk0, j))]
        out_shape = [jax.ShapeDtypeStruct((m, n), BF16)]
        w_bytes = 4 if fold_weight else 2
        blk = 2 * (bm * k * 2 + k * bn * w_bytes + bm * bn * 2 + 4 * bm * HEAD_DIM * 4) + 2 * bm * bn * 4
        aliases = {}
        if fold_weight:
            out_specs.append(pl.BlockSpec((k, bn), lambda i, j: (0, j)))
            out_shape.append(jax.ShapeDtypeStruct((k, n), BF16))
            blk += 3 * k * bn * 2 + k * HEAD_DIM * 4
        kernel_fn = functools.partial(
            _proj_kernel, d_model=k, epilogue=epilogue, fold_weight=fold_weight, n_q_blocks=n_q_cols // bn,
            n_k_blocks=n_k_cols // bn, q_scale=q_scale, rope_shift=shift, qk_norm=qk_norm_w is not None,
            heads_per_block=bn // HEAD_DIM)
        if shared_out is not None:
            n_in = len(args)
            inner = kernel_fn

            def kernel_fn(*refs):
                return inner(*refs[:n_in], *refs[n_in + 1:])

            in_specs.append(pl.BlockSpec(memory_space=pl.ANY))
            args.append(shared_out)
            aliases = {n_in: 0}
        return pl.pallas_call(
            kernel_fn,
            grid=(n_row_blks, n // bn),
            in_specs=in_specs,
            out_specs=out_specs,
            out_shape=out_shape,
            input_output_aliases=aliases,
            compiler_params=_params(("parallel", "arbitrary"), blk),
            name="proj_" + epilogue + ("_fold" if fold_weight else ""),
        )(*args)

    out, w_bf16 = call(True, w, 0, 1, bn_fold, None)
    if m // bm > 1:
        out = call(False, w_bf16, 1, m // bm - 1, bn, out)[0]
    return out


def _resid_kernel(*refs, nk, emit_stats, fold_weight):
    refs = list(refs)
    a_ref, w_ref, r_ref, x_ref = refs[:4]
    del refs[:4]
    if emit_stats:
        xb_ref, ss_ref = refs[:2]
        del refs[:2]
    if fold_weight:
        b = w_ref[...].astype(BF16)
        refs.pop(0)[...] = b
    else:
        b = w_ref[...]
    part = jnp.dot(a_ref[...], b, preferred_element_type=F32)
    j = pl.program_id(1)

    def finish(val):
        x_ref[...] = val
        if not emit_stats:
            return
        xb_ref[...] = val.astype(BF16)
        sq = _fold_lanes(val * val)

        @pl.when(j == 0)
        def _():
            ss_ref[...] = sq

        @pl.when(j > 0)
        def _():
            ss_ref[...] += sq

    if nk == 1:
        finish(r_ref[...] + part)
        return
    k = pl.program_id(2)

    @pl.when(k == 0)
    def _():
        x_ref[...] = r_ref[...] + part

    @pl.when(jnp.logical_and(k > 0, k < nk - 1))
    def _():
        x_ref[...] += part

    @pl.when(k == nk - 1)
    def _():
        finish(x_ref[...] + part)


def _residual_matmul(a, w, x_parts, *, emit_stats=True, bm=1024, bn=1024, bk_max=4096, bn_fold=512):
    m, k = a.shape
    _, n = w.shape
    bm = _divisor_block(math.gcd(m, *[xp.shape[0] for xp in x_parts]), bm, 8)
    if emit_stats and k > bk_max:
        x_new, _, _ = _residual_matmul(a, w, x_parts, emit_stats=False, bm=bm, bn=bn, bk_max=bk_max,
                                       bn_fold=bn_fold)
        return (x_new,) + tuple(_cast_stats([x_new]))

    def call(fold_weight, weight, r, row_blk0, r_blk0, n_row_blks, bn, shared):
        bn = _divisor_block(n, bn, HEAD_DIM)
        w_bytes = 4 if fold_weight else 2

        def footprint(bk):
            byt = 2 * (bm * bk * 2 + bk * bn * w_bytes + 2 * bm * bn * 4) + 2 * bm * bn * 4
            if emit_stats:
                byt += 2 * (bm * bn * 2 + bm * HEAD_DIM * 4)
            if fold_weight:
                byt += 3 * bk * bn * 2
            return byt

        bk = _divisor_block(k, bk_max, HEAD_DIM)
        while footprint(bk) > VMEM_CAP_BYTES and bk % (2 * HEAD_DIM) == 0:
            bk //= 2
        nk = k // bk
        in_specs = [pl.BlockSpec((bm, bk), lambda i, j, kk: (i + row_blk0, kk)),
                    pl.BlockSpec((bk, bn), lambda i, j, kk: (kk, j)),
                    pl.BlockSpec((bm, bn), lambda i, j, kk: (i + r_blk0, j))]
        args = [a, weight, r]
        tile = pl.BlockSpec((bm, bn), lambda i, j, kk: (i + row_blk0, j))
        out_specs = [tile]
        out_shape = [jax.ShapeDtypeStruct((m, n), F32)]
        if emit_stats:
            out_specs += [tile, pl.BlockSpec((bm, HEAD_DIM), lambda i, j, kk: (i + row_blk0, 0))]
            out_shape += [jax.ShapeDtypeStruct((m, n), BF16), jax.ShapeDtypeStruct((m, HEAD_DIM), F32)]
        n_shared = len(out_specs)
        if fold_weight:
            out_specs.append(pl.BlockSpec((bk, bn), lambda i, j, kk: (kk, j)))
            out_shape.append(jax.ShapeDtypeStruct((k, n), BF16))
        kernel_fn = functools.partial(_resid_kernel, nk=nk, emit_stats=emit_stats, fold_weight=fold_weight)
        aliases = {}
        if shared is not None:
            n_in = len(args)
            inner = kernel_fn

            def kernel_fn(*refs):
                return inner(*refs[:n_in], *refs[n_in + n_shared:])

            in_specs += [pl.BlockSpec(memory_space=pl.ANY)] * n_shared
            args += list(shared)
            aliases = {n_in + t: t for t in range(n_shared)}
        out = pl.pallas_call(
            kernel_fn,
            grid=(n_row_blks, n // bn, nk),
            in_specs=in_specs,
            out_specs=out_specs,
            out_shape=out_shape,
            input_output_aliases=aliases,
            compiler_params=_params(("parallel", "arbitrary", "arbitrary"), footprint(bk)),
            name="resid_matmul" + ("_fold" if fold_weight else ""),
        )(*args)
        return tuple(out[:n_shared]), (out[n_shared] if fold_weight else weight)

    shared, w_bf16 = call(True, w, x_parts[0], 0, 0, 1, bn_fold, None)
    row_blk0 = 0
    for part_idx, xp in enumerate(x_parts):
        n_blks = xp.shape[0] // bm
        skip = 1 if part_idx == 0 else 0
        if n_blks > skip:
            shared, _ = call(False, w_bf16, xp, row_blk0 + skip, skip, n_blks - skip, bn, shared)
        row_blk0 += n_blks
    return shared if emit_stats else (shared[0], None, None)


def _rope_angles(pos, dim, theta):
    inv = theta ** (-jnp.arange(0, dim, 2, dtype=F32) / dim)
    ang = pos.astype(F32)[:, None] * inv[None, :]
    return jnp.cos(ang), jnp.sin(ang)


def _partial_rope_tables(pos):
    cos, sin = _rope_angles(pos, PARTIAL_ROT, ROPE_THETA)
    t, h = cos.shape
    rest = HEAD_DIM - 2 * h
    c = jnp.concatenate([cos, cos, jnp.ones((t, rest), F32)], axis=1)
    sa = jnp.concatenate([-sin, jnp.zeros((t, HEAD_DIM - h), F32)], axis=1)
    sb = jnp.concatenate([jnp.zeros((t, h), F32), sin, jnp.zeros((t, rest), F32)], axis=1)
    return c, sa, sb, h


def _axial_rope_tables(pos):
    rcos, rsin = _rope_angles(pos // GRID_W, AXIAL_DIM, AXIAL_THETA)
    ccos, csin = _rope_angles(pos % GRID_W, AXIAL_DIM, AXIAL_THETA)
    z = jnp.zeros_like(rsin)
    c = jnp.concatenate([rcos, rcos, ccos, ccos], axis=1)
    sa = jnp.concatenate([-rsin, z, -csin, z], axis=1)
    sb = jnp.concatenate([z, rsin, z, csin], axis=1)
    return c, sa, sb, rcos.shape[1]


def _rows(start, size):
    if isinstance(start, int):
        return pl.ds(start, size)
    return pl.ds(pl.multiple_of(start, size), size)


def _attention_call(kernel_fn, *, grid, in_specs, args, out_block, out_index, b_off, shared_out, out_view,
                    scratch=(), semantics, block_bytes, name):
    n_in = len(args)
    aliases = {}
    if shared_out is not None:
        inner = kernel_fn

        def kernel_fn(*refs):
            return inner(*refs[:n_in], *refs[n_in + 1:])

        in_specs = list(in_specs) + [pl.BlockSpec(memory_space=pl.ANY)]
        args = list(args) + [shared_out.reshape(out_view)]
        aliases = {n_in: 0}

    def out_map(b, *rest):
        return (b + b_off,) + tuple(out_index(b, *rest))

    return pl.pallas_call(
        kernel_fn,
        grid=grid,
        in_specs=in_specs,
        out_specs=pl.BlockSpec(out_block, out_map),
        out_shape=jax.ShapeDtypeStruct(out_view, BF16),
        scratch_shapes=list(scratch),
        input_output_aliases=aliases,
        compiler_params=_params(semantics, block_bytes),
        name=name,
    )(*args)


def _transpose_bf16(x):
    return x.astype(F32).T.astype(BF16)


def _build_vt(v_ref, vt_ref, n_chunks, bk):
    @pl.when(pl.program_id(2) == 0)
    def _():
        def body(c, carry):
            vt_ref[c] = _transpose_bf16(v_ref[_rows(c * bk, bk), :])
            return carry

        lax.fori_loop(0, n_chunks, body, 0)


def _attend_t(score_t, vt_ref, s_ref, m_ref, l_ref, acc_ref, n_chunks):
    assert n_chunks % 2 == 0

    def scores(c, slot):
        s = score_t(c)
        s_ref[slot] = s
        return jnp.max(s, axis=0, keepdims=True)

    def softmax_pv(c, slot, chunk_max):
        m_old = m_ref[...]
        m_new = jnp.maximum(m_old, chunk_max)
        alpha = jnp.exp2(m_old - m_new)
        p = jnp.exp2(s_ref[slot] - m_new)
        l_ref[...] = alpha * l_ref[...] + jnp.sum(p, axis=0, keepdims=True)
        acc_ref[...] = alpha * acc_ref[...] + jnp.dot(vt_ref[c], p.astype(BF16), preferred_element_type=F32)
        m_ref[...] = m_new

    def pair(c0, max0, last):
        max1 = scores(c0 + 1, 1)
        softmax_pv(c0, 0, max0)
        next_max = None if last else scores(c0 + 2, 0)
        softmax_pv(c0 + 1, 1, max1)
        return next_max

    m_ref[...] = jnp.full(m_ref.shape, NEG_BIG, F32)
    l_ref[...] = jnp.zeros(l_ref.shape, F32)
    acc_ref[...] = jnp.zeros(acc_ref.shape, F32)
    max0 = scores(0, 0)
    max0 = lax.fori_loop(0, n_chunks // 2 - 1, lambda ci, mx: pair(2 * ci, mx, False), max0)
    pair(n_chunks - 2, max0, True)
    return acc_ref[...] / l_ref[...]


def _diff_kernel(q_ref, k_ref, v_ref, lq1_ref, lk1_ref, lq2_ref, lk2_ref, sub_ref, o_ref,
                 vt_ref, qt_ref, *flash_refs, n_chunks, bq, bk, lambda_init):
    _build_vt(v_ref, vt_ref, n_chunks, bk)
    for h in range(2):
        qt_ref[h] = _transpose_bf16(q_ref[:, h * HEAD_DIM:(h + 1) * HEAD_DIM])

    def score_t(c):
        rows = _rows(c * bk, bk)
        s1 = jnp.dot(k_ref[rows, 0:HEAD_DIM], qt_ref[0], preferred_element_type=F32)
        s2 = jnp.dot(k_ref[rows, HEAD_DIM:2 * HEAD_DIM], qt_ref[1], preferred_element_type=F32)
        return jnp.concatenate([s1, s2], axis=1)

    o = _attend_t(score_t, vt_ref, *flash_refs, n_chunks).T
    lam = (jnp.exp(jnp.sum(lq1_ref[...] * lk1_ref[...], axis=-1, keepdims=True))
           - jnp.exp(jnp.sum(lq2_ref[...] * lk2_ref[...], axis=-1, keepdims=True)) + lambda_init)
    d = o[:bq] - lam * o[bq:]
    y = d * lax.rsqrt(jnp.mean(d * d, axis=-1, keepdims=True) + SUBLN_EPS) * sub_ref[...]
    o_ref[...] = (y * (1.0 - lambda_init)).astype(o_ref.dtype)


def _flash_scratch(n_chunks, dv, bk, m_rows, qt_shape):
    return [pltpu.VMEM((n_chunks, dv, bk), BF16), pltpu.VMEM(qt_shape, BF16),
            pltpu.VMEM((2, bk, m_rows), F32), pltpu.VMEM((1, m_rows), F32), pltpu.VMEM((1, m_rows), F32),
            pltpu.VMEM((dv, m_rows), F32)]


def _diff_attention(qkv3, b_off, nb, shared_out, lam_params, subln, lambda_init, *, bq=512, bk=1024):
    n_seq, seq, three_d = qkv3.shape
    d = three_d // 3
    n_pairs = d // (2 * HEAD_DIM)
    w = 2 * HEAD_DIM
    bq, bk = min(bq, seq), min(bk, seq // 2)
    n_chunks = seq // bk
    vec = pl.BlockSpec((1, HEAD_DIM), lambda b, t, i: (0, 0))
    blk = 2 * (2 * seq * w * 2 + 2 * bq * w * 2) + seq * w * 2 + 8 * bk * 2 * bq * 4
    return _attention_call(
        functools.partial(_diff_kernel, n_chunks=n_chunks, bq=bq, bk=bk, lambda_init=lambda_init),
        grid=(nb, n_pairs, seq // bq),
        in_specs=[pl.BlockSpec((None, bq, w), lambda b, t, i: (b + b_off, i, t)),
                  pl.BlockSpec((None, seq, w), lambda b, t, i: (b + b_off, 0, d // w + t)),
                  pl.BlockSpec((None, seq, w), lambda b, t, i: (b + b_off, 0, 2 * d // w + t)),
                  vec, vec, vec, vec,
                  pl.BlockSpec((1, w), lambda b, t, i: (0, 0))],
        args=[qkv3, qkv3, qkv3, *[p.reshape(1, HEAD_DIM).astype(F32) for p in lam_params],
              subln.reshape(1, w).astype(F32)],
        out_block=(None, bq, w), out_index=lambda b, t, i: (i, t),
        b_off=b_off, shared_out=shared_out, out_view=(n_seq, seq, d),
        scratch=_flash_scratch(n_chunks, w, bk, 2 * bq, (2, HEAD_DIM, bq)),
        semantics=("parallel", "parallel", "arbitrary"), block_bytes=blk, name="diff_attention")


def _gqa_kernel(q_ref, k_ref, v_ref, o_ref, vt_ref, qt_ref, *flash_refs,
                n_chunks, bq, bk, group):
    _build_vt(v_ref, vt_ref, n_chunks, bk)
    for g in range(group):
        qt_ref[:, g * bq:(g + 1) * bq] = _transpose_bf16(q_ref[:, g * HEAD_DIM:(g + 1) * HEAD_DIM])

    def score_t(c):
        return jnp.dot(k_ref[_rows(c * bk, bk), :], qt_ref[...], preferred_element_type=F32)

    ot = _attend_t(score_t, vt_ref, *flash_refs, n_chunks)
    for g in range(group):
        o_ref[:, g * HEAD_DIM:(g + 1) * HEAD_DIM] = ot[:, g * bq:(g + 1) * bq].T.astype(o_ref.dtype)


def _gqa_attention(qkv3, b_off, nb, shared_out, d, *, bq=256, bk=1024):
    n_seq, seq, n_cols = qkv3.shape
    kv_dim = (n_cols - d) // 2
    n_kv = kv_dim // HEAD_DIM
    group = d // kv_dim
    w = group * HEAD_DIM
    bq, bk = min(bq, seq), min(bk, seq // 2)
    n_chunks = seq // bk
    blk = 2 * (2 * seq * HEAD_DIM * 2 + 2 * bq * w * 2) + seq * HEAD_DIM * 2 + 8 * bk * group * bq * 4
    return _attention_call(
        functools.partial(_gqa_kernel, n_chunks=n_chunks, bq=bq, bk=bk, group=group),
        grid=(nb, n_kv, seq // bq),
        in_specs=[pl.BlockSpec((None, bq, w), lambda b, n, i: (b + b_off, i, n)),
                  pl.BlockSpec((None, seq, HEAD_DIM), lambda b, n, i: (b + b_off, 0, d // HEAD_DIM + n)),
                  pl.BlockSpec((None, seq, HEAD_DIM),
                               lambda b, n, i: (b + b_off, 0, (d + kv_dim) // HEAD_DIM + n))],
        args=[qkv3, qkv3, qkv3],
        out_block=(None, bq, w), out_index=lambda b, n, i: (i, n),
        b_off=b_off, shared_out=shared_out, out_view=(n_seq, seq, d),
        scratch=_flash_scratch(n_chunks, HEAD_DIM, bk, group * bq, (HEAD_DIM, group * bq)),
        semantics=("parallel", "parallel", "arbitrary"), block_bytes=blk, name="gqa_attention")


def _nat_bias_table(rel_bias, rows):
    n_heads = rel_bias.shape[0]
    kh = min(NA_WIN_H, rows)
    n_dr = 2 * NA_WIN_H - 1
    col = np.arange(GRID_W)
    col_start = np.clip(col - NA_WIN_W // 2, 0, GRID_W - NA_WIN_W)
    col_ok = (col[None, :] >= col_start[:, None]) & (col[None, :] < col_start[:, None] + NA_WIN_W)
    dc = np.clip(col[None, :] - col[:, None], -(NA_WIN_W - 1), NA_WIN_W - 1) + NA_WIN_W - 1
    planes = jnp.take(rel_bias.astype(F32) * LOG2_E, jnp.asarray(dc), axis=2)
    planes = jnp.where(jnp.asarray(col_ok)[None, None], planes, NEG_BIG)
    pad = jnp.full((n_heads, GRID_W, NAT_G * GRID_W), NEG_BIG, F32)
    wide = jnp.concatenate([pad, planes.transpose(0, 2, 1, 3).reshape(n_heads, GRID_W, n_dr * GRID_W), pad],
                           axis=2)
    blocks = []
    for r_base in (0, NAT_G, rows - NAT_G):
        a = int(np.clip(r_base - NA_WIN_H // 2, 0, rows - NAT_KROWS))
        for g in range(NAT_G):
            r = r_base + g
            r0 = int(np.clip(r - kh // 2, 0, rows - kh))
            dr0 = a - r + NA_WIN_H - 1
            assert -NAT_G <= dr0 and dr0 + NAT_KROWS <= n_dr + NAT_G
            row_ok = np.array([r0 <= a + j < r0 + kh for j in range(NAT_KROWS)])
            start = (dr0 + NAT_G) * GRID_W
            window = wide[:, :, start:start + NAT_KROWS * GRID_W]
            blocks.append(jnp.where(jnp.asarray(np.repeat(row_ok, GRID_W))[None, None], window, NEG_BIG))
    return jnp.stack(blocks, axis=1).reshape(n_heads, 3, NAT_G * GRID_W, NAT_KROWS * GRID_W)


def _nat_kernel(q_ref, k_ref, v_ref, bias_ref, o_ref, s_ref, p_ref, l_ref, *, rows):
    n_groups = rows // NAT_G
    gq = NAT_G * GRID_W
    gk = NAT_KROWS * GRID_W
    assert n_groups % 2 == 0 and n_groups >= 4

    def window(gi):
        a = jnp.clip(gi * NAT_G - NA_WIN_H // 2, 0, rows - NAT_KROWS)
        return pl.ds(pl.multiple_of(a * GRID_W, GRID_W), gk)

    def scores(gi, slot):
        var = jnp.where(gi == 0, 0, jnp.where(gi == n_groups - 1, 2, 1))
        s_ref[slot] = _dot_nt(q_ref[_rows(gi * gq, gq), :], k_ref[window(gi), :]) + bias_ref[var]

    def softmax(slot):
        s = s_ref[slot]
        e = jnp.exp2(s - jnp.max(s, axis=1, keepdims=True))
        l_ref[slot] = jnp.sum(e, axis=1, keepdims=True)
        p_ref[slot] = e.astype(BF16)

    def pv(gi, slot):
        o = jnp.dot(p_ref[slot], v_ref[window(gi), :], preferred_element_type=F32) / l_ref[slot]
        o_ref[_rows(gi * gq, gq), :] = o.astype(o_ref.dtype)

    def step(gi, parity):
        pv(gi - 1, 1 - parity)
        scores(gi + 1, 1 - parity)
        softmax(parity)

    scores(0, 0)
    scores(1, 1)
    softmax(0)

    def body(t, carry):
        step(2 * t + 1, 1)
        step(2 * t + 2, 0)
        return carry

    lax.fori_loop(0, n_groups // 2 - 1, body, 0)
    pv(n_groups - 2, 0)
    softmax(1)
    pv(n_groups - 1, 1)


def _nat_attention(qkv3, b_off, nb, shared_out, bias_tbl):
    n_seq, seq, three_d = qkv3.shape
    d = three_d // 3
    n_heads = d // HEAD_DIM
    rows = seq // GRID_W
    assert rows % NAT_G == 0 and rows >= NAT_KROWS and rows >= 3 * NAT_G
    gq, gk = NAT_G * GRID_W, NAT_KROWS * GRID_W
    blk = 2 * (4 * seq * HEAD_DIM * 2 + 3 * gq * gk * 4) + 6 * gq * gk * 4

    def col_spec(col0):
        return pl.BlockSpec((None, seq, HEAD_DIM), lambda b, h: (b + b_off, 0, col0 + h))

    return _attention_call(
        functools.partial(_nat_kernel, rows=rows),
        grid=(nb, n_heads),
        in_specs=[col_spec(0), col_spec(n_heads), col_spec(2 * n_heads),
                  pl.BlockSpec((None, 3, gq, gk), lambda b, h: (h, 0, 0, 0))],
        args=[qkv3, qkv3, qkv3, bias_tbl],
        out_block=(None, seq, HEAD_DIM), out_index=lambda b, h: (0, h),
        b_off=b_off, shared_out=shared_out, out_view=(n_seq, seq, d),
        scratch=[pltpu.VMEM((2, gq, gk), F32), pltpu.VMEM((2, gq, gk), BF16), pltpu.VMEM((2, gq, 1), F32)],
        semantics=("parallel", "parallel"), block_bytes=blk, name="nat_attention")


def _lambda_init(layer_idx):
    return 0.8 - 0.6 * math.exp(-0.3 * layer_idx)


def _seq_view(arr, seq):
    t, c = arr.shape
    return arr.reshape(t // seq, seq, c)


def _trunk(x_prompt, x_sample, layers, final_norm):
    bp, sp, d = x_prompt.shape
    bs, ss, _ = x_sample.shape
    tp, ts = bp * sp, bs * ss
    t = tp + ts
    assert tp % ss == 0 and t % sp == 0 and t % ss == 0
    groups = ((sp, 0, bp), (ss, tp // ss, bs))
    x_parts = [x_prompt.reshape(tp, d), x_sample.reshape(ts, d)]
    pos = jnp.concatenate([jnp.tile(jnp.arange(sp), bp), jnp.tile(jnp.arange(ss), bs)])

    def per_group(fn):
        out = None
        for seq, b_off, nb in groups:
            out = fn(seq, b_off, nb, out).reshape(t, d)
        return out

    xb, sumsq = _cast_stats(x_parts)
    for li, p in enumerate(layers):
        kind = p["kind"]
        w_qkv, g_mix = p["w_qkv"], p["norm_mix"]
        if kind == "nat":
            qkv = _project(xb, sumsq, w_qkv, g_mix, epilogue="qkv", n_q_cols=d, q_scale=QK_SCALE_LOG2)
            tbls = {seq: _nat_bias_table(p["rel_bias"], seq // GRID_W) for seq, _, _ in groups}
            o = per_group(lambda seq, b_off, nb, out: _nat_attention(
                _seq_view(qkv, seq), b_off, nb, out, tbls[seq]))
        elif kind == "diff":
            qkv = _project(xb, sumsq, w_qkv, g_mix, epilogue="qkv_rope", n_q_cols=d, n_k_cols=d,
                           q_scale=QK_SCALE_LOG2, rope=_partial_rope_tables(pos))
            lam_params = (p["lq1"], p["lk1"], p["lq2"], p["lk2"])
            o = per_group(lambda seq, b_off, nb, out: _diff_attention(
                _seq_view(qkv, seq), b_off, nb, out, lam_params, p["subln"], _lambda_init(li)))
        else:
            kv_dim = (w_qkv.shape[1] - d) // 2
            norm_w = jnp.stack([p["q_norm"], p["k_norm"]]).astype(F32)
            qkv = _project(xb, sumsq, w_qkv, g_mix, epilogue="qkv_rope", n_q_cols=d, n_k_cols=kv_dim,
                           q_scale=QK_SCALE_LOG2, rope=_axial_rope_tables(pos), qk_norm_w=norm_w)
            o = per_group(lambda seq, b_off, nb, out: _gqa_attention(
                _seq_view(qkv, seq), b_off, nb, out, d))
        x, xb, sumsq = _residual_matmul(o, p["w_o"], x_parts)
        u = _project(xb, sumsq, p["w_up"], p["norm_mlp"], epilogue="relu2")
        x, xb, sumsq = _residual_matmul(u, p["w_down"], [x], emit_stats=li + 1 < len(layers))
        x_parts = [x]

    y_prompt = _rmsnorm(x, final_norm, F32, row0=0, n_rows=tp)
    y_sample = _rmsnorm(x, final_norm, F32, row0=tp, n_rows=ts)
    return y_prompt.reshape(bp, sp, d), y_sample.reshape(bs, ss, d)


def kernel(x_prompt, x_sample, l0_norm_mix, l0_w_qkv, l0_rel_bias, l0_w_o, l0_norm_mlp, l0_w_up, l0_w_down, l1_norm_mix, l1_w_qkv, l1_lambda_q1, l1_lambda_k1, l1_lambda_q2, l1_lambda_k2, l1_subln, l1_w_o, l1_norm_mlp, l1_w_up, l1_w_down, l2_norm_mix, l2_w_qkv, l2_q_norm, l2_k_norm, l2_w_o, l2_norm_mlp, l2_w_up, l2_w_down, l3_norm_mix, l3_w_qkv, l3_rel_bias, l3_w_o, l3_norm_mlp, l3_w_up, l3_w_down, final_norm):
    layers = [
        {"kind": "nat", "norm_mix": l0_norm_mix, "w_qkv": l0_w_qkv, "rel_bias": l0_rel_bias, "w_o": l0_w_o,
         "norm_mlp": l0_norm_mlp, "w_up": l0_w_up, "w_down": l0_w_down},
        {"kind": "diff", "norm_mix": l1_norm_mix, "w_qkv": l1_w_qkv, "lq1": l1_lambda_q1, "lk1": l1_lambda_k1,
         "lq2": l1_lambda_q2, "lk2": l1_lambda_k2, "subln": l1_subln, "w_o": l1_w_o,
         "norm_mlp": l1_norm_mlp, "w_up": l1_w_up, "w_down": l1_w_down},
        {"kind": "gqa", "norm_mix": l2_norm_mix, "w_qkv": l2_w_qkv, "q_norm": l2_q_norm, "k_norm": l2_k_norm,
         "w_o": l2_w_o, "norm_mlp": l2_norm_mlp, "w_up": l2_w_up, "w_down": l2_w_down},
        {"kind": "nat", "norm_mix": l3_norm_mix, "w_qkv": l3_w_qkv, "rel_bias": l3_rel_bias, "w_o": l3_w_o,
         "norm_mlp": l3_norm_mlp, "w_up": l3_w_up, "w_down": l3_w_down},
    ]
    return _trunk(x_prompt, x_sample, layers, final_norm)
```

```python
import functools
import math

import numpy as np
import jax
import jax.numpy as jnp
from jax import lax
from jax.experimental import pallas as pl
from jax.experimental.pallas import tpu as pltpu

F32 = jnp.float32
BF16 = jnp.bfloat16

HEAD_DIM = 128
GRID_W = 64
NA_WIN_H = 8
NA_WIN_W = 16
ROPE_THETA = 500000.0
PARTIAL_ROT = HEAD_DIM // 4
AXIAL_THETA = 10000.0
AXIAL_DIM = HEAD_DIM // 2
NORM_EPS = 1e-6
SUBLN_EPS = 1e-5
NEG_BIG = -1e30
LOG2_E = math.log2(math.e)
QK_SCALE_LOG2 = HEAD_DIM ** -0.5 * LOG2_E

NAT_G = 4
NAT_KROWS = NAT_G + NA_WIN_H

V7X_VMEM_BYTES = 64 * 1024 * 1024
VMEM_CAP_BYTES = V7X_VMEM_BYTES - 2 * 1024 * 1024


def _params(semantics, block_bytes):
    limit = min(VMEM_CAP_BYTES, int(block_bytes * 1.2) + (8 << 20))
    return pltpu.CompilerParams(dimension_semantics=semantics, vmem_limit_bytes=limit)


def _divisor_block(n, target, align):
    best = None
    for b in range(align, min(n, target) + 1, align):
        if n % b == 0:
            best = b
    assert best is not None, (n, target, align)
    return best


def _dot_nt(a, b):
    return lax.dot_general(a, b, (((1,), (1,)), ((), ())), preferred_element_type=F32)


def _rmsnorm_kernel(x_ref, w_ref, o_ref, *, eps):
    x = x_ref[...]
    y = x * lax.rsqrt(jnp.mean(x * x, axis=-1, keepdims=True) + eps)
    o_ref[...] = (y * w_ref[...]).astype(o_ref.dtype)


def _rmsnorm(x, w, out_dtype, *, row0=0, n_rows=None, bt=256):
    t, d = x.shape
    n_rows = t if n_rows is None else n_rows
    bt = math.gcd(math.gcd(bt, n_rows), row0) if row0 else min(bt, n_rows)
    blk0 = row0 // bt
    blk = bt * d * (4 + jnp.dtype(out_dtype).itemsize) * 2
    return pl.pallas_call(
        functools.partial(_rmsnorm_kernel, eps=NORM_EPS),
        grid=(n_rows // bt,),
        in_specs=[pl.BlockSpec((bt, d), lambda i: (i + blk0, 0)),
                  pl.BlockSpec((1, d), lambda i: (0, 0))],
        out_specs=pl.BlockSpec((bt, d), lambda i: (i, 0)),
        out_shape=jax.ShapeDtypeStruct((n_rows, d), out_dtype),
        compiler_params=_params(("parallel",), blk),
        name="rmsnorm",
    )(x, w.reshape(1, d).astype(F32))


def _fold_lanes(v):
    acc = v[:, 0:HEAD_DIM]
    for g in range(1, v.shape[1] // HEAD_DIM):
        acc = acc + v[:, g * HEAD_DIM:(g + 1) * HEAD_DIM]
    return acc


def _cast_stats_kernel(x_ref, xb_ref, ss_ref):
    x = x_ref[...]
    xb_ref[...] = x.astype(BF16)
    ss_ref[...] = _fold_lanes(x * x)


def _cast_stats(x_parts, *, bt=256):
    d = x_parts[0].shape[1]
    t = sum(xp.shape[0] for xp in x_parts)
    bt = math.gcd(bt, *[xp.shape[0] for xp in x_parts])
    out = None
    blk0 = 0
    for xp in x_parts:
        in_specs = [pl.BlockSpec((bt, d), lambda i: (i, 0))]
        args = [xp]
        kernel_fn = _cast_stats_kernel
        aliases = {}
        if out is not None:
            def kernel_fn(x_ref, xb_prev, ss_prev, xb_ref, ss_ref):
                _cast_stats_kernel(x_ref, xb_ref, ss_ref)

            in_specs += [pl.BlockSpec(memory_space=pl.ANY)] * 2
            args += list(out)
            aliases = {1: 0, 2: 1}
        out = pl.pallas_call(
            kernel_fn,
            grid=(xp.shape[0] // bt,),
            in_specs=in_specs,
            out_specs=[pl.BlockSpec((bt, d), lambda i, b0=blk0: (i + b0, 0)),
                       pl.BlockSpec((bt, HEAD_DIM), lambda i, b0=blk0: (i + b0, 0))],
            out_shape=[jax.ShapeDtypeStruct((t, d), BF16), jax.ShapeDtypeStruct((t, HEAD_DIM), F32)],
            input_output_aliases=aliases,
            compiler_params=_params(("parallel",), bt * d * 6 * 2),
            name="cast_stats",
        )(*args)
        blk0 += xp.shape[0] // bt
    return tuple(out)


def _proj_kernel(*refs, d_model, epilogue, fold_weight, n_q_blocks, n_k_blocks, q_scale, rope_shift, qk_norm,
                 heads_per_block):
    refs = list(refs)
    a_ref, w_ref = refs[:2]
    del refs[:2]
    gain_ref = refs.pop(0) if fold_weight else None
    ss_ref = refs.pop(0)
    if epilogue == "qkv_rope":
        c_ref, sa_ref, sb_ref = refs[:3]
        del refs[:3]
        nw_ref = refs.pop(0) if qk_norm else None
    o_ref = refs.pop(0)
    if fold_weight:
        b = (gain_ref[...] * w_ref[...]).astype(BF16)
        refs.pop(0)[...] = b
    else:
        b = w_ref[...]
    rstd = lax.rsqrt(jnp.sum(ss_ref[...], axis=-1, keepdims=True) / d_model + NORM_EPS)
    if epilogue != "qkv_rope":
        y = jnp.dot(a_ref[...], b, preferred_element_type=F32) * rstd
        if epilogue == "relu2":
            o_ref[...] = jnp.square(jnp.maximum(y, 0.0)).astype(o_ref.dtype)
        else:
            sc = jnp.where(pl.program_id(1) < n_q_blocks, q_scale, 1.0).astype(F32)
            o_ref[...] = (y * sc).astype(o_ref.dtype)
        return
    j = pl.program_id(1)
    is_q = j < n_q_blocks
    y = jnp.dot(a_ref[...], b, preferred_element_type=F32) * rstd

    @pl.when(j >= n_q_blocks + n_k_blocks)
    def _():
        o_ref[...] = y.astype(o_ref.dtype)

    @pl.when(j < n_q_blocks + n_k_blocks)
    def _():
        sc = jnp.where(is_q, q_scale, 1.0).astype(F32)
        c = c_ref[...]
        sa = sa_ref[...]
        sb = sb_ref[...]
        if qk_norm:
            gain = jnp.where(is_q, nw_ref[0:1, :], nw_ref[1:2, :])
        for hh in range(heads_per_block):
            lanes = slice(hh * HEAD_DIM, (hh + 1) * HEAD_DIM)
            x = y[:, lanes]
            if qk_norm:
                x = x * lax.rsqrt(jnp.mean(x * x, axis=-1, keepdims=True) + NORM_EPS) * gain
            x = x * c + pltpu.roll(x, HEAD_DIM - rope_shift, 1) * sa + pltpu.roll(x, rope_shift, 1) * sb
            o_ref[:, lanes] = (x * sc).astype(o_ref.dtype)


def _project(xb, sumsq, w, gain, *, epilogue, n_q_cols=0, n_k_cols=0, q_scale=1.0, rope=None, qk_norm_w=None,
             bm=1024, bn=1024, bn_fold=512):
    m, k = xb.shape
    _, n = w.shape
    bm = _divisor_block(m, bm, 8)
    if epilogue == "qkv_rope":
        *tables, shift = rope
    else:
        shift, tables = 0, []

    def call(fold_weight, weight, row_blk0, n_row_blks, bn, shared_out):
        bn = _divisor_block(n, bn, HEAD_DIM)
        if n_q_cols:
            bn = math.gcd(math.gcd(bn, n_q_cols), n_k_cols) if n_k_cols else math.gcd(bn, n_q_cols)
        in_specs = [pl.BlockSpec((bm, k), lambda i, j: (i + row_blk0, 0)),
                    pl.BlockSpec((k, bn), lambda i, j: (0, j))]
        args = [xb, weight]
        if fold_weight:
            in_specs.append(pl.BlockSpec((k, 1), lambda i, j: (0, 0)))
            args.append(gain.reshape(k, 1).astype(F32))
        in_specs.append(pl.BlockSpec((bm, HEAD_DIM), lambda i, j: (i + row_blk0, 0)))
        args.append(sumsq)
        if tables:
            in_specs += [pl.BlockSpec((bm, HEAD_DIM), lambda i, j: (i + row_blk0, 0))] * 3
            args += tables
            if qk_norm_w is not None:
                in_specs.append(pl.BlockSpec((2, HEAD_DIM), lambda i, j: (0, 0)))
                args.append(qk_norm_w)
        out_specs = [pl.BlockSpec((bm, bn), lambda i, j: (i + row_blk0, j))]
        out_shape = [jax.ShapeDtypeStruct((m, n), BF16)]
        w_bytes = 4 if fold_weight else 2
        blk = 2 * (bm * k * 2 + k * bn * w_bytes + bm * bn * 2 + 4 * bm * HEAD_DIM * 4) + 2 * bm * bn * 4
        aliases = {}
        if fold_weight:
            out_specs.append(pl.BlockSpec((k, bn), lambda i, j: (0, j)))
            out_shape.append(jax.ShapeDtypeStruct((k, n), BF16))
            blk += 3 * k * bn * 2 + k * HEAD_DIM * 4
        kernel_fn = functools.partial(
            _proj_kernel, d_model=k, epilogue=epilogue, fold_weight=fold_weight, n_q_blocks=n_q_cols // bn,
            n_k_blocks=n_k_cols // bn, q_scale=q_scale, rope_shift=shift, qk_norm=qk_norm_w is not None,
            heads_per_block=bn // HEAD_DIM)
        if shared_out is not None:
            n_in = len(args)
            inner = kernel_fn

            def kernel_fn(*refs):
                return inner(*refs[:n_in], *refs[n_in + 1:])

            in_specs.append(pl.BlockSpec(memory_space=pl.ANY))
            args.append(shared_out)
            aliases = {n_in: 0}
        return pl.pallas_call(
            kernel_fn,
            grid=(n_row_blks, n // bn),
            in_specs=in_specs,
            out_specs=out_specs,
            out_shape=out_shape,
            input_output_aliases=aliases,
            compiler_params=_params(("parallel", "arbitrary"), blk),
            name="proj_" + epilogue + ("_fold" if fold_weight else ""),
        )(*args)

    out, w_bf16 = call(True, w, 0, 1, bn_fold, None)
    if m // bm > 1:
        out = call(False, w_bf16, 1, m // bm - 1, bn, out)[0]
    return out


def _resid_kernel(*refs, nk, emit_stats, fold_weight):
    refs = list(refs)
    a_ref, w_ref, r_ref, x_ref = refs[:4]
    del refs[:4]
    if emit_stats:
        xb_ref, ss_ref = refs[:2]
        del refs[:2]
    if fold_weight:
        b = w_ref[...].astype(BF16)
        refs.pop(0)[...] = b
    else:
        b = w_ref[...]
    part = jnp.dot(a_ref[...], b, preferred_element_type=F32)
    j = pl.program_id(1)

    def finish(val):
        x_ref[...] = val
        if not emit_stats:
            return
        xb_ref[...] = val.astype(BF16)
        sq = _fold_lanes(val * val)

        @pl.when(j == 0)
        def _():
            ss_ref[...] = sq

        @pl.when(j > 0)
        def _():
            ss_ref[...] += sq

    if nk == 1:
        finish(r_ref[...] + part)
        return
    k = pl.program_id(2)

    @pl.when(k == 0)
    def _():
        x_ref[...] = r_ref[...] + part

    @pl.when(jnp.logical_and(k > 0, k < nk - 1))
    def _():
        x_ref[...] += part

    @pl.when(k == nk - 1)
    def _():
        finish(x_ref[...] + part)


def _residual_matmul(a, w, x_parts, *, emit_stats=True, bm=1024, bn=1024, bk_max=4096, bn_fold=512):
    m, k = a.shape
    _, n = w.shape
    bm = _divisor_block(math.gcd(m, *[xp.shape[0] for xp in x_parts]), bm, 8)
    if emit_stats and k > bk_max:
        x_new, _, _ = _residual_matmul(a, w, x_parts, emit_stats=False, bm=bm, bn=bn, bk_max=bk_max,
                                       bn_fold=bn_fold)
        return (x_new,) + tuple(_cast_stats([x_new]))

    def call(fold_weight, weight, r, row_blk0, r_blk0, n_row_blks, bn, shared):
        bn = _divisor_block(n, bn, HEAD_DIM)
        w_bytes = 4 if fold_weight else 2

        def footprint(bk):
            byt = 2 * (bm * bk * 2 + bk * bn * w_bytes + 2 * bm * bn * 4) + 2 * bm * bn * 4
            if emit_stats:
                byt += 2 * (bm * bn * 2 + bm * HEAD_DIM * 4)
            if fold_weight:
                byt += 3 * bk * bn * 2
            return byt

        bk = _divisor_block(k, bk_max, HEAD_DIM)
        while footprint(bk) > VMEM_CAP_BYTES and bk % (2 * HEAD_DIM) == 0:
            bk //= 2
        nk = k // bk
        in_specs = [pl.BlockSpec((bm, bk), lambda i, j, kk: (i + row_blk0, kk)),
                    pl.BlockSpec((bk, bn), lambda i, j, kk: (kk, j)),
                    pl.BlockSpec((bm, bn), lambda i, j, kk: (i + r_blk0, j))]
        args = [a, weight, r]
        tile = pl.BlockSpec((bm, bn), lambda i, j, kk: (i + row_blk0, j))
        out_specs = [tile]
        out_shape = [jax.ShapeDtypeStruct((m, n), F32)]
        if emit_stats:
            out_specs += [tile, pl.BlockSpec((bm, HEAD_DIM), lambda i, j, kk: (i + row_blk0, 0))]
            out_shape += [jax.ShapeDtypeStruct((m, n), BF16), jax.ShapeDtypeStruct((m, HEAD_DIM), F32)]
        n_shared = len(out_specs)
        if fold_weight:
            out_specs.append(pl.BlockSpec((bk, bn), lambda i, j, kk: (kk, j)))
            out_shape.append(jax.ShapeDtypeStruct((k, n), BF16))
        kernel_fn = functools.partial(_resid_kernel, nk=nk, emit_stats=emit_stats, fold_weight=fold_weight)
        aliases = {}
        if shared is not None:
            n_in = len(args)
            inner = kernel_fn

            def kernel_fn(*refs):
                return inner(*refs[:n_in], *refs[n_in + n_shared:])

            in_specs += [pl.BlockSpec(memory_space=pl.ANY)] * n_shared
            args += list(shared)
            aliases = {n_in + t: t for t in range(n_shared)}
        out = pl.pallas_call(
            kernel_fn,
            grid=(n_row_blks, n // bn, nk),
            in_specs=in_specs,
            out_specs=out_specs,
            out_shape=out_shape,
            input_output_aliases=aliases,
            compiler_params=_params(("parallel", "arbitrary", "arbitrary"), footprint(bk)),
            name="resid_matmul" + ("_fold" if fold_weight else ""),
        )(*args)
        return tuple(out[:n_shared]), (out[n_shared] if fold_weight else weight)

    shared, w_bf16 = call(True, w, x_parts[0], 0, 0, 1, bn_fold, None)
    row_blk0 = 0
    for part_idx, xp in enumerate(x_parts):
        n_blks = xp.shape[0] // bm
        skip = 1 if part_idx == 0 else 0
        if n_blks > skip:
            shared, _ = call(False, w_bf16, xp, row_blk0 + skip, skip, n_blks - skip, bn, shared)
        row_blk0 += n_blks
    return shared if emit_stats else (shared[0], None, None)


def _rope_angles(pos, dim, theta):
    inv = theta ** (-jnp.arange(0, dim, 2, dtype=F32) / dim)
    ang = pos.astype(F32)[:, None] * inv[None, :]
    return jnp.cos(ang), jnp.sin(ang)


def _partial_rope_tables(pos):
    cos, sin = _rope_angles(pos, PARTIAL_ROT, ROPE_THETA)
    t, h = cos.shape
    rest = HEAD_DIM - 2 * h
    c = jnp.concatenate([cos, cos, jnp.ones((t, rest), F32)], axis=1)
    sa = jnp.concatenate([-sin, jnp.zeros((t, HEAD_DIM - h), F32)], axis=1)
    sb = jnp.concatenate([jnp.zeros((t, h), F32), sin, jnp.zeros((t, rest), F32)], axis=1)
    return c, sa, sb, h


def _axial_rope_tables(pos):
    rcos, rsin = _rope_angles(pos // GRID_W, AXIAL_DIM, AXIAL_THETA)
    ccos, csin = _rope_angles(pos % GRID_W, AXIAL_DIM, AXIAL_THETA)
    z = jnp.zeros_like(rsin)
    c = jnp.concatenate([rcos, rcos, ccos, ccos], axis=1)
    sa = jnp.concatenate([-rsin, z, -csin, z], axis=1)
    sb = jnp.concatenate([z, rsin, z, csin], axis=1)
    return c, sa, sb, rcos.shape[1]


def _rows(start, size):
    if isinstance(start, int):
        return pl.ds(start, size)
    return pl.ds(pl.multiple_of(start, size), size)


def _attention_call(kernel_fn, *, grid, in_specs, args, out_block, out_index, b_off, shared_out, out_view,
                    scratch=(), semantics, block_bytes, name):
    n_in = len(args)
    aliases = {}
    if shared_out is not None:
        inner = kernel_fn

        def kernel_fn(*refs):
            return inner(*refs[:n_in], *refs[n_in + 1:])

        in_specs = list(in_specs) + [pl.BlockSpec(memory_space=pl.ANY)]
        args = list(args) + [shared_out.reshape(out_view)]
        aliases = {n_in: 0}

    def out_map(b, *rest):
        return (b + b_off,) + tuple(out_index(b, *rest))

    return pl.pallas_call(
        kernel_fn,
        grid=grid,
        in_specs=in_specs,
        out_specs=pl.BlockSpec(out_block, out_map),
        out_shape=jax.ShapeDtypeStruct(out_view, BF16),
        scratch_shapes=list(scratch),
        input_output_aliases=aliases,
        compiler_params=_params(semantics, block_bytes),
        name=name,
    )(*args)


def _transpose_bf16(x):
    return x.astype(F32).T.astype(BF16)


def _build_vt(v_ref, vt_ref, n_chunks, bk):
    @pl.when(pl.program_id(2) == 0)
    def _():
        def body(c, carry):
            vt_ref[c] = _transpose_bf16(v_ref[_rows(c * bk, bk), :])
            return carry

        lax.fori_loop(0, n_chunks, body, 0)


def _attend_t(score_t, vt_ref, s_ref, m_ref, l_ref, acc_ref, n_chunks):
    assert n_chunks % 2 == 0

    def scores(c, slot):
        s = score_t(c)
        s_ref[slot] = s
        return jnp.max(s, axis=0, keepdims=True)

    def softmax_pv(c, slot, chunk_max):
        m_old = m_ref[...]
        m_new = jnp.maximum(m_old, chunk_max)
        alpha = jnp.exp2(m_old - m_new)
        p = jnp.exp2(s_ref[slot] - m_new)
        l_ref[...] = alpha * l_ref[...] + jnp.sum(p, axis=0, keepdims=True)
        acc_ref[...] = alpha * acc_ref[...] + jnp.dot(vt_ref[c], p.astype(BF16), preferred_element_type=F32)
        m_ref[...] = m_new

    def pair(c0, max0, last):
        max1 = scores(c0 + 1, 1)
        softmax_pv(c0, 0, max0)
        next_max = None if last else scores(c0 + 2, 0)
        softmax_pv(c0 + 1, 1, max1)
        return next_max

    m_ref[...] = jnp.full(m_ref.shape, NEG_BIG, F32)
    l_ref[...] = jnp.zeros(l_ref.shape, F32)
    acc_ref[...] = jnp.zeros(acc_ref.shape, F32)
    max0 = scores(0, 0)
    max0 = lax.fori_loop(0, n_chunks // 2 - 1, lambda ci, mx: pair(2 * ci, mx, False), max0)
    pair(n_chunks - 2, max0, True)
    return acc_ref[...] / l_ref[...]


def _diff_kernel(q_ref, k_ref, v_ref, lq1_ref, lk1_ref, lq2_ref, lk2_ref, sub_ref, o_ref,
                 vt_ref, qt_ref, *flash_refs, n_chunks, bq, bk, lambda_init):
    _build_vt(v_ref, vt_ref, n_chunks, bk)
    for h in range(2):
        qt_ref[h] = _transpose_bf16(q_ref[:, h * HEAD_DIM:(h + 1) * HEAD_DIM])

    def score_t(c):
        rows = _rows(c * bk, bk)
        s1 = jnp.dot(k_ref[rows, 0:HEAD_DIM], qt_ref[0], preferred_element_type=F32)
        s2 = jnp.dot(k_ref[rows, HEAD_DIM:2 * HEAD_DIM], qt_ref[1], preferred_element_type=F32)
        return jnp.concatenate([s1, s2], axis=1)

    o = _attend_t(score_t, vt_ref, *flash_refs, n_chunks).T
    lam = (jnp.exp(jnp.sum(lq1_ref[...] * lk1_ref[...], axis=-1, keepdims=True))
           - jnp.exp(jnp.sum(lq2_ref[...] * lk2_ref[...], axis=-1, keepdims=True)) + lambda_init)
    d = o[:bq] - lam * o[bq:]
    y = d * lax.rsqrt(jnp.mean(d * d, axis=-1, keepdims=True) + SUBLN_EPS) * sub_ref[...]
    o_ref[...] = (y * (1.0 - lambda_init)).astype(o_ref.dtype)


def _flash_scratch(n_chunks, dv, bk, m_rows, qt_shape):
    return [pltpu.VMEM((n_chunks, dv, bk), BF16), pltpu.VMEM(qt_shape, BF16),
            pltpu.VMEM((2, bk, m_rows), F32), pltpu.VMEM((1, m_rows), F32), pltpu.VMEM((1, m_rows), F32),
            pltpu.VMEM((dv, m_rows), F32)]


def _diff_attention(qkv3, b_off, nb, shared_out, lam_params, subln, lambda_init, *, bq=1024, bk=1024):
    n_seq, seq, three_d = qkv3.shape
    d = three_d // 3
    n_pairs = d // (2 * HEAD_DIM)
    w = 2 * HEAD_DIM
    bq, bk = min(bq, seq), min(bk, seq // 2)
    n_chunks = seq // bk
    vec = pl.BlockSpec((1, HEAD_DIM), lambda b, t, i: (0, 0))
    blk = 2 * (2 * seq * w * 2 + 2 * bq * w * 2) + seq * w * 2 + 8 * bk * 2 * bq * 4
    return _attention_call(
        functools.partial(_diff_kernel, n_chunks=n_chunks, bq=bq, bk=bk, lambda_init=lambda_init),
        grid=(nb, n_pairs, seq // bq),
        in_specs=[pl.BlockSpec((None, bq, w), lambda b, t, i: (b + b_off, i, t)),
                  pl.BlockSpec((None, seq, w), lambda b, t, i: (b + b_off, 0, d // w + t)),
                  pl.BlockSpec((None, seq, w), lambda b, t, i: (b + b_off, 0, 2 * d // w + t)),
                  vec, vec, vec, vec,
                  pl.BlockSpec((1, w), lambda b, t, i: (0, 0))],
        args=[qkv3, qkv3, qkv3, *[p.reshape(1, HEAD_DIM).astype(F32) for p in lam_params],
              subln.reshape(1, w).astype(F32)],
        out_block=(None, bq, w), out_index=lambda b, t, i: (i, t),
        b_off=b_off, shared_out=shared_out, out_view=(n_seq, seq, d),
        scratch=_flash_scratch(n_chunks, w, bk, 2 * bq, (2, HEAD_DIM, bq)),
        semantics=("parallel", "parallel", "arbitrary"), block_bytes=blk, name="diff_attention")


def _gqa_kernel(q_ref, k_ref, v_ref, o_ref, vt_ref, qt_ref, *flash_refs,
                n_chunks, bq, bk, group):
    _build_vt(v_ref, vt_ref, n_chunks, bk)
    for g in range(group):
        qt_ref[:, g * bq:(g + 1) * bq] = _transpose_bf16(q_ref[:, g * HEAD_DIM:(g + 1) * HEAD_DIM])

    def score_t(c):
        return jnp.dot(k_ref[_rows(c * bk, bk), :], qt_ref[...], preferred_element_type=F32)

    ot = _attend_t(score_t, vt_ref, *flash_refs, n_chunks)
    for g in range(group):
        o_ref[:, g * HEAD_DIM:(g + 1) * HEAD_DIM] = ot[:, g * bq:(g + 1) * bq].T.astype(o_ref.dtype)


def _gqa_attention(qkv3, b_off, nb, shared_out, d, *, bq=512, bk=1024):
    n_seq, seq, n_cols = qkv3.shape
    kv_dim = (n_cols - d) // 2
    n_kv = kv_dim // HEAD_DIM
    group = d // kv_dim
    w = group * HEAD_DIM
    bq, bk = min(bq, seq), min(bk, seq // 2)
    n_chunks = seq // bk
    blk = 2 * (2 * seq * HEAD_DIM * 2 + 2 * bq * w * 2) + seq * HEAD_DIM * 2 + 8 * bk * group * bq * 4
    return _attention_call(
        functools.partial(_gqa_kernel, n_chunks=n_chunks, bq=bq, bk=bk, group=group),
        grid=(nb, n_kv, seq // bq),
        in_specs=[pl.BlockSpec((None, bq, w), lambda b, n, i: (b + b_off, i, n)),
                  pl.BlockSpec((None, seq, HEAD_DIM), lambda b, n, i: (b + b_off, 0, d // HEAD_DIM + n)),
                  pl.BlockSpec((None, seq, HEAD_DIM),
                               lambda b, n, i: (b + b_off, 0, (d + kv_dim) // HEAD_DIM + n))],
        args=[qkv3, qkv3, qkv3],
        out_block=(None, bq, w), out_index=lambda b, n, i: (i, n),
        b_off=b_off, shared_out=shared_out, out_view=(n_seq, seq, d),
        scratch=_flash_scratch(n_chunks, HEAD_DIM, bk, group * bq, (HEAD_DIM, group * bq)),
        semantics=("parallel", "parallel", "arbitrary"), block_bytes=blk, name="gqa_attention")


def _nat_bias_table(rel_bias, rows):
    n_heads = rel_bias.shape[0]
    kh = min(NA_WIN_H, rows)
    n_dr = 2 * NA_WIN_H - 1
    col = np.arange(GRID_W)
    col_start = np.clip(col - NA_WIN_W // 2, 0, GRID_W - NA_WIN_W)
    col_ok = (col[None, :] >= col_start[:, None]) & (col[None, :] < col_start[:, None] + NA_WIN_W)
    dc = np.clip(col[None, :] - col[:, None], -(NA_WIN_W - 1), NA_WIN_W - 1) + NA_WIN_W - 1
    planes = jnp.take(rel_bias.astype(F32) * LOG2_E, jnp.asarray(dc), axis=2)
    planes = jnp.where(jnp.asarray(col_ok)[None, None], planes, NEG_BIG)
    pad = jnp.full((n_heads, GRID_W, NAT_G * GRID_W), NEG_BIG, F32)
    wide = jnp.concatenate([pad, planes.transpose(0, 2, 1, 3).reshape(n_heads, GRID_W, n_dr * GRID_W), pad],
                           axis=2)
    blocks = []
    for r_base in (0, NAT_G, rows - NAT_G):
        a = int(np.clip(r_base - NA_WIN_H // 2, 0, rows - NAT_KROWS))
        for g in range(NAT_G):
            r = r_base + g
            r0 = int(np.clip(r - kh // 2, 0, rows - kh))
            dr0 = a - r + NA_WIN_H - 1
            assert -NAT_G <= dr0 and dr0 + NAT_KROWS <= n_dr + NAT_G
            row_ok = np.array([r0 <= a + j < r0 + kh for j in range(NAT_KROWS)])
            start = (dr0 + NAT_G) * GRID_W
            window = wide[:, :, start:start + NAT_KROWS * GRID_W]
            blocks.append(jnp.where(jnp.asarray(np.repeat(row_ok, GRID_W))[None, None], window, NEG_BIG))
    return jnp.stack(blocks, axis=1).reshape(n_heads, 3, NAT_G * GRID_W, NAT_KROWS * GRID_W)


def _nat_kernel(q_ref, k_ref, v_ref, bias_ref, o_ref, s_ref, p_ref, l_ref, *, rows):
    n_groups = rows // NAT_G
    gq = NAT_G * GRID_W
    gk = NAT_KROWS * GRID_W
    assert n_groups % 2 == 0 and n_groups >= 4

    def window(gi):
        a = jnp.clip(gi * NAT_G - NA_WIN_H // 2, 0, rows - NAT_KROWS)
        return pl.ds(pl.multiple_of(a * GRID_W, GRID_W), gk)

    def scores(gi, slot):
        var = jnp.where(gi == 0, 0, jnp.where(gi == n_groups - 1, 2, 1))
        s_ref[slot] = _dot_nt(q_ref[_rows(gi * gq, gq), :], k_ref[window(gi), :]) + bias_ref[var]

    def softmax(slot):
        s = s_ref[slot]
        e = jnp.exp2(s - jnp.max(s, axis=1, keepdims=True))
        l_ref[slot] = jnp.sum(e, axis=1, keepdims=True)
        p_ref[slot] = e.astype(BF16)

    def pv(gi, slot):
        o = jnp.dot(p_ref[slot], v_ref[window(gi), :], preferred_element_type=F32) / l_ref[slot]
        o_ref[_rows(gi * gq, gq), :] = o.astype(o_ref.dtype)

    def step(gi, parity):
        pv(gi - 1, 1 - parity)
        scores(gi + 1, 1 - parity)
        softmax(parity)

    scores(0, 0)
    scores(1, 1)
    softmax(0)

    def body(t, carry):
        step(2 * t + 1, 1)
        step(2 * t + 2, 0)
        return carry

    lax.fori_loop(0, n_groups // 2 - 1, body, 0)
    pv(n_groups - 2, 0)
    softmax(1)
    pv(n_groups - 1, 1)


def _nat_attention(qkv3, b_off, nb, shared_out, bias_tbl):
    n_seq, seq, three_d = qkv3.shape
    d = three_d // 3
    n_heads = d // HEAD_DIM
    rows = seq // GRID_W
    assert rows % NAT_G == 0 and rows >= NAT_KROWS and rows >= 3 * NAT_G
    gq, gk = NAT_G * GRID_W, NAT_KROWS * GRID_W
    blk = 2 * (4 * seq * HEAD_DIM * 2 + 3 * gq * gk * 4) + 6 * gq * gk * 4

    def col_spec(col0):
        return pl.BlockSpec((None, seq, HEAD_DIM), lambda b, h: (b + b_off, 0, col0 + h))

    return _attention_call(
        functools.partial(_nat_kernel, rows=rows),
        grid=(nb, n_heads),
        in_specs=[col_spec(0), col_spec(n_heads), col_spec(2 * n_heads),
                  pl.BlockSpec((None, 3, gq, gk), lambda b, h: (h, 0, 0, 0))],
        args=[qkv3, qkv3, qkv3, bias_tbl],
        out_block=(None, seq, HEAD_DIM), out_index=lambda b, h: (0, h),
        b_off=b_off, shared_out=shared_out, out_view=(n_seq, seq, d),
        scratch=[pltpu.VMEM((2, gq, gk), F32), pltpu.VMEM((2, gq, gk), BF16), pltpu.VMEM((2, gq, 1), F32)],
        semantics=("parallel", "parallel"), block_bytes=blk, name="nat_attention")


def _lambda_init(layer_idx):
    return 0.8 - 0.6 * math.exp(-0.3 * layer_idx)


def _seq_view(arr, seq):
    t, c = arr.shape
    return arr.reshape(t // seq, seq, c)


def _trunk(x_prompt, x_sample, layers, final_norm):
    bp, sp, d = x_prompt.shape
    bs, ss, _ = x_sample.shape
    tp, ts = bp * sp, bs * ss
    t = tp + ts
    assert tp % ss == 0 and t % sp == 0 and t % ss == 0
    groups = ((sp, 0, bp), (ss, tp // ss, bs))
    x_parts = [x_prompt.reshape(tp, d), x_sample.reshape(ts, d)]
    pos = jnp.concatenate([jnp.tile(jnp.arange(sp), bp), jnp.tile(jnp.arange(ss), bs)])

    def per_group(fn):
        out = None
        for seq, b_off, nb in groups:
            out = fn(seq, b_off, nb, out).reshape(t, d)
        return out

    xb, sumsq = _cast_stats(x_parts)
    for li, p in enumerate(layers):
        kind = p["kind"]
        w_qkv, g_mix = p["w_qkv"], p["norm_mix"]
        if kind == "nat":
            qkv = _project(xb, sumsq, w_qkv, g_mix, epilogue="qkv", n_q_cols=d, q_scale=QK_SCALE_LOG2)
            tbls = {seq: _nat_bias_table(p["rel_bias"], seq // GRID_W) for seq, _, _ in groups}
            o = per_group(lambda seq, b_off, nb, out: _nat_attention(
                _seq_view(qkv, seq), b_off, nb, out, tbls[seq]))
        elif kind == "diff":
            qkv = _project(xb, sumsq, w_qkv, g_mix, epilogue="qkv_rope", n_q_cols=d, n_k_cols=d,
                           q_scale=QK_SCALE_LOG2, rope=_partial_rope_tables(pos))
            lam_params = (p["lq1"], p["lk1"], p["lq2"], p["lk2"])
            o = per_group(lambda seq, b_off, nb, out: _diff_attention(
                _seq_view(qkv, seq), b_off, nb, out, lam_params, p["subln"], _lambda_init(li)))
        else:
            kv_dim = (w_qkv.shape[1] - d) // 2
            norm_w = jnp.stack([p["q_norm"], p["k_norm"]]).astype(F32)
            qkv = _project(xb, sumsq, w_qkv, g_mix, epilogue="qkv_rope", n_q_cols=d, n_k_cols=kv_dim,
                           q_scale=QK_SCALE_LOG2, rope=_axial_rope_tables(pos), qk_norm_w=norm_w)
            o = per_group(lambda seq, b_off, nb, out: _gqa_attention(
                _seq_view(qkv, seq), b_off, nb, out, d))
        x, xb, sumsq = _residual_matmul(o, p["w_o"], x_parts)
        u = _project(xb, sumsq, p["w_up"], p["norm_mlp"], epilogue="relu2")
        x, xb, sumsq = _residual_matmul(u, p["w_down"], [x], emit_stats=li + 1 < len(layers))
        x_parts = [x]

    y_prompt = _rmsnorm(x, final_norm, F32, row0=0, n_rows=tp)
    y_sample = _rmsnorm(x, final_norm, F32, row0=tp, n_rows=ts)
    return y_prompt.reshape(bp, sp, d), y_sample.reshape(bs, ss, d)


def kernel(x_prompt, x_sample, l0_norm_mix, l0_w_qkv, l0_rel_bias, l0_w_o, l0_norm_mlp, l0_w_up, l0_w_down, l1_norm_mix, l1_w_qkv, l1_lambda_q1, l1_lambda_k1, l1_lambda_q2, l1_lambda_k2, l1_subln, l1_w_o, l1_norm_mlp, l1_w_up, l1_w_down, l2_norm_mix, l2_w_qkv, l2_q_norm, l2_k_norm, l2_w_o, l2_norm_mlp, l2_w_up, l2_w_down, l3_norm_mix, l3_w_qkv, l3_rel_bias, l3_w_o, l3_norm_mlp, l3_w_up, l3_w_down, final_norm):
    layers = [
        {"kind": "nat", "norm_mix": l0_norm_mix, "w_qkv": l0_w_qkv, "rel_bias": l0_rel_bias, "w_o": l0_w_o,
         "norm_mlp": l0_norm_mlp, "w_up": l0_w_up, "w_down": l0_w_down},
        {"kind": "diff", "norm_mix": l1_norm_mix, "w_qkv": l1_w_qkv, "lq1": l1_lambda_q1, "lk1": l1_lambda_k1,
         "lq2": l1_lambda_q2, "lk2": l1_lambda_k2, "subln": l1_subln, "w_o": l1_w_o,
         "norm_mlp": l1_norm_mlp, "w_up": l1_w_up, "w_down": l1_w_down},
        {"kind": "gqa", "norm_mix": l2_norm_mix, "w_qkv": l2_w_qkv, "q_norm": l2_q_norm, "k_norm": l2_k_norm,
         "w_o": l2_w_o, "norm_mlp": l2_norm_mlp, "w_up": l2_w_up, "w_down": l2_w_down},
        {"kind": "nat", "norm_mix": l3_norm_mix, "w_qkv": l3_w_qkv, "rel_bias": l3_rel_bias, "w_o": l3_w_o,
         "norm_mlp": l3_norm_mlp, "w_up": l3_w_up, "w_down": l3_w_down},
    ]
    return _trunk(x_prompt, x_sample, layers, final_norm)
```

```python
import functools
import math

import numpy as np
import jax
import jax.numpy as jnp
from jax import lax
from jax.experimental import pallas as pl
from jax.experimental.pallas import tpu as pltpu

F32 = jnp.float32
BF16 = jnp.bfloat16

HEAD_DIM = 128
GRID_W = 64
NA_WIN_H = 8
NA_WIN_W = 16
ROPE_THETA = 500000.0
PARTIAL_ROT = HEAD_DIM // 4
AXIAL_THETA = 10000.0
AXIAL_DIM = HEAD_DIM // 2
NORM_EPS = 1e-6
SUBLN_EPS = 1e-5
NEG_BIG = -1e30
LOG2_E = math.log2(math.e)
QK_SCALE_LOG2 = HEAD_DIM ** -0.5 * LOG2_E

NAT_G = 4
NAT_KROWS = NAT_G + NA_WIN_H

V7X_VMEM_BYTES = 64 * 1024 * 1024
VMEM_CAP_BYTES = V7X_VMEM_BYTES - 2 * 1024 * 1024


def _params(semantics, block_bytes):
    limit = min(VMEM_CAP_BYTES, int(block_bytes * 1.2) + (8 << 20))
    return pltpu.CompilerParams(dimension_semantics=semantics, vmem_limit_bytes=limit)


def _divisor_block(n, target, align):
    best = None
    for b in range(align, min(n, target) + 1, align):
        if n % b == 0:
            best = b
    assert best is not None, (n, target, align)
    return best


def _dot_nt(a, b):
    return lax.dot_general(a, b, (((1,), (1,)), ((), ())), preferred_element_type=F32)


def _rmsnorm_kernel(x_ref, w_ref, o_ref, *, eps):
    x = x_ref[...]
    y = x * lax.rsqrt(jnp.mean(x * x, axis=-1, keepdims=True) + eps)
    o_ref[...] = (y * w_ref[...]).astype(o_ref.dtype)


def _rmsnorm(x, w, out_dtype, *, row0=0, n_rows=None, bt=256):
    t, d = x.shape
    n_rows = t if n_rows is None else n_rows
    bt = math.gcd(math.gcd(bt, n_rows), row0) if row0 else min(bt, n_rows)
    blk0 = row0 // bt
    blk = bt * d * (4 + jnp.dtype(out_dtype).itemsize) * 2
    return pl.pallas_call(
        functools.partial(_rmsnorm_kernel, eps=NORM_EPS),
        grid=(n_rows // bt,),
        in_specs=[pl.BlockSpec((bt, d), lambda i: (i + blk0, 0)),
                  pl.BlockSpec((1, d), lambda i: (0, 0))],
        out_specs=pl.BlockSpec((bt, d), lambda i: (i, 0)),
        out_shape=jax.ShapeDtypeStruct((n_rows, d), out_dtype),
        compiler_params=_params(("parallel",), blk),
        name="rmsnorm",
    )(x, w.reshape(1, d).astype(F32))


def _fold_lanes(v):
    acc = v[:, 0:HEAD_DIM]
    for g in range(1, v.shape[1] // HEAD_DIM):
        acc = acc + v[:, g * HEAD_DIM:(g + 1) * HEAD_DIM]
    return acc


def _cast_stats_kernel(x_ref, xb_ref, ss_ref):
    x = x_ref[...]
    xb_ref[...] = x.astype(BF16)
    ss_ref[...] = _fold_lanes(x * x)


def _cast_stats(x_parts, *, bt=256):
    d = x_parts[0].shape[1]
    t = sum(xp.shape[0] for xp in x_parts)
    bt = math.gcd(bt, *[xp.shape[0] for xp in x_parts])
    out = None
    blk0 = 0
    for xp in x_parts:
        in_specs = [pl.BlockSpec((bt, d), lambda i: (i, 0))]
        args = [xp]
        kernel_fn = _cast_stats_kernel
        aliases = {}
        if out is not None:
            def kernel_fn(x_ref, xb_prev, ss_prev, xb_ref, ss_ref):
                _cast_stats_kernel(x_ref, xb_ref, ss_ref)

            in_specs += [pl.BlockSpec(memory_space=pl.ANY)] * 2
            args += list(out)
            aliases = {1: 0, 2: 1}
        out = pl.pallas_call(
            kernel_fn,
            grid=(xp.shape[0] // bt,),
            in_specs=in_specs,
            out_specs=[pl.BlockSpec((bt, d), lambda i, b0=blk0: (i + b0, 0)),
                       pl.BlockSpec((bt, HEAD_DIM), lambda i, b0=blk0: (i + b0, 0))],
            out_shape=[jax.ShapeDtypeStruct((t, d), BF16), jax.ShapeDtypeStruct((t, HEAD_DIM), F32)],
            input_output_aliases=aliases,
            compiler_params=_params(("parallel",), bt * d * 6 * 2),
            name="cast_stats",
        )(*args)
        blk0 += xp.shape[0] // bt
    return tuple(out)


def _proj_kernel(*refs, d_model, epilogue, fold_weight, n_q_blocks, n_k_blocks, q_scale, rope_shift, qk_norm,
                 heads_per_block):
    refs = list(refs)
    a_ref, w_ref = refs[:2]
    del refs[:2]
    gain_ref = refs.pop(0) if fold_weight else None
    ss_ref = refs.pop(0)
    if epilogue == "qkv_rope":
        c_ref, sa_ref, sb_ref = refs[:3]
        del refs[:3]
        nw_ref = refs.pop(0) if qk_norm else None
    o_ref = refs.pop(0)
    if fold_weight:
        b = (gain_ref[...] * w_ref[...]).astype(BF16)
        refs.pop(0)[...] = b
    else:
        b = w_ref[...]
    rstd = lax.rsqrt(jnp.sum(ss_ref[...], axis=-1, keepdims=True) / d_model + NORM_EPS)
    if epilogue != "qkv_rope":
        y = jnp.dot(a_ref[...], b, preferred_element_type=F32) * rstd
        if epilogue == "relu2":
            o_ref[...] = jnp.square(jnp.maximum(y, 0.0)).astype(o_ref.dtype)
        else:
            sc = jnp.where(pl.program_id(1) < n_q_blocks, q_scale, 1.0).astype(F32)
            o_ref[...] = (y * sc).astype(o_ref.dtype)
        return
    j = pl.program_id(1)
    is_q = j < n_q_blocks
    is_qk = j < n_q_blocks + n_k_blocks
    sc = jnp.where(is_q, q_scale, 1.0).astype(F32)
    c = c_ref[...]
    sa = sa_ref[...]
    sb = sb_ref[...]
    if qk_norm:
        gain = jnp.where(is_q, nw_ref[0:1, :], jnp.where(is_qk, nw_ref[1:2, :], 1.0))
    y = jnp.dot(a_ref[...], b, preferred_element_type=F32) * rstd
    for hh in range(heads_per_block):
        lanes = slice(hh * HEAD_DIM, (hh + 1) * HEAD_DIM)
        x = y[:, lanes]
        if qk_norm:
            inv = lax.rsqrt(jnp.mean(x * x, axis=-1, keepdims=True) + NORM_EPS)
            x = x * jnp.where(is_qk, inv, 1.0) * gain
        x = x * c + pltpu.roll(x, HEAD_DIM - rope_shift, 1) * sa + pltpu.roll(x, rope_shift, 1) * sb
        o_ref[:, lanes] = (x * sc).astype(o_ref.dtype)


def _project(xb, sumsq, w, gain, *, epilogue, n_q_cols=0, n_k_cols=0, q_scale=1.0, rope=None, qk_norm_w=None,
             bm=1024, bn=1024, bn_fold=512):
    m, k = xb.shape
    _, n = w.shape
    bm = _divisor_block(m, bm, 8)
    if epilogue == "qkv_rope":
        c, sa, sb, shift = rope
        tables = [jnp.stack([c, jnp.ones_like(c)]), jnp.stack([sa, jnp.zeros_like(sa)]),
                  jnp.stack([sb, jnp.zeros_like(sb)])]
    else:
        shift, tables = 0, []

    def call(fold_weight, weight, row_blk0, n_row_blks, bn, shared_out):
        bn = _divisor_block(n, bn, HEAD_DIM)
        if n_q_cols:
            bn = math.gcd(math.gcd(bn, n_q_cols), n_k_cols) if n_k_cols else math.gcd(bn, n_q_cols)
        in_specs = [pl.BlockSpec((bm, k), lambda i, j: (i + row_blk0, 0)),
                    pl.BlockSpec((k, bn), lambda i, j: (0, j))]
        args = [xb, weight]
        if fold_weight:
            in_specs.append(pl.BlockSpec((k, 1), lambda i, j: (0, 0)))
            args.append(gain.reshape(k, 1).astype(F32))
        in_specs.append(pl.BlockSpec((bm, HEAD_DIM), lambda i, j: (i + row_blk0, 0)))
        args.append(sumsq)
        if tables:
            n_qk_blocks = (n_q_cols + n_k_cols) // bn
            tab_spec = pl.BlockSpec((None, bm, HEAD_DIM),
                                    lambda i, j: (jnp.where(j < n_qk_blocks, 0, 1), i + row_blk0, 0))
            in_specs += [tab_spec] * 3
            args += tables
            if qk_norm_w is not None:
                in_specs.append(pl.BlockSpec((2, HEAD_DIM), lambda i, j: (0, 0)))
                args.append(qk_norm_w)
        out_specs = [pl.BlockSpec((bm, bn), lambda i, j: (i + row_blk0, j))]
        out_shape = [jax.ShapeDtypeStruct((m, n), BF16)]
        w_bytes = 4 if fold_weight else 2
        blk = 2 * (bm * k * 2 + k * bn * w_bytes + bm * bn * 2 + 4 * bm * HEAD_DIM * 4) + 2 * bm * bn * 4
        aliases = {}
        if fold_weight:
            out_specs.append(pl.BlockSpec((k, bn), lambda i, j: (0, j)))
            out_shape.append(jax.ShapeDtypeStruct((k, n), BF16))
            blk += 3 * k * bn * 2 + k * HEAD_DIM * 4
        kernel_fn = functools.partial(
            _proj_kernel, d_model=k, epilogue=epilogue, fold_weight=fold_weight, n_q_blocks=n_q_cols // bn,
            n_k_blocks=n_k_cols // bn, q_scale=q_scale, rope_shift=shift, qk_norm=qk_norm_w is not None,
            heads_per_block=bn // HEAD_DIM)
        if shared_out is not None:
            n_in = len(args)
            inner = kernel_fn

            def kernel_fn(*refs):
                return inner(*refs[:n_in], *refs[n_in + 1:])

            in_specs.append(pl.BlockSpec(memory_space=pl.ANY))
            args.append(shared_out)
            aliases = {n_in: 0}
        return pl.pallas_call(
            kernel_fn,
            grid=(n_row_blks, n // bn),
            in_specs=in_specs,
            out_specs=out_specs,
            out_shape=out_shape,
            input_output_aliases=aliases,
            compiler_params=_params(("parallel", "arbitrary"), blk),
            name="proj_" + epilogue + ("_fold" if fold_weight else ""),
        )(*args)

    out, w_bf16 = call(True, w, 0, 1, bn_fold, None)
    if m // bm > 1:
        out = call(False, w_bf16, 1, m // bm - 1, bn, out)[0]
    return out


def _resid_kernel(*refs, nk, emit_stats, fold_weight):
    refs = list(refs)
    a_ref, w_ref, r_ref, x_ref = refs[:4]
    del refs[:4]
    if emit_stats:
        xb_ref, ss_ref = refs[:2]
        del refs[:2]
    if fold_weight:
        b = w_ref[...].astype(BF16)
        refs.pop(0)[...] = b
    else:
        b = w_ref[...]
    part = jnp.dot(a_ref[...], b, preferred_element_type=F32)
    j = pl.program_id(1)

    def finish(val):
        x_ref[...] = val
        if not emit_stats:
            return
        xb_ref[...] = val.astype(BF16)
        sq = _fold_lanes(val * val)

        @pl.when(j == 0)
        def _():
            ss_ref[...] = sq

        @pl.when(j > 0)
        def _():
            ss_ref[...] += sq

    if nk == 1:
        finish(r_ref[...] + part)
        return
    k = pl.program_id(2)

    @pl.when(k == 0)
    def _():
        x_ref[...] = r_ref[...] + part

    @pl.when(jnp.logical_and(k > 0, k < nk - 1))
    def _():
        x_ref[...] += part

    @pl.when(k == nk - 1)
    def _():
        finish(x_ref[...] + part)


def _residual_matmul(a, w, x_parts, *, emit_stats=True, bm=1024, bn=1024, bk_max=4096, bn_fold=512):
    m, k = a.shape
    _, n = w.shape
    bm = _divisor_block(math.gcd(m, *[xp.shape[0] for xp in x_parts]), bm, 8)
    if emit_stats and k > bk_max:
        x_new, _, _ = _residual_matmul(a, w, x_parts, emit_stats=False, bm=bm, bn=bn, bk_max=bk_max,
                                       bn_fold=bn_fold)
        return (x_new,) + tuple(_cast_stats([x_new]))

    def call(fold_weight, weight, r, row_blk0, r_blk0, n_row_blks, bn, shared):
        bn = _divisor_block(n, bn, HEAD_DIM)
        w_bytes = 4 if fold_weight else 2

        def footprint(bk):
            byt = 2 * (bm * bk * 2 + bk * bn * w_bytes + 2 * bm * bn * 4) + 2 * bm * bn * 4
            if emit_stats:
                byt += 2 * (bm * bn * 2 + bm * HEAD_DIM * 4)
            if fold_weight:
                byt += 3 * bk * bn * 2
            return byt

        bk = _divisor_block(k, bk_max, HEAD_DIM)
        while footprint(bk) > VMEM_CAP_BYTES and bk % (2 * HEAD_DIM) == 0:
            bk //= 2
        nk = k // bk
        in_specs = [pl.BlockSpec((bm, bk), lambda i, j, kk: (i + row_blk0, kk)),
                    pl.BlockSpec((bk, bn), lambda i, j, kk: (kk, j)),
                    pl.BlockSpec((bm, bn), lambda i, j, kk: (i + r_blk0, j))]
        args = [a, weight, r]
        tile = pl.BlockSpec((bm, bn), lambda i, j, kk: (i + row_blk0, j))
        out_specs = [tile]
        out_shape = [jax.ShapeDtypeStruct((m, n), F32)]
        if emit_stats:
            out_specs += [tile, pl.BlockSpec((bm, HEAD_DIM), lambda i, j, kk: (i + row_blk0, 0))]
            out_shape += [jax.ShapeDtypeStruct((m, n), BF16), jax.ShapeDtypeStruct((m, HEAD_DIM), F32)]
        n_shared = len(out_specs)
        if fold_weight:
            out_specs.append(pl.BlockSpec((bk, bn), lambda i, j, kk: (kk, j)))
            out_shape.append(jax.ShapeDtypeStruct((k, n), BF16))
        kernel_fn = functools.partial(_resid_kernel, nk=nk, emit_stats=emit_stats, fold_weight=fold_weight)
        aliases = {}
        if shared is not None:
            n_in = len(args)
            inner = kernel_fn

            def kernel_fn(*refs):
                return inner(*refs[:n_in], *refs[n_in + n_shared:])

            in_specs += [pl.BlockSpec(memory_space=pl.ANY)] * n_shared
            args += list(shared)
            aliases = {n_in + t: t for t in range(n_shared)}
        out = pl.pallas_call(
            kernel_fn,
            grid=(n_row_blks, n // bn, nk),
            in_specs=in_specs,
            out_specs=out_specs,
            out_shape=out_shape,
            input_output_aliases=aliases,
            compiler_params=_params(("parallel", "arbitrary", "arbitrary"), footprint(bk)),
            name="resid_matmul" + ("_fold" if fold_weight else ""),
        )(*args)
        return tuple(out[:n_shared]), (out[n_shared] if fold_weight else weight)

    shared, w_bf16 = call(True, w, x_parts[0], 0, 0, 1, bn_fold, None)
    row_blk0 = 0
    for part_idx, xp in enumerate(x_parts):
        n_blks = xp.shape[0] // bm
        skip = 1 if part_idx == 0 else 0
        if n_blks > skip:
            shared, _ = call(False, w_bf16, xp, row_blk0 + skip, skip, n_blks - skip, bn, shared)
        row_blk0 += n_blks
    return shared if emit_stats else (shared[0], None, None)


def _rope_angles(pos, dim, theta):
    inv = theta ** (-jnp.arange(0, dim, 2, dtype=F32) / dim)
    ang = pos.astype(F32)[:, None] * inv[None, :]
    return jnp.cos(ang), jnp.sin(ang)


def _partial_rope_tables(pos):
    cos, sin = _rope_angles(pos, PARTIAL_ROT, ROPE_THETA)
    t, h = cos.shape
    rest = HEAD_DIM - 2 * h
    c = jnp.concatenate([cos, cos, jnp.ones((t, rest), F32)], axis=1)
    sa = jnp.concatenate([-sin, jnp.zeros((t, HEAD_DIM - h), F32)], axis=1)
    sb = jnp.concatenate([jnp.zeros((t, h), F32), sin, jnp.zeros((t, rest), F32)], axis=1)
    return c, sa, sb, h


def _axial_rope_tables(pos):
    rcos, rsin = _rope_angles(pos // GRID_W, AXIAL_DIM, AXIAL_THETA)
    ccos, csin = _rope_angles(pos % GRID_W, AXIAL_DIM, AXIAL_THETA)
    z = jnp.zeros_like(rsin)
    c = jnp.concatenate([rcos, rcos, ccos, ccos], axis=1)
    sa = jnp.concatenate([-rsin, z, -csin, z], axis=1)
    sb = jnp.concatenate([z, rsin, z, csin], axis=1)
    return c, sa, sb, rcos.shape[1]


def _rows(start, size):
    if isinstance(start, int):
        return pl.ds(start, size)
    return pl.ds(pl.multiple_of(start, size), size)


def _attention_call(kernel_fn, *, grid, in_specs, args, out_block, out_index, b_off, shared_out, out_view,
                    scratch=(), semantics, block_bytes, name):
    n_in = len(args)
    aliases = {}
    if shared_out is not None:
        inner = kernel_fn

        def kernel_fn(*refs):
            return inner(*refs[:n_in], *refs[n_in + 1:])

        in_specs = list(in_specs) + [pl.BlockSpec(memory_space=pl.ANY)]
        args = list(args) + [shared_out.reshape(out_view)]
        aliases = {n_in: 0}

    def out_map(b, *rest):
        return (b + b_off,) + tuple(out_index(b, *rest))

    return pl.pallas_call(
        kernel_fn,
        grid=grid,
        in_specs=in_specs,
        out_specs=pl.BlockSpec(out_block, out_map),
        out_shape=jax.ShapeDtypeStruct(out_view, BF16),
        scratch_shapes=list(scratch),
        input_output_aliases=aliases,
        compiler_params=_params(semantics, block_bytes),
        name=name,
    )(*args)


def _transpose_bf16(x):
    return x.astype(F32).T.astype(BF16)


def _build_vt(v_ref, vt_ref, n_chunks, bk):
    @pl.when(pl.program_id(2) == 0)
    def _():
        def body(c, carry):
            vt_ref[c] = _transpose_bf16(v_ref[_rows(c * bk, bk), :])
            return carry

        lax.fori_loop(0, n_chunks, body, 0)


def _attend_t(score_t, vt_ref, s_ref, m_ref, l_ref, acc_ref, n_chunks):
    assert n_chunks % 2 == 0

    def scores(c, slot):
        s = score_t(c)
        s_ref[slot] = s
        return jnp.max(s, axis=0, keepdims=True)

    def softmax_pv(c, slot, chunk_max):
        m_old = m_ref[...]
        m_new = jnp.maximum(m_old, chunk_max)
        alpha = jnp.exp2(m_old - m_new)
        p = jnp.exp2(s_ref[slot] - m_new)
        l_ref[...] = alpha * l_ref[...] + jnp.sum(p, axis=0, keepdims=True)
        acc_ref[...] = alpha * acc_ref[...] + jnp.dot(vt_ref[c], p.astype(BF16), preferred_element_type=F32)
        m_ref[...] = m_new

    def pair(c0, max0, last):
        max1 = scores(c0 + 1, 1)
        softmax_pv(c0, 0, max0)
        next_max = None if last else scores(c0 + 2, 0)
        softmax_pv(c0 + 1, 1, max1)
        return next_max

    m_ref[...] = jnp.full(m_ref.shape, NEG_BIG, F32)
    l_ref[...] = jnp.zeros(l_ref.shape, F32)
    acc_ref[...] = jnp.zeros(acc_ref.shape, F32)
    max0 = scores(0, 0)
    max0 = lax.fori_loop(0, n_chunks // 2 - 1, lambda ci, mx: pair(2 * ci, mx, False), max0)
    pair(n_chunks - 2, max0, True)
    return acc_ref[...] / l_ref[...]


def _diff_kernel(q_ref, k_ref, v_ref, lq1_ref, lk1_ref, lq2_ref, lk2_ref, sub_ref, o_ref,
                 vt_ref, qt_ref, *flash_refs, n_chunks, bq, bk, lambda_init):
    _build_vt(v_ref, vt_ref, n_chunks, bk)
    for h in range(2):
        qt_ref[h] = _transpose_bf16(q_ref[:, h * HEAD_DIM:(h + 1) * HEAD_DIM])

    def score_t(c):
        rows = _rows(c * bk, bk)
        s1 = jnp.dot(k_ref[rows, 0:HEAD_DIM], qt_ref[0], preferred_element_type=F32)
        s2 = jnp.dot(k_ref[rows, HEAD_DIM:2 * HEAD_DIM], qt_ref[1], preferred_element_type=F32)
        return jnp.concatenate([s1, s2], axis=1)

    o = _attend_t(score_t, vt_ref, *flash_refs, n_chunks).T
    lam = (jnp.exp(jnp.sum(lq1_ref[...] * lk1_ref[...], axis=-1, keepdims=True))
           - jnp.exp(jnp.sum(lq2_ref[...] * lk2_ref[...], axis=-1, keepdims=True)) + lambda_init)
    d = o[:bq] - lam * o[bq:]
    y = d * lax.rsqrt(jnp.mean(d * d, axis=-1, keepdims=True) + SUBLN_EPS) * sub_ref[...]
    o_ref[...] = (y * (1.0 - lambda_init)).astype(o_ref.dtype)


def _flash_scratch(n_chunks, dv, bk, m_rows, qt_shape):
    return [pltpu.VMEM((n_chunks, dv, bk), BF16), pltpu.VMEM(qt_shape, BF16),
            pltpu.VMEM((2, bk, m_rows), F32), pltpu.VMEM((1, m_rows), F32), pltpu.VMEM((1, m_rows), F32),
            pltpu.VMEM((dv, m_rows), F32)]


def _diff_attention(qkv3, b_off, nb, shared_out, lam_params, subln, lambda_init, *, bq=1024, bk=1024):
    n_seq, seq, three_d = qkv3.shape
    d = three_d // 3
    n_pairs = d // (2 * HEAD_DIM)
    w = 2 * HEAD_DIM
    bq, bk = min(bq, seq), min(bk, seq // 2)
    n_chunks = seq // bk
    vec = pl.BlockSpec((1, HEAD_DIM), lambda b, t, i: (0, 0))
    blk = 2 * (2 * seq * w * 2 + 2 * bq * w * 2) + seq * w * 2 + 8 * bk * 2 * bq * 4
    return _attention_call(
        functools.partial(_diff_kernel, n_chunks=n_chunks, bq=bq, bk=bk, lambda_init=lambda_init),
        grid=(nb, n_pairs, seq // bq),
        in_specs=[pl.BlockSpec((None, bq, w), lambda b, t, i: (b + b_off, i, t)),
                  pl.BlockSpec((None, seq, w), lambda b, t, i: (b + b_off, 0, d // w + t)),
                  pl.BlockSpec((None, seq, w), lambda b, t, i: (b + b_off, 0, 2 * d // w + t)),
                  vec, vec, vec, vec,
                  pl.BlockSpec((1, w), lambda b, t, i: (0, 0))],
        args=[qkv3, qkv3, qkv3, *[p.reshape(1, HEAD_DIM).astype(F32) for p in lam_params],
              subln.reshape(1, w).astype(F32)],
        out_block=(None, bq, w), out_index=lambda b, t, i: (i, t),
        b_off=b_off, shared_out=shared_out, out_view=(n_seq, seq, d),
        scratch=_flash_scratch(n_chunks, w, bk, 2 * bq, (2, HEAD_DIM, bq)),
        semantics=("parallel", "parallel", "arbitrary"), block_bytes=blk, name="diff_attention")


def _gqa_kernel(q_ref, k_ref, v_ref, o_ref, vt_ref, qt_ref, *flash_refs,
                n_chunks, bq, bk, group):
    _build_vt(v_ref, vt_ref, n_chunks, bk)
    for g in range(group):
        qt_ref[:, g * bq:(g + 1) * bq] = _transpose_bf16(q_ref[:, g * HEAD_DIM:(g + 1) * HEAD_DIM])

    def score_t(c):
        return jnp.dot(k_ref[_rows(c * bk, bk), :], qt_ref[...], preferred_element_type=F32)

    ot = _attend_t(score_t, vt_ref, *flash_refs, n_chunks)
    for g in range(group):
        o_ref[:, g * HEAD_DIM:(g + 1) * HEAD_DIM] = ot[:, g * bq:(g + 1) * bq].T.astype(o_ref.dtype)


def _gqa_attention(qkv3, b_off, nb, shared_out, d, *, bq=512, bk=1024):
    n_seq, seq, n_cols = qkv3.shape
    kv_dim = (n_cols - d) // 2
    n_kv = kv_dim // HEAD_DIM
    group = d // kv_dim
    w = group * HEAD_DIM
    bq, bk = min(bq, seq), min(bk, seq // 2)
    n_chunks = seq // bk
    blk = 2 * (2 * seq * HEAD_DIM * 2 + 2 * bq * w * 2) + seq * HEAD_DIM * 2 + 8 * bk * group * bq * 4
    return _attention_call(
        functools.partial(_gqa_kernel, n_chunks=n_chunks, bq=bq, bk=bk, group=group),
        grid=(nb, n_kv, seq // bq),
        in_specs=[pl.BlockSpec((None, bq, w), lambda b, n, i: (b + b_off, i, n)),
                  pl.BlockSpec((None, seq, HEAD_DIM), lambda b, n, i: (b + b_off, 0, d // HEAD_DIM + n)),
                  pl.BlockSpec((None, seq, HEAD_DIM),
                               lambda b, n, i: (b + b_off, 0, (d + kv_dim) // HEAD_DIM + n))],
        args=[qkv3, qkv3, qkv3],
        out_block=(None, bq, w), out_index=lambda b, n, i: (i, n),
        b_off=b_off, shared_out=shared_out, out_view=(n_seq, seq, d),
        scratch=_flash_scratch(n_chunks, HEAD_DIM, bk, group * bq, (HEAD_DIM, group * bq)),
        semantics=("parallel", "parallel", "arbitrary"), block_bytes=blk, name="gqa_attention")


def _nat_bias_table(rel_bias, rows):
    n_heads = rel_bias.shape[0]
    kh = min(NA_WIN_H, rows)
    n_dr = 2 * NA_WIN_H - 1
    col = np.arange(GRID_W)
    col_start = np.clip(col - NA_WIN_W // 2, 0, GRID_W - NA_WIN_W)
    col_ok = (col[None, :] >= col_start[:, None]) & (col[None, :] < col_start[:, None] + NA_WIN_W)
    dc = np.clip(col[None, :] - col[:, None], -(NA_WIN_W - 1), NA_WIN_W - 1) + NA_WIN_W - 1
    planes = jnp.take(rel_bias.astype(F32) * LOG2_E, jnp.asarray(dc), axis=2)
    planes = jnp.where(jnp.asarray(col_ok)[None, None], planes, NEG_BIG)
    pad = jnp.full((n_heads, GRID_W, NAT_G * GRID_W), NEG_BIG, F32)
    wide = jnp.concatenate([pad, planes.transpose(0, 2, 1, 3).reshape(n_heads, GRID_W, n_dr * GRID_W), pad],
                           axis=2)
    blocks = []
    for r_base in (0, NAT_G, rows - NAT_G):
        a = int(np.clip(r_base - NA_WIN_H // 2, 0, rows - NAT_KROWS))
        for g in range(NAT_G):
            r = r_base + g
            r0 = int(np.clip(r - kh // 2, 0, rows - kh))
            dr0 = a - r + NA_WIN_H - 1
            assert -NAT_G <= dr0 and dr0 + NAT_KROWS <= n_dr + NAT_G
            row_ok = np.array([r0 <= a + j < r0 + kh for j in range(NAT_KROWS)])
            start = (dr0 + NAT_G) * GRID_W
            window = wide[:, :, start:start + NAT_KROWS * GRID_W]
            blocks.append(jnp.where(jnp.asarray(np.repeat(row_ok, GRID_W))[None, None], window, NEG_BIG))
    return jnp.stack(blocks, axis=1).reshape(n_heads, 3, NAT_G * GRID_W, NAT_KROWS * GRID_W)


def _nat_kernel(q_ref, k_ref, v_ref, bias_ref, o_ref, s_ref, p_ref, l_ref, *, rows):
    n_groups = rows // NAT_G
    gq = NAT_G * GRID_W
    gk = NAT_KROWS * GRID_W
    assert n_groups % 2 == 0 and n_groups >= 4

    def window(gi):
        a = jnp.clip(gi * NAT_G - NA_WIN_H // 2, 0, rows - NAT_KROWS)
        return pl.ds(pl.multiple_of(a * GRID_W, GRID_W), gk)

    def scores(gi, slot):
        var = jnp.where(gi == 0, 0, jnp.where(gi == n_groups - 1, 2, 1))
        s_ref[slot] = _dot_nt(q_ref[_rows(gi * gq, gq), :], k_ref[window(gi), :]) + bias_ref[var]

    def softmax(slot):
        s = s_ref[slot]
        e = jnp.exp2(s - jnp.max(s, axis=1, keepdims=True))
        l_ref[slot] = jnp.sum(e, axis=1, keepdims=True)
        p_ref[slot] = e.astype(BF16)

    def pv(gi, slot):
        o = jnp.dot(p_ref[slot], v_ref[window(gi), :], preferred_element_type=F32) / l_ref[slot]
        o_ref[_rows(gi * gq, gq), :] = o.astype(o_ref.dtype)

    def step(gi, parity):
        pv(gi - 1, 1 - parity)
        scores(gi + 1, 1 - parity)
        softmax(parity)

    scores(0, 0)
    scores(1, 1)
    softmax(0)

    def body(t, carry):
        step(2 * t + 1, 1)
        step(2 * t + 2, 0)
        return carry

    lax.fori_loop(0, n_groups // 2 - 1, body, 0)
    pv(n_groups - 2, 0)
    softmax(1)
    pv(n_groups - 1, 1)


def _nat_attention(qkv3, b_off, nb, shared_out, bias_tbl):
    n_seq, seq, three_d = qkv3.shape
    d = three_d // 3
    n_heads = d // HEAD_DIM
    rows = seq // GRID_W
    assert rows % NAT_G == 0 and rows >= NAT_KROWS and rows >= 3 * NAT_G
    gq, gk = NAT_G * GRID_W, NAT_KROWS * GRID_W
    blk = 2 * (4 * seq * HEAD_DIM * 2 + 3 * gq * gk * 4) + 6 * gq * gk * 4

    def col_spec(col0):
        return pl.BlockSpec((None, seq, HEAD_DIM), lambda b, h: (b + b_off, 0, col0 + h))

    return _attention_call(
        functools.partial(_nat_kernel, rows=rows),
        grid=(nb, n_heads),
        in_specs=[col_spec(0), col_spec(n_heads), col_spec(2 * n_heads),
                  pl.BlockSpec((None, 3, gq, gk), lambda b, h: (h, 0, 0, 0))],
        args=[qkv3, qkv3, qkv3, bias_tbl],
        out_block=(None, seq, HEAD_DIM), out_index=lambda b, h: (0, h),
        b_off=b_off, shared_out=shared_out, out_view=(n_seq, seq, d),
        scratch=[pltpu.VMEM((2, gq, gk), F32), pltpu.VMEM((2, gq, gk), BF16), pltpu.VMEM((2, gq, 1), F32)],
        semantics=("parallel", "parallel"), block_bytes=blk, name="nat_attention")


def _lambda_init(layer_idx):
    return 0.8 - 0.6 * math.exp(-0.3 * layer_idx)


def _seq_view(arr, seq):
    t, c = arr.shape
    return arr.reshape(t // seq, seq, c)


def _trunk(x_prompt, x_sample, layers, final_norm):
    bp, sp, d = x_prompt.shape
    bs, ss, _ = x_sample.shape
    tp, ts = bp * sp, bs * ss
    t = tp + ts
    assert tp % ss == 0 and t % sp == 0 and t % ss == 0
    groups = ((sp, 0, bp), (ss, tp // ss, bs))
    x_parts = [x_prompt.reshape(tp, d), x_sample.reshape(ts, d)]
    pos = jnp.concatenate([jnp.tile(jnp.arange(sp), bp), jnp.tile(jnp.arange(ss), bs)])

    def per_group(fn):
        out = None
        for seq, b_off, nb in groups:
            out = fn(seq, b_off, nb, out).reshape(t, d)
        return out

    xb, sumsq = _cast_stats(x_parts)
    for li, p in enumerate(layers):
        kind = p["kind"]
        w_qkv, g_mix = p["w_qkv"], p["norm_mix"]
        if kind == "nat":
            qkv = _project(xb, sumsq, w_qkv, g_mix, epilogue="qkv", n_q_cols=d, q_scale=QK_SCALE_LOG2)
            tbls = {seq: _nat_bias_table(p["rel_bias"], seq // GRID_W) for seq, _, _ in groups}
            o = per_group(lambda seq, b_off, nb, out: _nat_attention(
                _seq_view(qkv, seq), b_off, nb, out, tbls[seq]))
        elif kind == "diff":
            qkv = _project(xb, sumsq, w_qkv, g_mix, epilogue="qkv_rope", n_q_cols=d, n_k_cols=d,
                           q_scale=QK_SCALE_LOG2, rope=_partial_rope_tables(pos))
            lam_params = (p["lq1"], p["lk1"], p["lq2"], p["lk2"])
            o = per_group(lambda seq, b_off, nb, out: _diff_attention(
                _seq_view(qkv, seq), b_off, nb, out, lam_params, p["subln"], _lambda_init(li)))
        else:
            kv_dim = (w_qkv.shape[1] - d) // 2
            norm_w = jnp.stack([p["q_norm"], p["k_norm"]]).astype(F32)
            qkv = _project(xb, sumsq, w_qkv, g_mix, epilogue="qkv_rope", n_q_cols=d, n_k_cols=kv_dim,
                           q_scale=QK_SCALE_LOG2, rope=_axial_rope_tables(pos), qk_norm_w=norm_w)
            o = per_group(lambda seq, b_off, nb, out: _gqa_attention(
                _seq_view(qkv, seq), b_off, nb, out, d))
        x, xb, sumsq = _residual_matmul(o, p["w_o"], x_parts)
        u = _project(xb, sumsq, p["w_up"], p["norm_mlp"], epilogue="relu2")
        x, xb, sumsq = _residual_matmul(u, p["w_down"], [x], emit_stats=li + 1 < len(layers))
        x_parts = [x]

    y_prompt = _rmsnorm(x, final_norm, F32, row0=0, n_rows=tp)
    y_sample = _rmsnorm(x, final_norm, F32, row0=tp, n_rows=ts)
    return y_prompt.reshape(bp, sp, d), y_sample.reshape(bs, ss, d)


def kernel(x_prompt, x_sample, l0_norm_mix, l0_w_qkv, l0_rel_bias, l0_w_o, l0_norm_mlp, l0_w_up, l0_w_down, l1_norm_mix, l1_w_qkv, l1_lambda_q1, l1_lambda_k1, l1_lambda_q2, l1_lambda_k2, l1_subln, l1_w_o, l1_norm_mlp, l1_w_up, l1_w_down, l2_norm_mix, l2_w_qkv, l2_q_norm, l2_k_norm, l2_w_o, l2_norm_mlp, l2_w_up, l2_w_down, l3_norm_mix, l3_w_qkv, l3_rel_bias, l3_w_o, l3_norm_mlp, l3_w_up, l3_w_down, final_norm):
    layers = [
        {"kind": "nat", "norm_mix": l0_norm_mix, "w_qkv": l0_w_qkv, "rel_bias": l0_rel_bias, "w_o": l0_w_o,
         "norm_mlp": l0_norm_mlp, "w_up": l0_w_up, "w_down": l0_w_down},
        {"kind": "diff", "norm_mix": l1_norm_mix, "w_qkv": l1_w_qkv, "lq1": l1_lambda_q1, "lk1": l1_lambda_k1,
         "lq2": l1_lambda_q2, "lk2": l1_lambda_k2, "subln": l1_subln, "w_o": l1_w_o,
         "norm_mlp": l1_norm_mlp, "w_up": l1_w_up, "w_down": l1_w_down},
        {"kind": "gqa", "norm_mix": l2_norm_mix, "w_qkv": l2_w_qkv, "q_norm": l2_q_norm, "k_norm": l2_k_norm,
         "w_o": l2_w_o, "norm_mlp": l2_norm_mlp, "w_up": l2_w_up, "w_down": l2_w_down},
        {"kind": "nat", "norm_mix": l3_norm_mix, "w_qkv": l3_w_qkv, "rel_bias": l3_rel_bias, "w_o": l3_w_o,
         "norm_mlp": l3_norm_mlp, "w_up": l3_w_up, "w_down": l3_w_down},
    ]
    return _trunk(x_prompt, x_sample, layers, final_norm)
```

```python
import functools
import math

import numpy as np
import jax
import jax.numpy as jnp
from jax import lax
from jax.experimental import pallas as pl
from jax.experimental.pallas import tpu as pltpu

F32 = jnp.float32
BF16 = jnp.bfloat16

HEAD_DIM = 128
GRID_W = 64
NA_WIN_H = 8
NA_WIN_W = 16
ROPE_THETA = 500000.0
PARTIAL_ROT = HEAD_DIM // 4
AXIAL_THETA = 10000.0
AXIAL_DIM = HEAD_DIM // 2
NORM_EPS = 1e-6
SUBLN_EPS = 1e-5
NEG_BIG = -1e30
LOG2_E = math.log2(math.e)
QK_SCALE_LOG2 = HEAD_DIM ** -0.5 * LOG2_E

NAT_G = 4
NAT_KROWS = NAT_G + NA_WIN_H

V7X_VMEM_BYTES = 64 * 1024 * 1024
VMEM_CAP_BYTES = V7X_VMEM_BYTES - 2 * 1024 * 1024


def _params(semantics, block_bytes):
    limit = min(VMEM_CAP_BYTES, int(block_bytes * 1.2) + (8 << 20))
    return pltpu.CompilerParams(dimension_semantics=semantics, vmem_limit_bytes=limit)


def _divisor_block(n, target, align):
    best = None
    for b in range(align, min(n, target) + 1, align):
        if n % b == 0:
            best = b
    assert best is not None, (n, target, align)
    return best


def _dot_nt(a, b):
    return lax.dot_general(a, b, (((1,), (1,)), ((), ())), preferred_element_type=F32)


def _rmsnorm_kernel(x_ref, w_ref, o_ref, *, eps):
    x = x_ref[...]
    y = x * lax.rsqrt(jnp.mean(x * x, axis=-1, keepdims=True) + eps)
    o_ref[...] = (y * w_ref[...]).astype(o_ref.dtype)


def _rmsnorm(x, w, out_dtype, *, row0=0, n_rows=None, bt=256):
    t, d = x.shape
    n_rows = t if n_rows is None else n_rows
    bt = math.gcd(math.gcd(bt, n_rows), row0) if row0 else min(bt, n_rows)
    blk0 = row0 // bt
    blk = bt * d * (4 + jnp.dtype(out_dtype).itemsize) * 2
    return pl.pallas_call(
        functools.partial(_rmsnorm_kernel, eps=NORM_EPS),
        grid=(n_rows // bt,),
        in_specs=[pl.BlockSpec((bt, d), lambda i: (i + blk0, 0)),
                  pl.BlockSpec((1, d), lambda i: (0, 0))],
        out_specs=pl.BlockSpec((bt, d), lambda i: (i, 0)),
        out_shape=jax.ShapeDtypeStruct((n_rows, d), out_dtype),
        compiler_params=_params(("parallel",), blk),
        name="rmsnorm",
    )(x, w.reshape(1, d).astype(F32))


def _fold_lanes(v):
    acc = v[:, 0:HEAD_DIM]
    for g in range(1, v.shape[1] // HEAD_DIM):
        acc = acc + v[:, g * HEAD_DIM:(g + 1) * HEAD_DIM]
    return acc


def _cast_stats_kernel(x_ref, xb_ref, ss_ref):
    x = x_ref[...]
    xb_ref[...] = x.astype(BF16)
    ss_ref[...] = _fold_lanes(x * x)


def _cast_stats(x_parts, *, bt=256):
    d = x_parts[0].shape[1]
    t = sum(xp.shape[0] for xp in x_parts)
    bt = math.gcd(bt, *[xp.shape[0] for xp in x_parts])
    out = None
    blk0 = 0
    for xp in x_parts:
        in_specs = [pl.BlockSpec((bt, d), lambda i: (i, 0))]
        args = [xp]
        kernel_fn = _cast_stats_kernel
        aliases = {}
        if out is not None:
            def kernel_fn(x_ref, xb_prev, ss_prev, xb_ref, ss_ref):
                _cast_stats_kernel(x_ref, xb_ref, ss_ref)

            in_specs += [pl.BlockSpec(memory_space=pl.ANY)] * 2
            args += list(out)
            aliases = {1: 0, 2: 1}
        out = pl.pallas_call(
            kernel_fn,
            grid=(xp.shape[0] // bt,),
            in_specs=in_specs,
            out_specs=[pl.BlockSpec((bt, d), lambda i, b0=blk0: (i + b0, 0)),
                       pl.BlockSpec((bt, HEAD_DIM), lambda i, b0=blk0: (i + b0, 0))],
            out_shape=[jax.ShapeDtypeStruct((t, d), BF16), jax.ShapeDtypeStruct((t, HEAD_DIM), F32)],
            input_output_aliases=aliases,
            compiler_params=_params(("parallel",), bt * d * 6 * 2),
            name="cast_stats",
        )(*args)
        blk0 += xp.shape[0] // bt
    return tuple(out)


def _proj_kernel(*refs, d_model, epilogue, fold_weight, n_q_blocks, n_k_blocks, q_scale, rope_shift, qk_norm,
                 heads_per_block):
    refs = list(refs)
    a_ref, w_ref = refs[:2]
    del refs[:2]
    gain_ref = refs.pop(0) if fold_weight else None
    ss_ref = refs.pop(0)
    if epilogue == "qkv_rope":
        c_ref, sa_ref, sb_ref = refs[:3]
        del refs[:3]
        nw_ref = refs.pop(0) if qk_norm else None
    o_ref = refs.pop(0)
    if fold_weight:
        b = (gain_ref[...] * w_ref[...]).astype(BF16)
        refs.pop(0)[...] = b
    else:
        b = w_ref[...]
    rstd = lax.rsqrt(jnp.sum(ss_ref[...], axis=-1, keepdims=True) / d_model + NORM_EPS)
    if epilogue != "qkv_rope":
        y = jnp.dot(a_ref[...], b, preferred_element_type=F32) * rstd
        if epilogue == "relu2":
            o_ref[...] = jnp.square(jnp.maximum(y, 0.0)).astype(o_ref.dtype)
        else:
            sc = jnp.where(pl.program_id(1) < n_q_blocks, q_scale, 1.0).astype(F32)
            o_ref[...] = (y * sc).astype(o_ref.dtype)
        return
    j = pl.program_id(1)
    is_q = j < n_q_blocks
    is_qk = j < n_q_blocks + n_k_blocks
    sc = jnp.where(is_q, q_scale, 1.0).astype(F32)
    c = c_ref[...]
    sa = sa_ref[...]
    sb = sb_ref[...]
    if qk_norm:
        gain = jnp.where(is_q, nw_ref[0:1, :], jnp.where(is_qk, nw_ref[1:2, :], 1.0))
    y = jnp.dot(a_ref[...], b, preferred_element_type=F32) * rstd
    for hh in range(heads_per_block):
        lanes = slice(hh * HEAD_DIM, (hh + 1) * HEAD_DIM)
        x = y[:, lanes]
        if qk_norm:
            inv = lax.rsqrt(jnp.mean(x * x, axis=-1, keepdims=True) + NORM_EPS)
            x = x * jnp.where(is_qk, inv, 1.0) * gain
        x = x * c + pltpu.roll(x, HEAD_DIM - rope_shift, 1) * sa + pltpu.roll(x, rope_shift, 1) * sb
        o_ref[:, lanes] = (x * sc).astype(o_ref.dtype)


def _project(xb, sumsq, w, gain, *, epilogue, n_q_cols=0, n_k_cols=0, q_scale=1.0, rope=None, qk_norm_w=None,
             bm=1024, bn=1024, bn_fold=512):
    m, k = xb.shape
    _, n = w.shape
    bm = _divisor_block(m, bm, 8)
    if epilogue == "qkv_rope":
        c, sa, sb, shift = rope
        tables = [jnp.stack([c, jnp.ones_like(c)]), jnp.stack([sa, jnp.zeros_like(sa)]),
                  jnp.stack([sb, jnp.zeros_like(sb)])]
    else:
        shift, tables = 0, []

    def call(fold_weight, weight, row_blk0, n_row_blks, bn, shared_out):
        bn = _divisor_block(n, bn, HEAD_DIM)
        if n_q_cols:
            bn = math.gcd(math.gcd(bn, n_q_cols), n_k_cols) if n_k_cols else math.gcd(bn, n_q_cols)
        in_specs = [pl.BlockSpec((bm, k), lambda i, j: (i + row_blk0, 0)),
                    pl.BlockSpec((k, bn), lambda i, j: (0, j))]
        args = [xb, weight]
        if fold_weight:
            in_specs.append(pl.BlockSpec((k, 1), lambda i, j: (0, 0)))
            args.append(gain.reshape(k, 1).astype(F32))
        in_specs.append(pl.BlockSpec((bm, HEAD_DIM), lambda i, j: (i + row_blk0, 0)))
        args.append(sumsq)
        if tables:
            n_qk_blocks = (n_q_cols + n_k_cols) // bn
            tab_spec = pl.BlockSpec((None, bm, HEAD_DIM),
                                    lambda i, j: (jnp.where(j < n_qk_blocks, 0, 1), i + row_blk0, 0))
            in_specs += [tab_spec] * 3
            args += tables
            if qk_norm_w is not None:
                in_specs.append(pl.BlockSpec((2, HEAD_DIM), lambda i, j: (0, 0)))
                args.append(qk_norm_w)
        out_specs = [pl.BlockSpec((bm, bn), lambda i, j: (i + row_blk0, j))]
        out_shape = [jax.ShapeDtypeStruct((m, n), BF16)]
        w_bytes = 4 if fold_weight else 2
        blk = 2 * (bm * k * 2 + k * bn * w_bytes + bm * bn * 2 + 4 * bm * HEAD_DIM * 4) + 2 * bm * bn * 4
        aliases = {}
        if fold_weight:
            out_specs.append(pl.BlockSpec((k, bn), lambda i, j: (0, j)))
            out_shape.append(jax.ShapeDtypeStruct((k, n), BF16))
            blk += 3 * k * bn * 2 + k * HEAD_DIM * 4
        kernel_fn = functools.partial(
            _proj_kernel, d_model=k, epilogue=epilogue, fold_weight=fold_weight, n_q_blocks=n_q_cols // bn,
            n_k_blocks=n_k_cols // bn, q_scale=q_scale, rope_shift=shift, qk_norm=qk_norm_w is not None,
            heads_per_block=bn // HEAD_DIM)
        if shared_out is not None:
            n_in = len(args)
            inner = kernel_fn

            def kernel_fn(*refs):
                return inner(*refs[:n_in], *refs[n_in + 1:])

            in_specs.append(pl.BlockSpec(memory_space=pl.ANY))
            args.append(shared_out)
            aliases = {n_in: 0}
        return pl.pallas_call(
            kernel_fn,
            grid=(n_row_blks, n // bn),
            in_specs=in_specs,
            out_specs=out_specs,
            out_shape=out_shape,
            input_output_aliases=aliases,
            compiler_params=_params(("parallel", "arbitrary"), blk),
            name="proj_" + epilogue + ("_fold" if fold_weight else ""),
        )(*args)

    out, w_bf16 = call(True, w, 0, 1, bn_fold, None)
    if m // bm > 1:
        out = call(False, w_bf16, 1, m // bm - 1, bn, out)[0]
    return out


def _resid_kernel(*refs, nk, emit_stats, fold_weight):
    refs = list(refs)
    a_ref, w_ref, r_ref, x_ref = refs[:4]
    del refs[:4]
    if emit_stats:
        xb_ref, ss_ref = refs[:2]
        del refs[:2]
    if fold_weight:
        b = w_ref[...].astype(BF16)
        refs.pop(0)[...] = b
    else:
        b = w_ref[...]
    part = jnp.dot(a_ref[...], b, preferred_element_type=F32)
    j = pl.program_id(1)

    def finish(val):
        x_ref[...] = val
        if not emit_stats:
            return
        xb_ref[...] = val.astype(BF16)
        sq = _fold_lanes(val * val)

        @pl.when(j == 0)
        def _():
            ss_ref[...] = sq

        @pl.when(j > 0)
        def _():
            ss_ref[...] += sq

    if nk == 1:
        finish(r_ref[...] + part)
        return
    k = pl.program_id(2)

    @pl.when(k == 0)
    def _():
        x_ref[...] = r_ref[...] + part

    @pl.when(jnp.logical_and(k > 0, k < nk - 1))
    def _():
        x_ref[...] += part

    @pl.when(k == nk - 1)
    def _():
        finish(x_ref[...] + part)


def _residual_matmul(a, w, x_parts, *, emit_stats=True, bm=1024, bn=1024, bk_max=4096, bn_fold=512):
    m, k = a.shape
    _, n = w.shape
    bm = _divisor_block(math.gcd(m, *[xp.shape[0] for xp in x_parts]), bm, 8)
    xb_mode = dict(pipeline_mode=pl.Buffered(1)) if k > bk_max else {}

    def call(fold_weight, weight, r, row_blk0, r_blk0, n_row_blks, bn, shared):
        bn = _divisor_block(n, bn, HEAD_DIM)
        w_bytes = 4 if fold_weight else 2

        def footprint(bk):
            byt = 2 * (bm * bk * 2 + bk * bn * w_bytes + 2 * bm * bn * 4) + 2 * bm * bn * 4
            if emit_stats:
                byt += (1 if xb_mode else 2) * bm * bn * 2 + 2 * bm * HEAD_DIM * 4
            if fold_weight:
                byt += 3 * bk * bn * 2
            return byt

        bk = _divisor_block(k, bk_max, HEAD_DIM)
        while footprint(bk) > VMEM_CAP_BYTES and bk % (2 * HEAD_DIM) == 0:
            bk //= 2
        nk = k // bk
        in_specs = [pl.BlockSpec((bm, bk), lambda i, j, kk: (i + row_blk0, kk)),
                    pl.BlockSpec((bk, bn), lambda i, j, kk: (kk, j)),
                    pl.BlockSpec((bm, bn), lambda i, j, kk: (i + r_blk0, j))]
        args = [a, weight, r]
        tile = pl.BlockSpec((bm, bn), lambda i, j, kk: (i + row_blk0, j))
        out_specs = [tile]
        out_shape = [jax.ShapeDtypeStruct((m, n), F32)]
        if emit_stats:
            out_specs += [pl.BlockSpec((bm, bn), lambda i, j, kk: (i + row_blk0, j), **xb_mode),
                          pl.BlockSpec((bm, HEAD_DIM), lambda i, j, kk: (i + row_blk0, 0))]
            out_shape += [jax.ShapeDtypeStruct((m, n), BF16), jax.ShapeDtypeStruct((m, HEAD_DIM), F32)]
        n_shared = len(out_specs)
        if fold_weight:
            out_specs.append(pl.BlockSpec((bk, bn), lambda i, j, kk: (kk, j)))
            out_shape.append(jax.ShapeDtypeStruct((k, n), BF16))
        kernel_fn = functools.partial(_resid_kernel, nk=nk, emit_stats=emit_stats, fold_weight=fold_weight)
        aliases = {}
        if shared is not None:
            n_in = len(args)
            inner = kernel_fn

            def kernel_fn(*refs):
                return inner(*refs[:n_in], *refs[n_in + n_shared:])

            in_specs += [pl.BlockSpec(memory_space=pl.ANY)] * n_shared
            args += list(shared)
            aliases = {n_in + t: t for t in range(n_shared)}
        out = pl.pallas_call(
            kernel_fn,
            grid=(n_row_blks, n // bn, nk),
            in_specs=in_specs,
            out_specs=out_specs,
            out_shape=out_shape,
            input_output_aliases=aliases,
            compiler_params=_params(("parallel", "arbitrary", "arbitrary"), footprint(bk)),
            name="resid_matmul" + ("_fold" if fold_weight else ""),
        )(*args)
        return tuple(out[:n_shared]), (out[n_shared] if fold_weight else weight)

    shared, w_bf16 = call(True, w, x_parts[0], 0, 0, 1, bn_fold, None)
    row_blk0 = 0
    for part_idx, xp in enumerate(x_parts):
        n_blks = xp.shape[0] // bm
        skip = 1 if part_idx == 0 else 0
        if n_blks > skip:
            shared, _ = call(False, w_bf16, xp, row_blk0 + skip, skip, n_blks - skip, bn, shared)
        row_blk0 += n_blks
    return shared if emit_stats else (shared[0], None, None)


def _rope_angles(pos, dim, theta):
    inv = theta ** (-jnp.arange(0, dim, 2, dtype=F32) / dim)
    ang = pos.astype(F32)[:, None] * inv[None, :]
    return jnp.cos(ang), jnp.sin(ang)


def _partial_rope_tables(pos):
    cos, sin = _rope_angles(pos, PARTIAL_ROT, ROPE_THETA)
    t, h = cos.shape
    rest = HEAD_DIM - 2 * h
    c = jnp.concatenate([cos, cos, jnp.ones((t, rest), F32)], axis=1)
    sa = jnp.concatenate([-sin, jnp.zeros((t, HEAD_DIM - h), F32)], axis=1)
    sb = jnp.concatenate([jnp.zeros((t, h), F32), sin, jnp.zeros((t, rest), F32)], axis=1)
    return c, sa, sb, h


def _axial_rope_tables(pos):
    rcos, rsin = _rope_angles(pos // GRID_W, AXIAL_DIM, AXIAL_THETA)
    ccos, csin = _rope_angles(pos % GRID_W, AXIAL_DIM, AXIAL_THETA)
    z = jnp.zeros_like(rsin)
    c = jnp.concatenate([rcos, rcos, ccos, ccos], axis=1)
    sa = jnp.concatenate([-rsin, z, -csin, z], axis=1)
    sb = jnp.concatenate([z, rsin, z, csin], axis=1)
    return c, sa, sb, rcos.shape[1]


def _rows(start, size):
    if isinstance(start, int):
        return pl.ds(start, size)
    return pl.ds(pl.multiple_of(start, size), size)


def _attention_call(kernel_fn, *, grid, in_specs, args, out_block, out_index, b_off, shared_out, out_view,
                    scratch=(), semantics, block_bytes, name):
    n_in = len(args)
    aliases = {}
    if shared_out is not None:
        inner = kernel_fn

        def kernel_fn(*refs):
            return inner(*refs[:n_in], *refs[n_in + 1:])

        in_specs = list(in_specs) + [pl.BlockSpec(memory_space=pl.ANY)]
        args = list(args) + [shared_out.reshape(out_view)]
        aliases = {n_in: 0}

    def out_map(b, *rest):
        return (b + b_off,) + tuple(out_index(b, *rest))

    return pl.pallas_call(
        kernel_fn,
        grid=grid,
        in_specs=in_specs,
        out_specs=pl.BlockSpec(out_block, out_map),
        out_shape=jax.ShapeDtypeStruct(out_view, BF16),
        scratch_shapes=list(scratch),
        input_output_aliases=aliases,
        compiler_params=_params(semantics, block_bytes),
        name=name,
    )(*args)


def _transpose_bf16(x):
    return x.astype(F32).T.astype(BF16)


def _build_vt(v_ref, vt_ref, n_chunks, bk):
    @pl.when(pl.program_id(2) == 0)
    def _():
        def body(c, carry):
            vt_ref[c] = _transpose_bf16(v_ref[_rows(c * bk, bk), :])
            return carry

        lax.fori_loop(0, n_chunks, body, 0)


def _attend_t(score_t, vt_ref, s_ref, m_ref, l_ref, acc_ref, n_chunks):
    assert n_chunks % 2 == 0

    def scores(c, slot):
        s = score_t(c)
        s_ref[slot] = s
        return jnp.max(s, axis=0, keepdims=True)

    def softmax_pv(c, slot, chunk_max):
        m_old = m_ref[...]
        m_new = jnp.maximum(m_old, chunk_max)
        alpha = jnp.exp2(m_old - m_new)
        p = jnp.exp2(s_ref[slot] - m_new)
        l_ref[...] = alpha * l_ref[...] + jnp.sum(p, axis=0, keepdims=True)
        acc_ref[...] = alpha * acc_ref[...] + jnp.dot(vt_ref[c], p.astype(BF16), preferred_element_type=F32)
        m_ref[...] = m_new

    def pair(c0, max0, last):
        max1 = scores(c0 + 1, 1)
        softmax_pv(c0, 0, max0)
        next_max = None if last else scores(c0 + 2, 0)
        softmax_pv(c0 + 1, 1, max1)
        return next_max

    m_ref[...] = jnp.full(m_ref.shape, NEG_BIG, F32)
    l_ref[...] = jnp.zeros(l_ref.shape, F32)
    acc_ref[...] = jnp.zeros(acc_ref.shape, F32)
    max0 = scores(0, 0)
    max0 = lax.fori_loop(0, n_chunks // 2 - 1, lambda ci, mx: pair(2 * ci, mx, False), max0)
    pair(n_chunks - 2, max0, True)
    return acc_ref[...] / l_ref[...]


def _diff_kernel(q_ref, k_ref, v_ref, lq1_ref, lk1_ref, lq2_ref, lk2_ref, sub_ref, o_ref,
                 vt_ref, qt_ref, *flash_refs, n_chunks, bq, bk, lambda_init):
    _build_vt(v_ref, vt_ref, n_chunks, bk)
    for h in range(2):
        qt_ref[h] = _transpose_bf16(q_ref[:, h * HEAD_DIM:(h + 1) * HEAD_DIM])

    def score_t(c):
        rows = _rows(c * bk, bk)
        s1 = jnp.dot(k_ref[rows, 0:HEAD_DIM], qt_ref[0], preferred_element_type=F32)
        s2 = jnp.dot(k_ref[rows, HEAD_DIM:2 * HEAD_DIM], qt_ref[1], preferred_element_type=F32)
        return jnp.concatenate([s1, s2], axis=1)

    o = _attend_t(score_t, vt_ref, *flash_refs, n_chunks).T
    lam = (jnp.exp(jnp.sum(lq1_ref[...] * lk1_ref[...], axis=-1, keepdims=True))
           - jnp.exp(jnp.sum(lq2_ref[...] * lk2_ref[...], axis=-1, keepdims=True)) + lambda_init)
    d = o[:bq] - lam * o[bq:]
    y = d * lax.rsqrt(jnp.mean(d * d, axis=-1, keepdims=True) + SUBLN_EPS) * sub_ref[...]
    o_ref[...] = (y * (1.0 - lambda_init)).astype(o_ref.dtype)


def _flash_scratch(n_chunks, dv, bk, m_rows, qt_shape):
    return [pltpu.VMEM((n_chunks, dv, bk), BF16), pltpu.VMEM(qt_shape, BF16),
            pltpu.VMEM((2, bk, m_rows), F32), pltpu.VMEM((1, m_rows), F32), pltpu.VMEM((1, m_rows), F32),
            pltpu.VMEM((dv, m_rows), F32)]


def _diff_attention(qkv3, b_off, nb, shared_out, lam_params, subln, lambda_init, *, bq=1024, bk=1024):
    n_seq, seq, three_d = qkv3.shape
    d = three_d // 3
    n_pairs = d // (2 * HEAD_DIM)
    w = 2 * HEAD_DIM
    bq, bk = min(bq, seq), min(bk, seq // 2)
    n_chunks = seq // bk
    vec = pl.BlockSpec((1, HEAD_DIM), lambda b, t, i: (0, 0))
    blk = 2 * (2 * seq * w * 2 + 2 * bq * w * 2) + seq * w * 2 + 8 * bk * 2 * bq * 4
    return _attention_call(
        functools.partial(_diff_kernel, n_chunks=n_chunks, bq=bq, bk=bk, lambda_init=lambda_init),
        grid=(nb, n_pairs, seq // bq),
        in_specs=[pl.BlockSpec((None, bq, w), lambda b, t, i: (b + b_off, i, t)),
                  pl.BlockSpec((None, seq, w), lambda b, t, i: (b + b_off, 0, d // w + t)),
                  pl.BlockSpec((None, seq, w), lambda b, t, i: (b + b_off, 0, 2 * d // w + t)),
                  vec, vec, vec, vec,
                  pl.BlockSpec((1, w), lambda b, t, i: (0, 0))],
        args=[qkv3, qkv3, qkv3, *[p.reshape(1, HEAD_DIM).astype(F32) for p in lam_params],
              subln.reshape(1, w).astype(F32)],
        out_block=(None, bq, w), out_index=lambda b, t, i: (i, t),
        b_off=b_off, shared_out=shared_out, out_view=(n_seq, seq, d),
        scratch=_flash_scratch(n_chunks, w, bk, 2 * bq, (2, HEAD_DIM, bq)),
        semantics=("parallel", "parallel", "arbitrary"), block_bytes=blk, name="diff_attention")


def _gqa_kernel(q_ref, k_ref, v_ref, o_ref, vt_ref, qt_ref, *flash_refs,
                n_chunks, bq, bk, group):
    _build_vt(v_ref, vt_ref, n_chunks, bk)
    for g in range(group):
        qt_ref[:, g * bq:(g + 1) * bq] = _transpose_bf16(q_ref[:, g * HEAD_DIM:(g + 1) * HEAD_DIM])

    def score_t(c):
        return jnp.dot(k_ref[_rows(c * bk, bk), :], qt_ref[...], preferred_element_type=F32)

    ot = _attend_t(score_t, vt_ref, *flash_refs, n_chunks)
    for g in range(group):
        o_ref[:, g * HEAD_DIM:(g + 1) * HEAD_DIM] = ot[:, g * bq:(g + 1) * bq].T.astype(o_ref.dtype)


def _gqa_attention(qkv3, b_off, nb, shared_out, d, *, bq=512, bk=1024):
    n_seq, seq, n_cols = qkv3.shape
    kv_dim = (n_cols - d) // 2
    n_kv = kv_dim // HEAD_DIM
    group = d // kv_dim
    w = group * HEAD_DIM
    bq, bk = min(bq, seq), min(bk, seq // 2)
    n_chunks = seq // bk
    blk = 2 * (2 * seq * HEAD_DIM * 2 + 2 * bq * w * 2) + seq * HEAD_DIM * 2 + 8 * bk * group * bq * 4
    return _attention_call(
        functools.partial(_gqa_kernel, n_chunks=n_chunks, bq=bq, bk=bk, group=group),
        grid=(nb, n_kv, seq // bq),
        in_specs=[pl.BlockSpec((None, bq, w), lambda b, n, i: (b + b_off, i, n)),
                  pl.BlockSpec((None, seq, HEAD_DIM), lambda b, n, i: (b + b_off, 0, d // HEAD_DIM + n)),
                  pl.BlockSpec((None, seq, HEAD_DIM),
                               lambda b, n, i: (b + b_off, 0, (d + kv_dim) // HEAD_DIM + n))],
        args=[qkv3, qkv3, qkv3],
        out_block=(None, bq, w), out_index=lambda b, n, i: (i, n),
        b_off=b_off, shared_out=shared_out, out_view=(n_seq, seq, d),
        scratch=_flash_scratch(n_chunks, HEAD_DIM, bk, group * bq, (HEAD_DIM, group * bq)),
        semantics=("parallel", "parallel", "arbitrary"), block_bytes=blk, name="gqa_attention")


def _nat_bias_table(rel_bias, rows):
    n_heads = rel_bias.shape[0]
    kh = min(NA_WIN_H, rows)
    n_dr = 2 * NA_WIN_H - 1
    col = np.arange(GRID_W)
    col_start = np.clip(col - NA_WIN_W // 2, 0, GRID_W - NA_WIN_W)
    col_ok = (col[None, :] >= col_start[:, None]) & (col[None, :] < col_start[:, None] + NA_WIN_W)
    dc = np.clip(col[None, :] - col[:, None], -(NA_WIN_W - 1), NA_WIN_W - 1) + NA_WIN_W - 1
    planes = jnp.take(rel_bias.astype(F32) * LOG2_E, jnp.asarray(dc), axis=2)
    planes = jnp.where(jnp.asarray(col_ok)[None, None], planes, NEG_BIG)
    pad = jnp.full((n_heads, GRID_W, NAT_G * GRID_W), NEG_BIG, F32)
    wide = jnp.concatenate([pad, planes.transpose(0, 2, 1, 3).reshape(n_heads, GRID_W, n_dr * GRID_W), pad],
                           axis=2)
    blocks = []
    for r_base in (0, NAT_G, rows - NAT_G):
        a = int(np.clip(r_base - NA_WIN_H // 2, 0, rows - NAT_KROWS))
        for g in range(NAT_G):
            r = r_base + g
            r0 = int(np.clip(r - kh // 2, 0, rows - kh))
            dr0 = a - r + NA_WIN_H - 1
            assert -NAT_G <= dr0 and dr0 + NAT_KROWS <= n_dr + NAT_G
            row_ok = np.array([r0 <= a + j < r0 + kh for j in range(NAT_KROWS)])
            start = (dr0 + NAT_G) * GRID_W
            window = wide[:, :, start:start + NAT_KROWS * GRID_W]
            blocks.append(jnp.where(jnp.asarray(np.repeat(row_ok, GRID_W))[None, None], window, NEG_BIG))
    return jnp.stack(blocks, axis=1).reshape(n_heads, 3, NAT_G * GRID_W, NAT_KROWS * GRID_W)


def _nat_kernel(q_ref, k_ref, v_ref, bias_ref, o_ref, s_ref, p_ref, l_ref, *, rows):
    n_groups = rows // NAT_G
    gq = NAT_G * GRID_W
    gk = NAT_KROWS * GRID_W
    assert n_groups % 2 == 0 and n_groups >= 4

    def window(gi):
        a = jnp.clip(gi * NAT_G - NA_WIN_H // 2, 0, rows - NAT_KROWS)
        return pl.ds(pl.multiple_of(a * GRID_W, GRID_W), gk)

    def scores(gi, slot):
        var = jnp.where(gi == 0, 0, jnp.where(gi == n_groups - 1, 2, 1))
        s_ref[slot] = _dot_nt(q_ref[_rows(gi * gq, gq), :], k_ref[window(gi), :]) + bias_ref[var]

    def softmax(slot):
        s = s_ref[slot]
        e = jnp.exp2(s - jnp.max(s, axis=1, keepdims=True))
        l_ref[slot] = jnp.sum(e, axis=1, keepdims=True)
        p_ref[slot] = e.astype(BF16)

    def pv(gi, slot):
        o = jnp.dot(p_ref[slot], v_ref[window(gi), :], preferred_element_type=F32) / l_ref[slot]
        o_ref[_rows(gi * gq, gq), :] = o.astype(o_ref.dtype)

    def step(gi, parity):
        pv(gi - 1, 1 - parity)
        scores(gi + 1, 1 - parity)
        softmax(parity)

    scores(0, 0)
    scores(1, 1)
    softmax(0)

    def body(t, carry):
        step(2 * t + 1, 1)
        step(2 * t + 2, 0)
        return carry

    lax.fori_loop(0, n_groups // 2 - 1, body, 0)
    pv(n_groups - 2, 0)
    softmax(1)
    pv(n_groups - 1, 1)


def _nat_attention(qkv3, b_off, nb, shared_out, bias_tbl):
    n_seq, seq, three_d = qkv3.shape
    d = three_d // 3
    n_heads = d // HEAD_DIM
    rows = seq // GRID_W
    assert rows % NAT_G == 0 and rows >= NAT_KROWS and rows >= 3 * NAT_G
    gq, gk = NAT_G * GRID_W, NAT_KROWS * GRID_W
    blk = 2 * (4 * seq * HEAD_DIM * 2 + 3 * gq * gk * 4) + 6 * gq * gk * 4

    def col_spec(col0):
        return pl.BlockSpec((None, seq, HEAD_DIM), lambda b, h: (b + b_off, 0, col0 + h))

    return _attention_call(
        functools.partial(_nat_kernel, rows=rows),
        grid=(nb, n_heads),
        in_specs=[col_spec(0), col_spec(n_heads), col_spec(2 * n_heads),
                  pl.BlockSpec((None, 3, gq, gk), lambda b, h: (h, 0, 0, 0))],
        args=[qkv3, qkv3, qkv3, bias_tbl],
        out_block=(None, seq, HEAD_DIM), out_index=lambda b, h: (0, h),
        b_off=b_off, shared_out=shared_out, out_view=(n_seq, seq, d),
        scratch=[pltpu.VMEM((2, gq, gk), F32), pltpu.VMEM((2, gq, gk), BF16), pltpu.VMEM((2, gq, 1), F32)],
        semantics=("parallel", "parallel"), block_bytes=blk, name="nat_attention")


def _lambda_init(layer_idx):
    return 0.8 - 0.6 * math.exp(-0.3 * layer_idx)


def _seq_view(arr, seq):
    t, c = arr.shape
    return arr.reshape(t // seq, seq, c)


def _trunk(x_prompt, x_sample, layers, final_norm):
    bp, sp, d = x_prompt.shape
    bs, ss, _ = x_sample.shape
    tp, ts = bp * sp, bs * ss
    t = tp + ts
    assert tp % ss == 0 and t % sp == 0 and t % ss == 0
    groups = ((sp, 0, bp), (ss, tp // ss, bs))
    x_parts = [x_prompt.reshape(tp, d), x_sample.reshape(ts, d)]
    pos = jnp.concatenate([jnp.tile(jnp.arange(sp), bp), jnp.tile(jnp.arange(ss), bs)])

    def per_group(fn):
        out = None
        for seq, b_off, nb in groups:
            out = fn(seq, b_off, nb, out).reshape(t, d)
        return out

    xb, sumsq = _cast_stats(x_parts)
    for li, p in enumerate(layers):
        kind = p["kind"]
        w_qkv, g_mix = p["w_qkv"], p["norm_mix"]
        if kind == "nat":
            qkv = _project(xb, sumsq, w_qkv, g_mix, epilogue="qkv", n_q_cols=d, q_scale=QK_SCALE_LOG2)
            tbls = {seq: _nat_bias_table(p["rel_bias"], seq // GRID_W) for seq, _, _ in groups}
            o = per_group(lambda seq, b_off, nb, out: _nat_attention(
                _seq_view(qkv, seq), b_off, nb, out, tbls[seq]))
        elif kind == "diff":
            qkv = _project(xb, sumsq, w_qkv, g_mix, epilogue="qkv_rope", n_q_cols=d, n_k_cols=d,
                           q_scale=QK_SCALE_LOG2, rope=_partial_rope_tables(pos))
            lam_params = (p["lq1"], p["lk1"], p["lq2"], p["lk2"])
            o = per_group(lambda seq, b_off, nb, out: _diff_attention(
                _seq_view(qkv, seq), b_off, nb, out, lam_params, p["subln"], _lambda_init(li)))
        else:
            kv_dim = (w_qkv.shape[1] - d) // 2
            norm_w = jnp.stack([p["q_norm"], p["k_norm"]]).astype(F32)
            qkv = _project(xb, sumsq, w_qkv, g_mix, epilogue="qkv_rope", n_q_cols=d, n_k_cols=kv_dim,
                           q_scale=QK_SCALE_LOG2, rope=_axial_rope_tables(pos), qk_norm_w=norm_w)
            o = per_group(lambda seq, b_off, nb, out: _gqa_attention(
                _seq_view(qkv, seq), b_off, nb, out, d))
        x, xb, sumsq = _residual_matmul(o, p["w_o"], x_parts)
        u = _project(xb, sumsq, p["w_up"], p["norm_mlp"], epilogue="relu2")
        x, xb, sumsq = _residual_matmul(u, p["w_down"], [x], emit_stats=li + 1 < len(layers))
        x_parts = [x]

    y_prompt = _rmsnorm(x, final_norm, F32, row0=0, n_rows=tp)
    y_sample = _rmsnorm(x, final_norm, F32, row0=tp, n_rows=ts)
    return y_prompt.reshape(bp, sp, d), y_sample.reshape(bs, ss, d)


def kernel(x_prompt, x_sample, l0_norm_mix, l0_w_qkv, l0_rel_bias, l0_w_o, l0_norm_mlp, l0_w_up, l0_w_down, l1_norm_mix, l1_w_qkv, l1_lambda_q1, l1_lambda_k1, l1_lambda_q2, l1_lambda_k2, l1_subln, l1_w_o, l1_norm_mlp, l1_w_up, l1_w_down, l2_norm_mix, l2_w_qkv, l2_q_norm, l2_k_norm, l2_w_o, l2_norm_mlp, l2_w_up, l2_w_down, l3_norm_mix, l3_w_qkv, l3_rel_bias, l3_w_o, l3_norm_mlp, l3_w_up, l3_w_down, final_norm):
    layers = [
        {"kind": "nat", "norm_mix": l0_norm_mix, "w_qkv": l0_w_qkv, "rel_bias": l0_rel_bias, "w_o": l0_w_o,
         "norm_mlp": l0_norm_mlp, "w_up": l0_w_up, "w_down": l0_w_down},
        {"kind": "diff", "norm_mix": l1_norm_mix, "w_qkv": l1_w_qkv, "lq1": l1_lambda_q1, "lk1": l1_lambda_k1,
         "lq2": l1_lambda_q2, "lk2": l1_lambda_k2, "subln": l1_subln, "w_o": l1_w_o,
         "norm_mlp": l1_norm_mlp, "w_up": l1_w_up, "w_down": l1_w_down},
        {"kind": "gqa", "norm_mix": l2_norm_mix, "w_qkv": l2_w_qkv, "q_norm": l2_q_norm, "k_norm": l2_k_norm,
         "w_o": l2_w_o, "norm_mlp": l2_norm_mlp, "w_up": l2_w_up, "w_down": l2_w_down},
        {"kind": "nat", "norm_mix": l3_norm_mix, "w_qkv": l3_w_qkv, "rel_bias": l3_rel_bias, "w_o": l3_w_o,
         "norm_mlp": l3_norm_mlp, "w_up": l3_w_up, "w_down": l3_w_down},
    ]
    return _trunk(x_prompt, x_sample, layers, final_norm)
```

```python
import functools
import math

import numpy as np
import jax
import jax.numpy as jnp
from jax import lax
from jax.experimental import pallas as pl
from jax.experimental.pallas import tpu as pltpu

F32 = jnp.float32
BF16 = jnp.bfloat16

HEAD_DIM = 128
GRID_W = 64
NA_WIN_H = 8
NA_WIN_W = 16
ROPE_THETA = 500000.0
PARTIAL_ROT = HEAD_DIM // 4
AXIAL_THETA = 10000.0
AXIAL_DIM = HEAD_DIM // 2
NORM_EPS = 1e-6
SUBLN_EPS = 1e-5
NEG_BIG = -1e30
LOG2_E = math.log2(math.e)
QK_SCALE_LOG2 = HEAD_DIM ** -0.5 * LOG2_E

NAT_G = 4
NAT_KROWS = NAT_G + NA_WIN_H

V7X_VMEM_BYTES = 64 * 1024 * 1024
VMEM_CAP_BYTES = V7X_VMEM_BYTES - 2 * 1024 * 1024


def _params(semantics, block_bytes):
    limit = min(VMEM_CAP_BYTES, int(block_bytes * 1.2) + (8 << 20))
    return pltpu.CompilerParams(dimension_semantics=semantics, vmem_limit_bytes=limit)


def _divisor_block(n, target, align):
    best = None
    for b in range(align, min(n, target) + 1, align):
        if n % b == 0:
            best = b
    assert best is not None, (n, target, align)
    return best


def _dot_nt(a, b):
    return lax.dot_general(a, b, (((1,), (1,)), ((), ())), preferred_element_type=F32)


def _rmsnorm_kernel(x_ref, w_ref, o_ref, *, eps):
    x = x_ref[...]
    y = x * lax.rsqrt(jnp.mean(x * x, axis=-1, keepdims=True) + eps)
    o_ref[...] = (y * w_ref[...]).astype(o_ref.dtype)


def _rmsnorm(x, w, out_dtype, *, row0=0, n_rows=None, bt=256):
    t, d = x.shape
    n_rows = t if n_rows is None else n_rows
    bt = math.gcd(math.gcd(bt, n_rows), row0) if row0 else min(bt, n_rows)
    blk0 = row0 // bt
    blk = bt * d * (4 + jnp.dtype(out_dtype).itemsize) * 2
    return pl.pallas_call(
        functools.partial(_rmsnorm_kernel, eps=NORM_EPS),
        grid=(n_rows // bt,),
        in_specs=[pl.BlockSpec((bt, d), lambda i: (i + blk0, 0)),
                  pl.BlockSpec((1, d), lambda i: (0, 0))],
        out_specs=pl.BlockSpec((bt, d), lambda i: (i, 0)),
        out_shape=jax.ShapeDtypeStruct((n_rows, d), out_dtype),
        compiler_params=_params(("parallel",), blk),
        name="rmsnorm",
    )(x, w.reshape(1, d).astype(F32))


def _fold_lanes(v):
    acc = v[:, 0:HEAD_DIM]
    for g in range(1, v.shape[1] // HEAD_DIM):
        acc = acc + v[:, g * HEAD_DIM:(g + 1) * HEAD_DIM]
    return acc


def _cast_stats_kernel(x_ref, xb_ref, ss_ref):
    x = x_ref[...]
    xb_ref[...] = x.astype(BF16)
    ss_ref[...] = _fold_lanes(x * x)


def _cast_stats(x_parts, *, bt=256):
    d = x_parts[0].shape[1]
    t = sum(xp.shape[0] for xp in x_parts)
    bt = math.gcd(bt, *[xp.shape[0] for xp in x_parts])
    out = None
    blk0 = 0
    for xp in x_parts:
        in_specs = [pl.BlockSpec((bt, d), lambda i: (i, 0))]
        args = [xp]
        kernel_fn = _cast_stats_kernel
        aliases = {}
        if out is not None:
            def kernel_fn(x_ref, xb_prev, ss_prev, xb_ref, ss_ref):
                _cast_stats_kernel(x_ref, xb_ref, ss_ref)

            in_specs += [pl.BlockSpec(memory_space=pl.ANY)] * 2
            args += list(out)
            aliases = {1: 0, 2: 1}
        out = pl.pallas_call(
            kernel_fn,
            grid=(xp.shape[0] // bt,),
            in_specs=in_specs,
            out_specs=[pl.BlockSpec((bt, d), lambda i, b0=blk0: (i + b0, 0)),
                       pl.BlockSpec((bt, HEAD_DIM), lambda i, b0=blk0: (i + b0, 0))],
            out_shape=[jax.ShapeDtypeStruct((t, d), BF16), jax.ShapeDtypeStruct((t, HEAD_DIM), F32)],
            input_output_aliases=aliases,
            compiler_params=_params(("parallel",), bt * d * 6 * 2),
            name="cast_stats",
        )(*args)
        blk0 += xp.shape[0] // bt
    return tuple(out)


def _proj_kernel(*refs, d_model, epilogue, fold_weight, n_q_blocks, n_k_blocks, q_scale, rope_shift, qk_norm,
                 heads_per_block):
    refs = list(refs)
    a_ref, w_ref = refs[:2]
    del refs[:2]
    gain_ref = refs.pop(0) if fold_weight else None
    ss_ref = refs.pop(0)
    if epilogue == "qkv_rope":
        c_ref, sa_ref, sb_ref = refs[:3]
        del refs[:3]
        nw_ref = refs.pop(0) if qk_norm else None
    o_ref = refs.pop(0)
    if fold_weight:
        b = (gain_ref[...] * w_ref[...]).astype(BF16)
        refs.pop(0)[...] = b
    else:
        b = w_ref[...]
    rstd = lax.rsqrt(jnp.sum(ss_ref[...], axis=-1, keepdims=True) / d_model + NORM_EPS)
    if epilogue != "qkv_rope":
        y = jnp.dot(a_ref[...], b, preferred_element_type=F32) * rstd
        if epilogue == "relu2":
            o_ref[...] = jnp.square(jnp.maximum(y, 0.0)).astype(o_ref.dtype)
        else:
            sc = jnp.where(pl.program_id(1) < n_q_blocks, q_scale, 1.0).astype(F32)
            o_ref[...] = (y * sc).astype(o_ref.dtype)
        return
    j = pl.program_id(1)
    is_q = j < n_q_blocks
    is_qk = j < n_q_blocks + n_k_blocks
    sc = jnp.where(is_q, q_scale, 1.0).astype(F32)
    c = c_ref[...]
    sa = sa_ref[...]
    sb = sb_ref[...]
    if qk_norm:
        gain = jnp.where(is_q, nw_ref[0:1, :], jnp.where(is_qk, nw_ref[1:2, :], 1.0))
    y = jnp.dot(a_ref[...], b, preferred_element_type=F32) * rstd
    for hh in range(heads_per_block):
        lanes = slice(hh * HEAD_DIM, (hh + 1) * HEAD_DIM)
        x = y[:, lanes]
        if qk_norm:
            inv = lax.rsqrt(jnp.mean(x * x, axis=-1, keepdims=True) + NORM_EPS)
            x = x * jnp.where(is_qk, inv, 1.0) * gain
        x = x * c + pltpu.roll(x, HEAD_DIM - rope_shift, 1) * sa + pltpu.roll(x, rope_shift, 1) * sb
        o_ref[:, lanes] = (x * sc).astype(o_ref.dtype)


def _project(xb, sumsq, w, gain, *, epilogue, n_q_cols=0, n_k_cols=0, q_scale=1.0, rope=None, qk_norm_w=None,
             bm=1024, bn=1024, bn_fold=512):
    m, k = xb.shape
    _, n = w.shape
    bm = _divisor_block(m, bm, 8)
    if epilogue == "qkv_rope":
        c, sa, sb, shift = rope
        tables = [jnp.stack([c, jnp.ones_like(c)]), jnp.stack([sa, jnp.zeros_like(sa)]),
                  jnp.stack([sb, jnp.zeros_like(sb)])]
    else:
        shift, tables = 0, []

    def call(fold_weight, weight, row_blk0, n_row_blks, bn, shared_out):
        bn = _divisor_block(n, bn, HEAD_DIM)
        if n_q_cols:
            bn = math.gcd(math.gcd(bn, n_q_cols), n_k_cols) if n_k_cols else math.gcd(bn, n_q_cols)
        in_specs = [pl.BlockSpec((bm, k), lambda i, j: (i + row_blk0, 0)),
                    pl.BlockSpec((k, bn), lambda i, j: (0, j))]
        args = [xb, weight]
        if fold_weight:
            in_specs.append(pl.BlockSpec((k, 1), lambda i, j: (0, 0)))
            args.append(gain.reshape(k, 1).astype(F32))
        in_specs.append(pl.BlockSpec((bm, HEAD_DIM), lambda i, j: (i + row_blk0, 0)))
        args.append(sumsq)
        if tables:
            n_qk_blocks = (n_q_cols + n_k_cols) // bn
            tab_spec = pl.BlockSpec((None, bm, HEAD_DIM),
                                    lambda i, j: (jnp.where(j < n_qk_blocks, 0, 1), i + row_blk0, 0))
            in_specs += [tab_spec] * 3
            args += tables
            if qk_norm_w is not None:
                in_specs.append(pl.BlockSpec((2, HEAD_DIM), lambda i, j: (0, 0)))
                args.append(qk_norm_w)
        out_specs = [pl.BlockSpec((bm, bn), lambda i, j: (i + row_blk0, j))]
        out_shape = [jax.ShapeDtypeStruct((m, n), BF16)]
        w_bytes = 4 if fold_weight else 2
        blk = 2 * (bm * k * 2 + k * bn * w_bytes + bm * bn * 2 + 4 * bm * HEAD_DIM * 4) + 2 * bm * bn * 4
        aliases = {}
        if fold_weight:
            out_specs.append(pl.BlockSpec((k, bn), lambda i, j: (0, j)))
            out_shape.append(jax.ShapeDtypeStruct((k, n), BF16))
            blk += 3 * k * bn * 2 + k * HEAD_DIM * 4
        kernel_fn = functools.partial(
            _proj_kernel, d_model=k, epilogue=epilogue, fold_weight=fold_weight, n_q_blocks=n_q_cols // bn,
            n_k_blocks=n_k_cols // bn, q_scale=q_scale, rope_shift=shift, qk_norm=qk_norm_w is not None,
            heads_per_block=bn // HEAD_DIM)
        if shared_out is not None:
            n_in = len(args)
            inner = kernel_fn

            def kernel_fn(*refs):
                return inner(*refs[:n_in], *refs[n_in + 1:])

            in_specs.append(pl.BlockSpec(memory_space=pl.ANY))
            args.append(shared_out)
            aliases = {n_in: 0}
        return pl.pallas_call(
            kernel_fn,
            grid=(n_row_blks, n // bn),
            in_specs=in_specs,
            out_specs=out_specs,
            out_shape=out_shape,
            input_output_aliases=aliases,
            compiler_params=_params(("parallel", "arbitrary"), blk),
            name="proj_" + epilogue + ("_fold" if fold_weight else ""),
        )(*args)

    out, w_bf16 = call(True, w, 0, 1, bn_fold, None)
    if m // bm > 1:
        out = call(False, w_bf16, 1, m // bm - 1, bn, out)[0]
    return out


def _resid_kernel(*refs, nk, emit_stats, fold_weight):
    refs = list(refs)
    a_ref, w_ref, r_ref, x_ref = refs[:4]
    del refs[:4]
    if emit_stats:
        xb_ref, ss_ref = refs[:2]
        del refs[:2]
    if fold_weight:
        b = w_ref[...].astype(BF16)
        refs.pop(0)[...] = b
    else:
        b = w_ref[...]
    part = jnp.dot(a_ref[...], b, preferred_element_type=F32)
    j = pl.program_id(1)

    def finish(val):
        x_ref[...] = val
        if not emit_stats:
            return
        xb_ref[...] = val.astype(BF16)
        sq = _fold_lanes(val * val)

        @pl.when(j == 0)
        def _():
            ss_ref[...] = sq

        @pl.when(j > 0)
        def _():
            ss_ref[...] += sq

    if nk == 1:
        finish(r_ref[...] + part)
        return
    k = pl.program_id(2)

    @pl.when(k == 0)
    def _():
        x_ref[...] = r_ref[...] + part

    @pl.when(jnp.logical_and(k > 0, k < nk - 1))
    def _():
        x_ref[...] += part

    @pl.when(k == nk - 1)
    def _():
        finish(x_ref[...] + part)


def _residual_matmul(a, w, x_parts, *, emit_stats=True, bm=1024, bn=1024, bk_max=4096, bn_fold=512):
    m, k = a.shape
    _, n = w.shape
    bm = _divisor_block(math.gcd(m, *[xp.shape[0] for xp in x_parts]), bm, 8)
    if emit_stats and k > bk_max:
        x_new, _, _ = _residual_matmul(a, w, x_parts, emit_stats=False, bm=bm, bn=bn, bk_max=bk_max,
                                       bn_fold=bn_fold)
        return (x_new,) + tuple(_cast_stats([x_new]))

    def call(fold_weight, weight, r, row_blk0, r_blk0, n_row_blks, bn, shared):
        bn = _divisor_block(n, bn, HEAD_DIM)
        w_bytes = 4 if fold_weight else 2

        def footprint(bk):
            byt = 2 * (bm * bk * 2 + bk * bn * w_bytes + 2 * bm * bn * 4) + 2 * bm * bn * 4
            if emit_stats:
                byt += 2 * (bm * bn * 2 + bm * HEAD_DIM * 4)
            if fold_weight:
                byt += 3 * bk * bn * 2
            return byt

        bk = _divisor_block(k, bk_max, HEAD_DIM)
        while footprint(bk) > VMEM_CAP_BYTES and bk % (2 * HEAD_DIM) == 0:
            bk //= 2
        nk = k // bk
        in_specs = [pl.BlockSpec((bm, bk), lambda i, j, kk: (i + row_blk0, kk)),
                    pl.BlockSpec((bk, bn), lambda i, j, kk: (kk, j)),
                    pl.BlockSpec((bm, bn), lambda i, j, kk: (i + r_blk0, j))]
        args = [a, weight, r]
        tile = pl.BlockSpec((bm, bn), lambda i, j, kk: (i + row_blk0, j))
        out_specs = [tile]
        out_shape = [jax.ShapeDtypeStruct((m, n), F32)]
        if emit_stats:
            out_specs += [tile, pl.BlockSpec((bm, HEAD_DIM), lambda i, j, kk: (i + row_blk0, 0))]
            out_shape += [jax.ShapeDtypeStruct((m, n), BF16), jax.ShapeDtypeStruct((m, HEAD_DIM), F32)]
        n_shared = len(out_specs)
        if fold_weight:
            out_specs.append(pl.BlockSpec((bk, bn), lambda i, j, kk: (kk, j)))
            out_shape.append(jax.ShapeDtypeStruct((k, n), BF16))
        kernel_fn = functools.partial(_resid_kernel, nk=nk, emit_stats=emit_stats, fold_weight=fold_weight)
        aliases = {}
        if shared is not None:
            n_in = len(args)
            inner = kernel_fn

            def kernel_fn(*refs):
                return inner(*refs[:n_in], *refs[n_in + n_shared:])

            in_specs += [pl.BlockSpec(memory_space=pl.ANY)] * n_shared
            args += list(shared)
            aliases = {n_in + t: t for t in range(n_shared)}
        out = pl.pallas_call(
            kernel_fn,
            grid=(n_row_blks, n // bn, nk),
            in_specs=in_specs,
            out_specs=out_specs,
            out_shape=out_shape,
            input_output_aliases=aliases,
            compiler_params=_params(("parallel", "arbitrary", "arbitrary"), footprint(bk)),
            name="resid_matmul" + ("_fold" if fold_weight else ""),
        )(*args)
        return tuple(out[:n_shared]), (out[n_shared] if fold_weight else weight)

    shared, w_bf16 = call(True, w, x_parts[0], 0, 0, 1, bn_fold, None)
    row_blk0 = 0
    for part_idx, xp in enumerate(x_parts):
        n_blks = xp.shape[0] // bm
        skip = 1 if part_idx == 0 else 0
        if n_blks > skip:
            shared, _ = call(False, w_bf16, xp, row_blk0 + skip, skip, n_blks - skip, bn, shared)
        row_blk0 += n_blks
    return shared if emit_stats else (shared[0], None, None)


def _rope_angles(pos, dim, theta):
    inv = theta ** (-jnp.arange(0, dim, 2, dtype=F32) / dim)
    ang = pos.astype(F32)[:, None] * inv[None, :]
    return jnp.cos(ang), jnp.sin(ang)


def _partial_rope_tables(pos):
    cos, sin = _rope_angles(pos, PARTIAL_ROT, ROPE_THETA)
    t, h = cos.shape
    rest = HEAD_DIM - 2 * h
    c = jnp.concatenate([cos, cos, jnp.ones((t, rest), F32)], axis=1)
    sa = jnp.concatenate([-sin, jnp.zeros((t, HEAD_DIM - h), F32)], axis=1)
    sb = jnp.concatenate([jnp.zeros((t, h), F32), sin, jnp.zeros((t, rest), F32)], axis=1)
    return c, sa, sb, h


def _axial_rope_tables(pos):
    rcos, rsin = _rope_angles(pos // GRID_W, AXIAL_DIM, AXIAL_THETA)
    ccos, csin = _rope_angles(pos % GRID_W, AXIAL_DIM, AXIAL_THETA)
    z = jnp.zeros_like(rsin)
    c = jnp.concatenate([rcos, rcos, ccos, ccos], axis=1)
    sa = jnp.concatenate([-rsin, z, -csin, z], axis=1)
    sb = jnp.concatenate([z, rsin, z, csin], axis=1)
    return c, sa, sb, rcos.shape[1]


def _rows(start, size):
    if isinstance(start, int):
        return pl.ds(start, size)
    return pl.ds(pl.multiple_of(start, size), size)


def _attention_call(kernel_fn, *, grid, in_specs, args, out_block, out_index, b_off, shared_out, out_view, seq_axis=0,
                    scratch=(), semantics, block_bytes, name):
    n_in = len(args)
    aliases = {}
    if shared_out is not None:
        inner = kernel_fn

        def kernel_fn(*refs):
            return inner(*refs[:n_in], *refs[n_in + 1:])

        in_specs = list(in_specs) + [pl.BlockSpec(memory_space=pl.ANY)]
        args = list(args) + [shared_out.reshape(out_view)]
        aliases = {n_in: 0}

    def out_map(*ids):
        return (ids[seq_axis] + b_off,) + tuple(out_index(*ids))

    return pl.pallas_call(
        kernel_fn,
        grid=grid,
        in_specs=in_specs,
        out_specs=pl.BlockSpec(out_block, out_map),
        out_shape=jax.ShapeDtypeStruct(out_view, BF16),
        scratch_shapes=list(scratch),
        input_output_aliases=aliases,
        compiler_params=_params(semantics, block_bytes),
        name=name,
    )(*args)


def _transpose_bf16(x):
    return x.astype(F32).T.astype(BF16)


def _build_vt(v_ref, vt_ref, n_chunks, bk):
    @pl.when(pl.program_id(2) == 0)
    def _():
        def body(c, carry):
            vt_ref[c] = _transpose_bf16(v_ref[_rows(c * bk, bk), :])
            return carry

        lax.fori_loop(0, n_chunks, body, 0)


def _attend_t(score_t, vt_ref, s_ref, m_ref, l_ref, acc_ref, n_chunks):
    assert n_chunks % 2 == 0

    def scores(c, slot):
        s = score_t(c)
        s_ref[slot] = s
        return jnp.max(s, axis=0, keepdims=True)

    def softmax_pv(c, slot, chunk_max):
        m_old = m_ref[...]
        m_new = jnp.maximum(m_old, chunk_max)
        alpha = jnp.exp2(m_old - m_new)
        p = jnp.exp2(s_ref[slot] - m_new)
        l_ref[...] = alpha * l_ref[...] + jnp.sum(p, axis=0, keepdims=True)
        acc_ref[...] = alpha * acc_ref[...] + jnp.dot(vt_ref[c], p.astype(BF16), preferred_element_type=F32)
        m_ref[...] = m_new

    def pair(c0, max0, last):
        max1 = scores(c0 + 1, 1)
        softmax_pv(c0, 0, max0)
        next_max = None if last else scores(c0 + 2, 0)
        softmax_pv(c0 + 1, 1, max1)
        return next_max

    m_ref[...] = jnp.full(m_ref.shape, NEG_BIG, F32)
    l_ref[...] = jnp.zeros(l_ref.shape, F32)
    acc_ref[...] = jnp.zeros(acc_ref.shape, F32)
    max0 = scores(0, 0)
    max0 = lax.fori_loop(0, n_chunks // 2 - 1, lambda ci, mx: pair(2 * ci, mx, False), max0)
    pair(n_chunks - 2, max0, True)
    return acc_ref[...] / l_ref[...]


def _diff_kernel(q_ref, k_ref, v_ref, lq1_ref, lk1_ref, lq2_ref, lk2_ref, sub_ref, o_ref,
                 vt_ref, qt_ref, *flash_refs, n_chunks, bq, bk, lambda_init):
    _build_vt(v_ref, vt_ref, n_chunks, bk)
    for h in range(2):
        qt_ref[h] = _transpose_bf16(q_ref[:, h * HEAD_DIM:(h + 1) * HEAD_DIM])

    def score_t(c):
        rows = _rows(c * bk, bk)
        s1 = jnp.dot(k_ref[rows, 0:HEAD_DIM], qt_ref[0], preferred_element_type=F32)
        s2 = jnp.dot(k_ref[rows, HEAD_DIM:2 * HEAD_DIM], qt_ref[1], preferred_element_type=F32)
        return jnp.concatenate([s1, s2], axis=1)

    o = _attend_t(score_t, vt_ref, *flash_refs, n_chunks).T
    lam = (jnp.exp(jnp.sum(lq1_ref[...] * lk1_ref[...], axis=-1, keepdims=True))
           - jnp.exp(jnp.sum(lq2_ref[...] * lk2_ref[...], axis=-1, keepdims=True)) + lambda_init)
    d = o[:bq] - lam * o[bq:]
    y = d * lax.rsqrt(jnp.mean(d * d, axis=-1, keepdims=True) + SUBLN_EPS) * sub_ref[...]
    o_ref[...] = (y * (1.0 - lambda_init)).astype(o_ref.dtype)


def _flash_scratch(n_chunks, dv, bk, m_rows, qt_shape):
    return [pltpu.VMEM((n_chunks, dv, bk), BF16), pltpu.VMEM(qt_shape, BF16),
            pltpu.VMEM((2, bk, m_rows), F32), pltpu.VMEM((1, m_rows), F32), pltpu.VMEM((1, m_rows), F32),
            pltpu.VMEM((dv, m_rows), F32)]


def _diff_attention(qkv3, b_off, nb, shared_out, lam_params, subln, lambda_init, *, bq=1024, bk=1024):
    n_seq, seq, three_d = qkv3.shape
    d = three_d // 3
    n_pairs = d // (2 * HEAD_DIM)
    w = 2 * HEAD_DIM
    bq, bk = min(bq, seq), min(bk, seq // 2)
    n_chunks = seq // bk
    vec = pl.BlockSpec((1, HEAD_DIM), lambda b, t, i: (0, 0))
    blk = 2 * (2 * seq * w * 2 + 2 * bq * w * 2) + seq * w * 2 + 8 * bk * 2 * bq * 4
    return _attention_call(
        functools.partial(_diff_kernel, n_chunks=n_chunks, bq=bq, bk=bk, lambda_init=lambda_init),
        grid=(nb, n_pairs, seq // bq),
        in_specs=[pl.BlockSpec((None, bq, w), lambda b, t, i: (b + b_off, i, t)),
                  pl.BlockSpec((None, seq, w), lambda b, t, i: (b + b_off, 0, d // w + t)),
                  pl.BlockSpec((None, seq, w), lambda b, t, i: (b + b_off, 0, 2 * d // w + t)),
                  vec, vec, vec, vec,
                  pl.BlockSpec((1, w), lambda b, t, i: (0, 0))],
        args=[qkv3, qkv3, qkv3, *[p.reshape(1, HEAD_DIM).astype(F32) for p in lam_params],
              subln.reshape(1, w).astype(F32)],
        out_block=(None, bq, w), out_index=lambda b, t, i: (i, t),
        b_off=b_off, shared_out=shared_out, out_view=(n_seq, seq, d),
        scratch=_flash_scratch(n_chunks, w, bk, 2 * bq, (2, HEAD_DIM, bq)),
        semantics=("parallel", "parallel", "arbitrary"), block_bytes=blk, name="diff_attention")


def _gqa_kernel(q_ref, k_ref, v_ref, o_ref, vt_ref, qt_ref, *flash_refs,
                n_chunks, bq, bk, group):
    _build_vt(v_ref, vt_ref, n_chunks, bk)
    for g in range(group):
        qt_ref[:, g * bq:(g + 1) * bq] = _transpose_bf16(q_ref[:, g * HEAD_DIM:(g + 1) * HEAD_DIM])

    def score_t(c):
        return jnp.dot(k_ref[_rows(c * bk, bk), :], qt_ref[...], preferred_element_type=F32)

    ot = _attend_t(score_t, vt_ref, *flash_refs, n_chunks)
    for g in range(group):
        o_ref[:, g * HEAD_DIM:(g + 1) * HEAD_DIM] = ot[:, g * bq:(g + 1) * bq].T.astype(o_ref.dtype)


def _gqa_attention(qkv3, b_off, nb, shared_out, d, *, bq=512, bk=1024):
    n_seq, seq, n_cols = qkv3.shape
    kv_dim = (n_cols - d) // 2
    n_kv = kv_dim // HEAD_DIM
    group = d // kv_dim
    w = group * HEAD_DIM
    bq, bk = min(bq, seq), min(bk, seq // 2)
    n_chunks = seq // bk
    blk = 2 * (2 * seq * HEAD_DIM * 2 + 2 * bq * w * 2) + seq * HEAD_DIM * 2 + 8 * bk * group * bq * 4
    return _attention_call(
        functools.partial(_gqa_kernel, n_chunks=n_chunks, bq=bq, bk=bk, group=group),
        grid=(nb, n_kv, seq // bq),
        in_specs=[pl.BlockSpec((None, bq, w), lambda b, n, i: (b + b_off, i, n)),
                  pl.BlockSpec((None, seq, HEAD_DIM), lambda b, n, i: (b + b_off, 0, d // HEAD_DIM + n)),
                  pl.BlockSpec((None, seq, HEAD_DIM),
                               lambda b, n, i: (b + b_off, 0, (d + kv_dim) // HEAD_DIM + n))],
        args=[qkv3, qkv3, qkv3],
        out_block=(None, bq, w), out_index=lambda b, n, i: (i, n),
        b_off=b_off, shared_out=shared_out, out_view=(n_seq, seq, d),
        scratch=_flash_scratch(n_chunks, HEAD_DIM, bk, group * bq, (HEAD_DIM, group * bq)),
        semantics=("parallel", "parallel", "arbitrary"), block_bytes=blk, name="gqa_attention")


def _nat_bias_table(rel_bias, rows):
    n_heads = rel_bias.shape[0]
    kh = min(NA_WIN_H, rows)
    n_dr = 2 * NA_WIN_H - 1
    col = np.arange(GRID_W)
    col_start = np.clip(col - NA_WIN_W // 2, 0, GRID_W - NA_WIN_W)
    col_ok = (col[None, :] >= col_start[:, None]) & (col[None, :] < col_start[:, None] + NA_WIN_W)
    dc = np.clip(col[None, :] - col[:, None], -(NA_WIN_W - 1), NA_WIN_W - 1) + NA_WIN_W - 1
    planes = jnp.take(rel_bias.astype(F32) * LOG2_E, jnp.asarray(dc), axis=2)
    planes = jnp.where(jnp.asarray(col_ok)[None, None], planes, NEG_BIG)
    pad = jnp.full((n_heads, GRID_W, NAT_G * GRID_W), NEG_BIG, F32)
    wide = jnp.concatenate([pad, planes.transpose(0, 2, 1, 3).reshape(n_heads, GRID_W, n_dr * GRID_W), pad],
                           axis=2)
    blocks = []
    for r_base in (0, NAT_G, rows - NAT_G):
        a = int(np.clip(r_base - NA_WIN_H // 2, 0, rows - NAT_KROWS))
        for g in range(NAT_G):
            r = r_base + g
            r0 = int(np.clip(r - kh // 2, 0, rows - kh))
            dr0 = a - r + NA_WIN_H - 1
            assert -NAT_G <= dr0 and dr0 + NAT_KROWS <= n_dr + NAT_G
            row_ok = np.array([r0 <= a + j < r0 + kh for j in range(NAT_KROWS)])
            start = (dr0 + NAT_G) * GRID_W
            window = wide[:, :, start:start + NAT_KROWS * GRID_W]
            blocks.append(jnp.where(jnp.asarray(np.repeat(row_ok, GRID_W))[None, None], window, NEG_BIG))
    return jnp.stack(blocks, axis=1).reshape(n_heads, 3, NAT_G * GRID_W, NAT_KROWS * GRID_W)


def _nat_kernel(q_ref, k_ref, v_ref, bias_ref, o_ref, s_ref, p_ref, l_ref, *, rows):
    n_groups = rows // NAT_G
    gq = NAT_G * GRID_W
    gk = NAT_KROWS * GRID_W
    assert n_groups % 2 == 0 and n_groups >= 4

    def window(gi):
        a = jnp.clip(gi * NAT_G - NA_WIN_H // 2, 0, rows - NAT_KROWS)
        return pl.ds(pl.multiple_of(a * GRID_W, GRID_W), gk)

    def scores(gi, slot):
        var = jnp.where(gi == 0, 0, jnp.where(gi == n_groups - 1, 2, 1))
        s_ref[slot] = _dot_nt(q_ref[_rows(gi * gq, gq), :], k_ref[window(gi), :]) + bias_ref[var]

    def softmax(slot):
        s = s_ref[slot]
        e = jnp.exp2(s - jnp.max(s, axis=1, keepdims=True))
        l_ref[slot] = jnp.sum(e, axis=1, keepdims=True)
        p_ref[slot] = e.astype(BF16)

    def pv(gi, slot):
        o = jnp.dot(p_ref[slot], v_ref[window(gi), :], preferred_element_type=F32) / l_ref[slot]
        o_ref[_rows(gi * gq, gq), :] = o.astype(o_ref.dtype)

    def step(gi, parity):
        pv(gi - 1, 1 - parity)
        scores(gi + 1, 1 - parity)
        softmax(parity)

    scores(0, 0)
    scores(1, 1)
    softmax(0)

    def body(t, carry):
        step(2 * t + 1, 1)
        step(2 * t + 2, 0)
        return carry

    lax.fori_loop(0, n_groups // 2 - 1, body, 0)
    pv(n_groups - 2, 0)
    softmax(1)
    pv(n_groups - 1, 1)


def _nat_attention(qkv3, b_off, nb, shared_out, bias_tbl):
    n_seq, seq, three_d = qkv3.shape
    d = three_d // 3
    n_heads = d // HEAD_DIM
    rows = seq // GRID_W
    assert rows % NAT_G == 0 and rows >= NAT_KROWS and rows >= 3 * NAT_G
    gq, gk = NAT_G * GRID_W, NAT_KROWS * GRID_W
    blk = 2 * (4 * seq * HEAD_DIM * 2 + 3 * gq * gk * 4) + 6 * gq * gk * 4

    def col_spec(col0):
        return pl.BlockSpec((None, seq, HEAD_DIM), lambda h, b: (b + b_off, 0, col0 + h))

    return _attention_call(
        functools.partial(_nat_kernel, rows=rows),
        grid=(n_heads, nb),
        in_specs=[col_spec(0), col_spec(n_heads), col_spec(2 * n_heads),
                  pl.BlockSpec((None, 3, gq, gk), lambda h, b: (h, 0, 0, 0))],
        args=[qkv3, qkv3, qkv3, bias_tbl],
        out_block=(None, seq, HEAD_DIM), out_index=lambda h, b: (0, h),
        b_off=b_off, shared_out=shared_out, out_view=(n_seq, seq, d), seq_axis=1,
        scratch=[pltpu.VMEM((2, gq, gk), F32), pltpu.VMEM((2, gq, gk), BF16), pltpu.VMEM((2, gq, 1), F32)],
        semantics=("parallel", "parallel"), block_bytes=blk, name="nat_attention")


def _lambda_init(layer_idx):
    return 0.8 - 0.6 * math.exp(-0.3 * layer_idx)


def _seq_view(arr, seq):
    t, c = arr.shape
    return arr.reshape(t // seq, seq, c)


def _trunk(x_prompt, x_sample, layers, final_norm):
    bp, sp, d = x_prompt.shape
    bs, ss, _ = x_sample.shape
    tp, ts = bp * sp, bs * ss
    t = tp + ts
    assert tp % ss == 0 and t % sp == 0 and t % ss == 0
    groups = ((sp, 0, bp), (ss, tp // ss, bs))
    x_parts = [x_prompt.reshape(tp, d), x_sample.reshape(ts, d)]
    pos = jnp.concatenate([jnp.tile(jnp.arange(sp), bp), jnp.tile(jnp.arange(ss), bs)])

    def per_group(fn):
        out = None
        for seq, b_off, nb in groups:
            out = fn(seq, b_off, nb, out).reshape(t, d)
        return out

    xb, sumsq = _cast_stats(x_parts)
    for li, p in enumerate(layers):
        kind = p["kind"]
        w_qkv, g_mix = p["w_qkv"], p["norm_mix"]
        if kind == "nat":
            qkv = _project(xb, sumsq, w_qkv, g_mix, epilogue="qkv", n_q_cols=d, q_scale=QK_SCALE_LOG2)
            tbls = {seq: _nat_bias_table(p["rel_bias"], seq // GRID_W) for seq, _, _ in groups}
            o = per_group(lambda seq, b_off, nb, out: _nat_attention(
                _seq_view(qkv, seq), b_off, nb, out, tbls[seq]))
        elif kind == "diff":
            qkv = _project(xb, sumsq, w_qkv, g_mix, epilogue="qkv_rope", n_q_cols=d, n_k_cols=d,
                           q_scale=QK_SCALE_LOG2, rope=_partial_rope_tables(pos))
            lam_params = (p["lq1"], p["lk1"], p["lq2"], p["lk2"])
            o = per_group(lambda seq, b_off, nb, out: _diff_attention(
                _seq_view(qkv, seq), b_off, nb, out, lam_params, p["subln"], _lambda_init(li)))
        else:
            kv_dim = (w_qkv.shape[1] - d) // 2
            norm_w = jnp.stack([p["q_norm"], p["k_norm"]]).astype(F32)
            qkv = _project(xb, sumsq, w_qkv, g_mix, epilogue="qkv_rope", n_q_cols=d, n_k_cols=kv_dim,
                           q_scale=QK_SCALE_LOG2, rope=_axial_rope_tables(pos), qk_norm_w=norm_w)
            o = per_group(lambda seq, b_off, nb, out: _gqa_attention(
                _seq_view(qkv, seq), b_off, nb, out, d))
        x, xb, sumsq = _residual_matmul(o, p["w_o"], x_parts)
        u = _project(xb, sumsq, p["w_up"], p["norm_mlp"], epilogue="relu2")
        x, xb, sumsq = _residual_matmul(u, p["w_down"], [x], emit_stats=li + 1 < len(layers))
        x_parts = [x]

    y_prompt = _rmsnorm(x, final_norm, F32, row0=0, n_rows=tp)
    y_sample = _rmsnorm(x, final_norm, F32, row0=tp, n_rows=ts)
    return y_prompt.reshape(bp, sp, d), y_sample.reshape(bs, ss, d)


def kernel(x_prompt, x_sample, l0_norm_mix, l0_w_qkv, l0_rel_bias, l0_w_o, l0_norm_mlp, l0_w_up, l0_w_down, l1_norm_mix, l1_w_qkv, l1_lambda_q1, l1_lambda_k1, l1_lambda_q2, l1_lambda_k2, l1_subln, l1_w_o, l1_norm_mlp, l1_w_up, l1_w_down, l2_norm_mix, l2_w_qkv, l2_q_norm, l2_k_norm, l2_w_o, l2_norm_mlp, l2_w_up, l2_w_down, l3_norm_mix, l3_w_qkv, l3_rel_bias, l3_w_o, l3_norm_mlp, l3_w_up, l3_w_down, final_norm):
    layers = [
        {"kind": "nat", "norm_mix": l0_norm_mix, "w_qkv": l0_w_qkv, "rel_bias": l0_rel_bias, "w_o": l0_w_o,
         "norm_mlp": l0_norm_mlp, "w_up": l0_w_up, "w_down": l0_w_down},
        {"kind": "diff", "norm_mix": l1_norm_mix, "w_qkv": l1_w_qkv, "lq1": l1_lambda_q1, "lk1": l1_lambda_k1,
         "lq2": l1_lambda_q2, "lk2": l1_lambda_k2, "subln": l1_subln, "w_o": l1_w_o,
         "norm_mlp": l1_norm_mlp, "w_up": l1_w_up, "w_down": l1_w_down},
        {"kind": "gqa", "norm_mix": l2_norm_mix, "w_qkv": l2_w_qkv, "q_norm": l2_q_norm, "k_norm": l2_k_norm,
         "w_o": l2_w_o, "norm_mlp": l2_norm_mlp, "w_up": l2_w_up, "w_down": l2_w_down},
        {"kind": "nat", "norm_mix": l3_norm_mix, "w_qkv": l3_w_qkv, "rel_bias": l3_rel_bias, "w_o": l3_w_o,
         "norm_mlp": l3_norm_mlp, "w_up": l3_w_up, "w_down": l3_w_down},
    ]
    return _trunk(x_prompt, x_sample, layers, final_norm)
```
